```python
import math
import jax, jax.numpy as jnp
from jax import lax
import numpy as np

D_MODEL = 1024
BATCH = 8
SEQ = 2048
DEPTH = 2
DEC_BATCH = 128
DEC_SEQ = 1
PAST_LEN = 16384
PAGE_SIZE = 128

HG_HEADS = 4
HG_DK = 128
HG_DV = 128
HG_WIDTH = HG_HEADS * HG_DK
GLA_CHUNK = 64
GM_GROUPS = 4
GM_CHUNK = 128
GM_WIDTH = 512
GM_GC = GM_WIDTH // GM_GROUPS
S5_WIDTH = 512
S5_GROUP = 16
S5_GROUPS = S5_WIDTH // S5_GROUP
S5_STATE = 64
N_BRANCH = 3
BRANCH_W = 512
D_FF = 2816
N_IN = 4 * HG_WIDTH + 2 * GM_WIDTH + S5_WIDTH + N_BRANCH * D_MODEL
EPS = 1e-6

kernel_name = 'hybrid_hgrn2_gmlp_s5_decoder_step'

F32 = jnp.float32


def rms_norm(x, g):
    xf = x.astype(F32)
    y = xf * lax.rsqrt(jnp.mean(xf * xf, axis=-1, keepdims=True) + EPS)
    return (y * g.astype(F32)).astype(x.dtype)


def layer_norm(x, g, b):
    xf = x.astype(F32)
    mu = jnp.mean(xf, axis=-1, keepdims=True)
    var = jnp.mean(jnp.square(xf - mu), axis=-1, keepdims=True)
    y = (xf - mu) * lax.rsqrt(var + EPS) * g.astype(F32) + b.astype(F32)
    return y.astype(x.dtype)


def swiglu(x, wg, wu, wd):
    return (jax.nn.silu(x @ wg) * (x @ wu)) @ wd


def gla_chunked(q, k, v, logf, s0):
    B, L, H, K = q.shape
    V = v.shape[-1]
    c = math.gcd(L, GLA_CHUNK)
    n = L // c

    def to_chunks(a):
        return a.reshape(B, n, c, H, a.shape[-1]).swapaxes(0, 1)

    mask = jnp.tril(jnp.ones((c, c), dtype=bool))[None, :, :, None, None]

    def step(S, inp):
        qc, kc, vc, gc = inp
        b = jnp.cumsum(gc, axis=1)
        b_last = b[:, -1]
        o_inter = jnp.einsum('bthk,bhkv->bthv', qc * jnp.exp(b), S)
        rel = jnp.where(mask, b[:, :, None] - b[:, None, :], -jnp.inf)
        scores = jnp.einsum('bthk,btshk,bshk->btsh', qc, jnp.exp(rel), kc)
        o_intra = jnp.einsum('btsh,bshv->bthv', scores, vc)
        S_new = jnp.exp(b_last)[..., None] * S + jnp.einsum(
            'bshk,bshv->bhkv', kc * jnp.exp(b_last[:, None] - b), vc)
        return S_new, o_inter + o_intra

    S_fin, o = lax.scan(step, s0, (to_chunks(q), to_chunks(k), to_chunks(v), to_chunks(logf)))
    return o.swapaxes(0, 1).reshape(B, L, H, V), S_fin


def hgrn2_mixer(xq, xf, xi, xg, s0, lb, norm_g):
    B, L, _ = xq.shape

    def heads(a):
        return a.astype(F32).reshape(B, L, HG_HEADS, -1)

    zf = heads(xf)
    lbh = lb.astype(F32).reshape(HG_HEADS, HG_DK)
    logf = jnp.logaddexp(jnp.log(lbh), jnp.log1p(-lbh) + jax.nn.log_sigmoid(zf))
    k = (1.0 - lbh) * jax.nn.sigmoid(-zf)
    o, S = gla_chunked(heads(xq), k, heads(xi), logf, s0.astype(F32))
    o = o * lax.rsqrt(jnp.mean(o * o, axis=-1, keepdims=True) + EPS)
    o = o * norm_g.astype(F32).reshape(HG_HEADS, HG_DV)
    o = o.reshape(B, L, HG_WIDTH) * jax.nn.silu(xg.astype(F32))
    return o.astype(xq.dtype), S


def gmlp_mixer(xu, xv, ws, bs, ng, nb):
    B, L, _ = xu.shape
    u = jax.nn.gelu(xu)
    v = layer_norm(jax.nn.gelu(xv), ng, nb)
    n = -(-L // GM_CHUNK)
    pad = n * GM_CHUNK - L
    vp = jnp.pad(v, ((0, 0), (0, pad), (0, 0))).reshape(B, n, GM_CHUNK, GM_GROUPS, GM_GC)
    wm = ws * jnp.tril(jnp.ones((GM_CHUNK, GM_CHUNK), dtype=ws.dtype))
    mix = jnp.einsum('gts,bnsgc->bntgc', wm, vp) + bs.T[None, None, :, :, None]
    mix = mix.reshape(B, n * GM_CHUNK, GM_WIDTH)[:, :L]
    return u * mix, v


def _ssm_combine(c1, c2):
    a1, b1 = c1
    a2, b2 = c2
    return a1 * a2, a2 * b1 + b2


def s5_mixer(u, h0_re, h0_im, lam_re, lam_im, log_dt, b_re, b_im, c_re, c_im, d_skip, glu_w, glu_b):
    Bsz, L, _ = u.shape
    uf = u.astype(F32).reshape(Bsz, L, S5_GROUPS, S5_GROUP)
    lam = lax.complex(lam_re.astype(F32), lam_im.astype(F32))
    dt = jnp.exp(log_dt.astype(F32))[:, None]
    abar = jnp.exp(lam * dt)
    bbar = ((abar - 1.0) / lam)[..., None] * lax.complex(b_re.astype(F32), b_im.astype(F32))
    bu = lax.complex(jnp.einsum('gph,blgh->blgp', jnp.real(bbar), uf),
                     jnp.einsum('gph,blgh->blgp', jnp.imag(bbar), uf))
    h0 = lax.complex(h0_re.astype(F32), h0_im.astype(F32))
    bu = bu.at[:, 0].add(abar * h0)
    a = jnp.broadcast_to(abar, bu.shape)
    _, h = lax.associative_scan(_ssm_combine, (a, bu), axis=1)
    y = (jnp.einsum('ghp,blgp->blgh', c_re.astype(F32), jnp.real(h))
         - jnp.einsum('ghp,blgp->blgh', c_im.astype(F32), jnp.imag(h)))
    y = y.reshape(Bsz, L, S5_WIDTH) + d_skip.astype(F32) * u.astype(F32)
    z = jax.nn.gelu(y).astype(u.dtype)
    zz = z @ glu_w + glu_b
    out = zz[..., :S5_WIDTH] * jax.nn.sigmoid(zz[..., S5_WIDTH:])
    h_last = h[:, -1]
    return out, jnp.real(h_last), jnp.imag(h_last)


def decoder_layer(x, s_hg, h_re, h_im, lb, norm_g, ffn_w_gate, ffn_w_up, ffn_w_down, w_in,
                  hgrn_norm_g, gmlp_ws, gmlp_bs, gmlp_norm_g, gmlp_norm_b,
                  s5_lam_re, s5_lam_im, s5_log_dt, s5_b_re, s5_b_im, s5_c_re, s5_c_im, s5_d,
                  s5_glu_w, s5_glu_b, w_branch, w_out):
    B, L, _ = x.shape
    h = rms_norm(x, norm_g[0])
    x = x + 0.5 * rms_norm(swiglu(h, ffn_w_gate[0], ffn_w_up[0], ffn_w_down[0]), norm_g[1])
    h = rms_norm(x, norm_g[2])
    proj = h @ w_in
    sizes = [HG_WIDTH] * 4 + [GM_WIDTH] * 2 + [S5_WIDTH] + [N_BRANCH * D_MODEL]
    xq, xf, xi, xg, gu, gv, su, gate_cols = jnp.split(proj, np.cumsum(sizes)[:-1].tolist(), axis=-1)
    a_out, s_hg_new = hgrn2_mixer(xq, xf, xi, xg, s_hg, lb, hgrn_norm_g)
    b_out, v_rows = gmlp_mixer(gu, gv, gmlp_ws, gmlp_bs, gmlp_norm_g, gmlp_norm_b)
    c_out, h_re_new, h_im_new = s5_mixer(su, h_re, h_im, s5_lam_re, s5_lam_im, s5_log_dt,
                                         s5_b_re, s5_b_im, s5_c_re, s5_c_im, s5_d, s5_glu_w, s5_glu_b)
    branches = jnp.stack([a_out, b_out, c_out], axis=2)
    bproj = jnp.einsum('blnw,nwd->blnd', branches, w_branch)
    gates = jax.nn.sigmoid(gate_cols).reshape(B, L, N_BRANCH, D_MODEL)
    merged = jnp.einsum('blnd,blnd->bld', gates, bproj)
    x = x + rms_norm(merged @ w_out, norm_g[3])
    h = rms_norm(x, norm_g[4])
    x = x + 0.5 * rms_norm(swiglu(h, ffn_w_gate[1], ffn_w_up[1], ffn_w_down[1]), norm_g[5])
    return x, s_hg_new, h_re_new, h_im_new, v_rows


def setup_inputs(seed: int = 0) -> dict:
    key = jax.random.key(seed)
    ks = jax.random.split(key, 32)

    def nrm(k, shape, scale):
        return scale * jax.random.normal(k, shape, dtype=F32)

    P = S5_STATE
    G = S5_GROUPS
    return {
        'x_prompt': nrm(ks[0], (BATCH, SEQ, D_MODEL), 1.0),
        'x_sample': nrm(ks[1], (DEC_BATCH, DEC_SEQ, D_MODEL), 1.0),
        'state_hgrn': nrm(ks[2], (DEPTH, DEC_BATCH, HG_HEADS, HG_DK, HG_DV), 0.3),
        'state_s5_re': nrm(ks[3], (DEPTH, DEC_BATCH, G, P), 0.1),
        'state_s5_im': nrm(ks[4], (DEPTH, DEC_BATCH, G, P), 0.1),
        'norm_g': 1.0 + nrm(ks[5], (DEPTH, 6, D_MODEL), 0.05),
        'ffn_w_gate': nrm(ks[6], (DEPTH, 2, D_MODEL, D_FF), D_MODEL ** -0.5),
        'ffn_w_up': nrm(ks[7], (DEPTH, 2, D_MODEL, D_FF), D_MODEL ** -0.5),
        'ffn_w_down': nrm(ks[8], (DEPTH, 2, D_FF, D_MODEL), D_FF ** -0.5),
        'w_in': nrm(ks[9], (DEPTH, D_MODEL, N_IN), D_MODEL ** -0.5),
        'hgrn_lb_logits': 1.0 + nrm(ks[10], (DEPTH, HG_WIDTH), 0.1),
        'hgrn_norm_g': 1.0 + nrm(ks[11], (DEPTH, HG_WIDTH), 0.05),
        'gmlp_ws': nrm(ks[12], (DEPTH, GM_GROUPS, GM_CHUNK, GM_CHUNK), GM_CHUNK ** -0.5),
        'gmlp_bs': 1.0 + nrm(ks[13], (DEPTH, GM_GROUPS, GM_CHUNK), 0.01),
        'gmlp_norm_g': 1.0 + nrm(ks[14], (DEPTH, GM_WIDTH), 0.05),
        'gmlp_norm_b': nrm(ks[15], (DEPTH, GM_WIDTH), 0.01),
        's5_lam_re': -0.5 + nrm(ks[16], (DEPTH, G, P), 0.01),
        's5_lam_im': jnp.pi * jnp.arange(P, dtype=F32)[None, None, :] + nrm(ks[17], (DEPTH, G, P), 0.01),
        's5_log_dt': jax.random.uniform(ks[18], (DEPTH, G), dtype=F32,
                                        minval=math.log(1e-3), maxval=math.log(1e-1)),
        's5_b_re': nrm(ks[19], (DEPTH, G, P, S5_GROUP), (2 * S5_GROUP) ** -0.5),
        's5_b_im': nrm(ks[20], (DEPTH, G, P, S5_GROUP), (2 * S5_GROUP) ** -0.5),
        's5_c_re': nrm(ks[21], (DEPTH, G, S5_GROUP, P), (2 * P) ** -0.5),
        's5_c_im': nrm(ks[22], (DEPTH, G, S5_GROUP, P), (2 * P) ** -0.5),
        's5_d': nrm(ks[23], (DEPTH, S5_WIDTH), 0.5),
        's5_glu_w': nrm(ks[24], (DEPTH, S5_WIDTH, 2 * S5_WIDTH), S5_WIDTH ** -0.5),
        's5_glu_b': nrm(ks[25], (DEPTH, 2 * S5_WIDTH), 0.01),
        'w_branch': nrm(ks[26], (DEPTH, N_BRANCH, BRANCH_W, D_MODEL), BRANCH_W ** -0.5),
        'w_out': nrm(ks[27], (DEPTH, D_MODEL, D_MODEL), D_MODEL ** -0.5),
    }


def reference(x_prompt, x_sample, state_hgrn, state_s5_re, state_s5_im, norm_g, ffn_w_gate, ffn_w_up,
              ffn_w_down, w_in, hgrn_lb_logits, hgrn_norm_g, gmlp_ws, gmlp_bs, gmlp_norm_g, gmlp_norm_b,
              s5_lam_re, s5_lam_im, s5_log_dt, s5_b_re, s5_b_im, s5_c_re, s5_c_im, s5_d, s5_glu_w,
              s5_glu_b, w_branch, w_out):
    lb_all = jnp.cumsum(jax.nn.softmax(hgrn_lb_logits.astype(F32), axis=0), axis=0)
    lb_all = lb_all - lb_all[:1]
    yp = x_prompt
    ys = x_sample
    hg_p, re_p, im_p, hg_s, re_s, im_s, v_s = [], [], [], [], [], [], []
    for l in range(DEPTH):
        p = (lb_all[l], norm_g[l], ffn_w_gate[l], ffn_w_up[l], ffn_w_down[l], w_in[l],
             hgrn_norm_g[l], gmlp_ws[l], gmlp_bs[l], gmlp_norm_g[l], gmlp_norm_b[l],
             s5_lam_re[l], s5_lam_im[l], s5_log_dt[l], s5_b_re[l], s5_b_im[l], s5_c_re[l], s5_c_im[l],
             s5_d[l], s5_glu_w[l], s5_glu_b[l], w_branch[l], w_out[l])
        zero_hg = jnp.zeros((BATCH, HG_HEADS, HG_DK, HG_DV), F32)
        zero_s5 = jnp.zeros((BATCH, S5_GROUPS, S5_STATE), F32)
        yp, s1, r1, i1, _ = decoder_layer(yp, zero_hg, zero_s5, zero_s5, *p)
        ys, s2, r2, i2, v2 = decoder_layer(ys, state_hgrn[l], state_s5_re[l], state_s5_im[l], *p)
        hg_p.append(s1)
        re_p.append(r1)
        im_p.append(i1)
        hg_s.append(s2)
        re_s.append(r2)
        im_s.append(i2)
        v_s.append(v2)
    return (yp, ys, jnp.stack(hg_p), jnp.stack(re_p), jnp.stack(im_p),
            jnp.stack(hg_s), jnp.stack(re_s), jnp.stack(im_s), jnp.stack(v_s))
```

```python
import functools
import math

import jax
import jax.numpy as jnp
from jax import lax
from jax.experimental import pallas as pl
from jax.experimental.pallas import tpu as pltpu

F32 = jnp.float32
BF16 = jnp.bfloat16

D_MODEL = 1024
D_FF = 2816
HG_HEADS = 4
HG_DK = 128
HG_WIDTH = HG_HEADS * HG_DK
GM_GROUPS = 4
GM_CHUNK = 128
GM_WIDTH = 512
S5_WIDTH = 512
S5_GROUP = 16
S5_GROUPS = S5_WIDTH // S5_GROUP
S5_STATE = 64
S5_N = S5_GROUPS * S5_STATE
S5_BLOCKS = 4
N_BRANCH = 3
EPS = 1e-6

GLA_CHUNK = 32
VMEM_LIMIT = 56 * 1024 * 1024


def _rms(x, g):
    return x * lax.rsqrt(jnp.mean(x * x, axis=-1, keepdims=True) + EPS) * g


def _gelu(x):
    c = math.sqrt(2.0 / math.pi)
    return 0.5 * x * (1.0 + jnp.tanh(c * (x + 0.044715 * (x * x * x))))


def _sigmoid(x):
    return 1.0 / (1.0 + jnp.exp(-x))


def _dot(a, b):
    return jnp.dot(a, b, preferred_element_type=F32)


def _dot_nt(a, b):
    return lax.dot_general(a, b, (((1,), (1,)), ((), ())), preferred_element_type=F32)


def _dot_tn(a, b):
    return lax.dot_general(a, b, (((0,), (0,)), ((), ())), preferred_element_type=F32)


def _const_spec(shape):
    n = len(shape)
    return pl.BlockSpec(shape, lambda *_: (0,) * n, pipeline_mode=pl.Buffered(1))


def _params(n_grid):
    return pltpu.CompilerParams(
        dimension_semantics=("arbitrary",) * n_grid, vmem_limit_bytes=VMEM_LIMIT)


def _ffn_body(x, g_pre, g_post, wg_ref, wu_ref, wd_ref):
    h = _rms(x, g_pre).astype(BF16)
    a = _dot(h, wg_ref[...])
    u = _dot(h, wu_ref[...])
    m = (a * _sigmoid(a) * u).astype(BF16)
    d = _dot(m, wd_ref[...])
    return x + 0.5 * _rms(d, g_post)


def _ffn_kernel(x_ref, gpre_ref, gpost_ref, wg_ref, wu_ref, wd_ref, o_ref):
    o_ref[...] = _ffn_body(x_ref[...], gpre_ref[...], gpost_ref[...], wg_ref, wu_ref, wd_ref)


def _ffn_call(x2d, g_pre, g_post, wg, wu, wd, *, grid, tm, in_map, out_map, out_shape):
    return pl.pallas_call(
        _ffn_kernel,
        grid=grid,
        in_specs=[
            pl.BlockSpec((tm, D_MODEL), in_map),
            _const_spec((1, D_MODEL)),
            _const_spec((1, D_MODEL)),
            _const_spec((D_MODEL, D_FF)),
            _const_spec((D_MODEL, D_FF)),
            _const_spec((D_FF, D_MODEL)),
        ],
        out_specs=pl.BlockSpec((tm, D_MODEL), out_map),
        out_shape=jax.ShapeDtypeStruct(out_shape, F32),
        compiler_params=_params(len(grid)),
        name="ffn",
    )(x2d, g_pre, g_post, wg, wu, wd)


def _hgrn_gates(z, lb):
    ls = jnp.minimum(z, 0.0) - jnp.log1p(jnp.exp(-jnp.abs(z)))
    c = jnp.log1p(-lb) + ls
    a = jnp.log(jnp.maximum(lb, 1e-37))
    lae = jnp.maximum(a, c) + jnp.log1p(jnp.exp(-jnp.abs(a - c)))
    logf = jnp.where(lb > 0.0, lae, c)
    k = (1.0 - lb) * _sigmoid(-z)
    return logf, k


def _hgrn_out(o, xg, ng):
    parts = []
    for hd in range(HG_HEADS):
        oh = o[:, hd * HG_DK:(hd + 1) * HG_DK]
        parts.append(oh * lax.rsqrt(jnp.mean(oh * oh, axis=-1, keepdims=True) + EPS))
    on = jnp.concatenate(parts, axis=-1)
    return on * ng * (xg * _sigmoid(xg))


def _lower_bound(logits_ref, layer):
    lg = logits_ref[...]
    m = jnp.max(lg, axis=0, keepdims=True)
    e = jnp.exp(lg - m)
    den = jnp.sum(e, axis=0, keepdims=True)
    lb = jnp.zeros_like(den)
    for j in range(1, layer + 1):
        lb = lb + e[j:j + 1] / den
    return lb


def _hgrn_prompt_kernel(layer, tl, x_ref, g_ref, w_ref, logits_ref, ng_ref, tri_ref,
                        a_ref, sfin_ref, st_ref, q_s, k_s, v_s, b_s, o_s):
    t = pl.program_id(1)
    nt = pl.num_programs(1)

    @pl.when(t == 0)
    def _():
        st_ref[...] = jnp.zeros_like(st_ref)

    h = _rms(x_ref[...], g_ref[...]).astype(BF16)
    p = _dot(h, w_ref[...])
    xq = p[:, 0 * HG_WIDTH:1 * HG_WIDTH]
    z = p[:, 1 * HG_WIDTH:2 * HG_WIDTH]
    xi = p[:, 2 * HG_WIDTH:3 * HG_WIDTH]
    xg = p[:, 3 * HG_WIDTH:4 * HG_WIDTH]
    lb = _lower_bound(logits_ref, layer)
    logf, kk = _hgrn_gates(z, lb)

    tri = tri_ref[...]
    l0 = logf.astype(BF16)
    r1 = logf - l0.astype(F32)
    l1 = r1.astype(BF16)
    l2 = (r1 - l1.astype(F32)).astype(BF16)
    b = _dot(tri, l0) + _dot(tri, l1) + _dot(tri, l2)

    q_s[...] = xq
    k_s[...] = kk
    v_s[...] = xi
    b_s[...] = b

    c = GLA_CHUNK
    mid = c // 2 - 1
    row = lax.broadcasted_iota(jnp.int32, (c, c), 0)
    col = lax.broadcasted_iota(jnp.int32, (c, c), 1)
    causal = col <= row

    def chunk(ci, carry):
        r = pl.ds(pl.multiple_of(ci * c, c), c)
        for hd in range(HG_HEADS):
            cs = slice(hd * HG_DK, (hd + 1) * HG_DK)
            bq = b_s[r, cs]
            qc = q_s[r, cs]
            kc = k_s[r, cs]
            vc = v_s[r, cs].astype(BF16)
            b_mid = bq[mid:mid + 1]
            b_last = bq[c - 1:c]
            st = st_ref[hd]
            q_in = (qc * jnp.exp(bq)).astype(BF16)
            q_md = (qc * jnp.exp(bq - b_mid)).astype(BF16)
            k_md = (kc * jnp.exp(b_mid - bq)).astype(BF16)
            k_ls = (kc * jnp.exp(b_last - bq)).astype(BF16)
            sc = jnp.where(causal, _dot_nt(q_md, k_md), 0.0).astype(BF16)
            o = _dot_nt(q_in, st.astype(BF16)) + _dot(sc, vc)
            o_s[r, cs] = o
            st_ref[hd] = st * jnp.exp(b_last) + _dot_tn(vc, k_ls)
        return carry

    lax.fori_loop(0, tl // c, chunk, 0)

    a_ref[...] = _hgrn_out(o_s[...], xg, ng_ref[...]).astype(BF16)

    @pl.when(t == nt - 1)
    def _():
        for hd in range(HG_HEADS):
            sfin_ref[0, hd] = st_ref[hd].T


def _hgrn_prompt_call(x_tm, g, w_hg, logits, ng, *, layer, nb, seq, tl):
    c = GLA_CHUNK
    idx = jnp.arange(tl)
    tri = ((idx[:, None] >= idx[None, :]) & (idx[:, None] // c == idx[None, :] // c)).astype(BF16)
    depth = logits.shape[0]
    return pl.pallas_call(
        functools.partial(_hgrn_prompt_kernel, layer, tl),
        grid=(nb, seq // tl),
        in_specs=[
            pl.BlockSpec((tl, D_MODEL), lambda b, t: (t, b)),
            _const_spec((1, D_MODEL)),
            _const_spec((D_MODEL, 4 * HG_WIDTH)),
            _const_spec((depth, HG_WIDTH)),
            _const_spec((1, HG_WIDTH)),
            _const_spec((tl, tl)),
        ],
        out_specs=[
            pl.BlockSpec((tl, HG_WIDTH), lambda b, t: (t, b)),
            pl.BlockSpec((1, HG_HEADS, HG_DK, HG_DK), lambda b, t: (b, 0, 0, 0)),
        ],
        out_shape=[
            jax.ShapeDtypeStruct((seq, nb * HG_WIDTH), BF16),
            jax.ShapeDtypeStruct((nb, HG_HEADS, HG_DK, HG_DK), F32),
        ],
        scratch_shapes=[
            pltpu.VMEM((HG_HEADS, HG_DK, HG_DK), F32),
            pltpu.VMEM((tl, HG_WIDTH), F32),
            pltpu.VMEM((tl, HG_WIDTH), F32),
            pltpu.VMEM((tl, HG_WIDTH), F32),
            pltpu.VMEM((tl, HG_WIDTH), F32),
            pltpu.VMEM((tl, HG_WIDTH), F32),
        ],
        compiler_params=_params(2),
        name="hgrn_prompt",
    )(x_tm, g, w_hg, logits, ng, tri)


def _gmlp_uv(x, g, w_ref, lng, lnb):
    h = _rms(x, g).astype(BF16)
    p = _dot(h, w_ref[...])
    u = _gelu(p[:, :GM_WIDTH])
    gv = _gelu(p[:, GM_WIDTH:])
    mu = jnp.mean(gv, axis=-1, keepdims=True)
    dv = gv - mu
    var = jnp.mean(dv * dv, axis=-1, keepdims=True)
    v = dv * lax.rsqrt(var + EPS) * lng + lnb
    return u, v


def _gmlp_prompt_kernel(tg, x_ref, g_ref, w_ref, lng_ref, lnb_ref, wm_ref, bst_ref, o_ref):
    u, v = _gmlp_uv(x_ref[...], g_ref[...], w_ref, lng_ref[...], lnb_ref[...])
    vb = v.astype(BF16)
    gc = GM_WIDTH // GM_GROUPS
    rows = []
    for j in range(tg // GM_CHUNK):
        parts = []
        for gi in range(GM_GROUPS):
            vj = vb[j * GM_CHUNK:(j + 1) * GM_CHUNK, gi * gc:(gi + 1) * gc]
            parts.append(_dot(wm_ref[gi], vj) + bst_ref[:, gi:gi + 1])
        rows.append(jnp.concatenate(parts, axis=-1))
    mix = jnp.concatenate(rows, axis=0)
    o_ref[...] = (u * mix).astype(BF16)


def _gmlp_prompt_call(x_tm, g, w_gm, lng, lnb, wm, bst, *, nb, seq, tg):
    return pl.pallas_call(
        functools.partial(_gmlp_prompt_kernel, tg),
        grid=(nb, seq // tg),
        in_specs=[
            pl.BlockSpec((tg, D_MODEL), lambda b, t: (t, b)),
            _const_spec((1, D_MODEL)),
            _const_spec((D_MODEL, 2 * GM_WIDTH)),
            _const_spec((1, GM_WIDTH)),
            _const_spec((1, GM_WIDTH)),
            _const_spec((GM_GROUPS, GM_CHUNK, GM_CHUNK)),
            _const_spec((GM_CHUNK, GM_GROUPS)),
        ],
        out_specs=pl.BlockSpec((tg, GM_WIDTH), lambda b, t: (t, b)),
        out_shape=jax.ShapeDtypeStruct((seq, nb * GM_WIDTH), BF16),
        compiler_params=_params(2),
        name="gmlp_prompt",
    )(x_tm, g, w_gm, lng, lnb, wm, bst)


def _s5_kernel(nb, tt, has_h0, *refs):
    if has_h0:
        (x_ref, g_ref, w_ref, are_ref, aim_ref, bre_ref, bim_ref, cre_ref, cim_ref, d_ref,
         gw_ref, gb_ref, h0re_ref, h0im_ref, c_ref, hre_ref, him_ref, sre, sim) = refs
    else:
        (x_ref, g_ref, w_ref, are_ref, aim_ref, bre_ref, bim_ref, cre_ref, cim_ref, d_ref,
         gw_ref, gb_ref, c_ref, hre_ref, him_ref, sre, sim) = refs
    t = pl.program_id(0)
    nt = pl.num_programs(0)

    @pl.when(t == 0)
    def _():
        if has_h0:
            hre_ref[...] = h0re_ref[...]
            him_ref[...] = h0im_ref[...]
        else:
            hre_ref[...] = jnp.zeros_like(hre_ref)
            him_ref[...] = jnp.zeros_like(him_ref)

    h = _rms(x_ref[...], g_ref[...]).astype(BF16)
    su = _dot(h, w_ref[...])
    sub = su.astype(BF16)
    wblk = S5_WIDTH // S5_BLOCKS
    sblk = S5_N // S5_BLOCKS
    for j in range(S5_BLOCKS):
        uj = sub[:, j * wblk:(j + 1) * wblk]
        sre[:, j * sblk:(j + 1) * sblk] = _dot(uj, bre_ref[j])
        sim[:, j * sblk:(j + 1) * sblk] = _dot(uj, bim_ref[j])

    a_re = jnp.broadcast_to(are_ref[...], (nb, S5_N))
    a_im = jnp.broadcast_to(aim_ref[...], (nb, S5_N))

    def step(i, carry):
        h_re, h_im = carry
        r = pl.ds(pl.multiple_of(i * nb, nb), nb)
        n_re = a_re * h_re - a_im * h_im + sre[r, :]
        n_im = a_re * h_im + a_im * h_re + sim[r, :]
        sre[r, :] = n_re
        sim[r, :] = n_im
        return n_re, n_im

    h_re, h_im = lax.fori_loop(0, tt, step, (hre_ref[...], him_ref[...]))
    hre_ref[...] = h_re
    him_ref[...] = h_im

    ys = []
    for j in range(S5_BLOCKS):
        hr = sre[:, j * sblk:(j + 1) * sblk].astype(BF16)
        hi = sim[:, j * sblk:(j + 1) * sblk].astype(BF16)
        ys.append(_dot(hr, cre_ref[j]) - _dot(hi, cim_ref[j]))
    y = jnp.concatenate(ys, axis=-1) + d_ref[...] * su
    zz = _dot(_gelu(y).astype(BF16), gw_ref[...]) + gb_ref[...]
    c_ref[...] = (zz[:, :S5_WIDTH] * _sigmoid(zz[:, S5_WIDTH:])).astype(BF16)


def _s5_call(x_rows, g, w_s5, a_re, a_im, b_re, b_im, c_re, c_im, d, gw, gb, h0=None,
             *, nb, seq, tt):
    rows = tt * nb
    wblk = S5_WIDTH // S5_BLOCKS
    sblk = S5_N // S5_BLOCKS
    in_specs = [
        pl.BlockSpec((rows, D_MODEL), lambda t: (t, 0)),
        _const_spec((1, D_MODEL)),
        _const_spec((D_MODEL, S5_WIDTH)),
        _const_spec((1, S5_N)),
        _const_spec((1, S5_N)),
        _const_spec((S5_BLOCKS, wblk, sblk)),
        _const_spec((S5_BLOCKS, wblk, sblk)),
        _const_spec((S5_BLOCKS, sblk, wblk)),
        _const_spec((S5_BLOCKS, sblk, wblk)),
        _const_spec((1, S5_WIDTH)),
        _const_spec((S5_WIDTH, 2 * S5_WIDTH)),
        _const_spec((1, 2 * S5_WIDTH)),
    ]
    args = [x_rows, g, w_s5, a_re, a_im, b_re, b_im, c_re, c_im, d, gw, gb]
    if h0 is not None:
        in_specs += [_const_spec((nb, S5_N)), _const_spec((nb, S5_N))]
        args += list(h0)
    state_spec = pl.BlockSpec((nb, S5_N), lambda t: (0, 0))
    return pl.pallas_call(
        functools.partial(_s5_kernel, nb, tt, h0 is not None),
        grid=(seq // tt,),
        in_specs=in_specs,
        out_specs=[pl.BlockSpec((rows, S5_WIDTH), lambda t: (t, 0)), state_spec, state_spec],
        out_shape=[
            jax.ShapeDtypeStruct((seq * nb, S5_WIDTH), BF16),
            jax.ShapeDtypeStruct((nb, S5_N), F32),
            jax.ShapeDtypeStruct((nb, S5_N), F32),
        ],
        scratch_shapes=[pltpu.VMEM((rows, S5_N), F32), pltpu.VMEM((rows, S5_N), F32)],
        compiler_params=_params(1),
        name="s5",
    )(*args)


def _merge_kernel(x_ref, a_ref, b_ref, c_ref, g2_ref, g3_ref, wgate_ref, wbr_ref, wout_ref, o_ref):
    x = x_ref[...]
    h = _rms(x, g2_ref[...]).astype(BF16)
    gates = _sigmoid(_dot(h, wgate_ref[...]))
    merged = None
    for n, br in enumerate((a_ref, b_ref, c_ref)):
        term = gates[:, n * D_MODEL:(n + 1) * D_MODEL] * _dot(br[...], wbr_ref[n])
        merged = term if merged is None else merged + term
    o_ref[...] = x + _rms(_dot(merged.astype(BF16), wout_ref[...]), g3_ref[...])


def _merge_call(x2d, a, b, c, g2, g3, wgate, wbr, wout, *, grid, tm, xmap, bmap, out_shape):
    return pl.pallas_call(
        _merge_kernel,
        grid=grid,
        in_specs=[
            pl.BlockSpec((tm, D_MODEL), xmap),
            pl.BlockSpec((tm, HG_WIDTH), bmap),
            pl.BlockSpec((tm, GM_WIDTH), bmap),
            pl.BlockSpec((tm, S5_WIDTH), bmap),
            _const_spec((1, D_MODEL)),
            _const_spec((1, D_MODEL)),
            _const_spec((D_MODEL, N_BRANCH * D_MODEL)),
            _const_spec((N_BRANCH, HG_WIDTH, D_MODEL)),
            _const_spec((D_MODEL, D_MODEL)),
        ],
        out_specs=pl.BlockSpec((tm, D_MODEL), xmap),
        out_shape=jax.ShapeDtypeStruct(out_shape, F32),
        compiler_params=_params(len(grid)),
        name="merge",
    )(x2d, a, b, c, g2, g3, wgate, wbr, wout)


def _sample_proj_kernel(layer, x_ref, g_ref, whg_ref, wgm_ref, logits_ref, lng_ref, lnb_ref,
                        w00_ref, b00_ref, q_ref, f_ref, k_ref, v_ref, xg_ref, bout_ref, vrow_ref):
    x = x_ref[...]
    h = _rms(x, g_ref[...]).astype(BF16)
    p = _dot(h, whg_ref[...])
    z = p[:, 1 * HG_WIDTH:2 * HG_WIDTH]
    logf, kk = _hgrn_gates(z, _lower_bound(logits_ref, layer))
    q_ref[...] = p[:, 0 * HG_WIDTH:1 * HG_WIDTH]
    f_ref[...] = jnp.exp(logf)
    k_ref[...] = kk
    v_ref[...] = p[:, 2 * HG_WIDTH:3 * HG_WIDTH]
    xg_ref[...] = p[:, 3 * HG_WIDTH:4 * HG_WIDTH]
    u, v = _gmlp_uv(x, g_ref[...], wgm_ref, lng_ref[...], lnb_ref[...])
    bout_ref[...] = (u * (v * w00_ref[...] + b00_ref[...])).astype(BF16)
    vrow_ref[...] = v


def _sample_proj_call(x, g, w_hg, w_gm, logits, lng, lnb, w00, b00, *, layer, n):
    depth = logits.shape[0]
    f32_out = jax.ShapeDtypeStruct((n, HG_WIDTH), F32)
    return pl.pallas_call(
        functools.partial(_sample_proj_kernel, layer),
        grid=(1,),
        in_specs=[
            _const_spec((n, D_MODEL)),
            _const_spec((1, D_MODEL)),
            _const_spec((D_MODEL, 4 * HG_WIDTH)),
            _const_spec((D_MODEL, 2 * GM_WIDTH)),
            _const_spec((depth, HG_WIDTH)),
            _const_spec((1, GM_WIDTH)),
            _const_spec((1, GM_WIDTH)),
            _const_spec((1, GM_WIDTH)),
            _const_spec((1, GM_WIDTH)),
        ],
        out_specs=[pl.BlockSpec((n, HG_WIDTH), lambda i: (0, 0))] * 7,
        out_shape=[f32_out] * 5 + [jax.ShapeDtypeStruct((n, GM_WIDTH), BF16), f32_out],
        compiler_params=_params(1),
        name="sample_proj",
    )(x, g, w_hg, w_gm, logits, lng, lnb, w00, b00)


def _hgrn_step_kernel(bb, s_ref, qc_ref, fc_ref, kc_ref, v_ref, xg_ref, ng_ref, snew_ref, a_ref, o_s):
    for i in range(bb):
        for hd in range(HG_HEADS):
            cs = slice(hd * HG_DK, (hd + 1) * HG_DK)
            f_col = fc_ref[0, hd, :, i:i + 1]
            k_col = kc_ref[0, hd, :, i:i + 1]
            q_col = qc_ref[0, hd, :, i:i + 1]
            v_row = v_ref[i:i + 1, cs]
            s_new = f_col * s_ref[i, hd] + k_col * v_row
            snew_ref[i, hd] = s_new
            o_s[i:i + 1, cs] = jnp.sum(q_col * s_new, axis=0, keepdims=True)
    a_ref[...] = _hgrn_out(o_s[...], xg_ref[...], ng_ref[...]).astype(BF16)


def _hgrn_step_call(s0, qc, fc, kc, v, xg, ng, *, n, bb):
    col_spec = pl.BlockSpec((1, HG_HEADS, HG_DK, bb), lambda i: (i, 0, 0, 0))
    row_spec = pl.BlockSpec((bb, HG_WIDTH), lambda i: (i, 0))
    st_spec = pl.BlockSpec((bb, HG_HEADS, HG_DK, HG_DK), lambda i: (i, 0, 0, 0))
    return pl.pallas_call(
        functools.partial(_hgrn_step_kernel, bb),
        grid=(n // bb,),
        in_specs=[st_spec, col_spec, col_spec, col_spec, row_spec, row_spec,
                  _const_spec((1, HG_WIDTH))],
        out_specs=[st_spec, row_spec],
        out_shape=[
            jax.ShapeDtypeStruct((n, HG_HEADS, HG_DK, HG_DK), F32),
            jax.ShapeDtypeStruct((n, HG_WIDTH), BF16),
        ],
        scratch_shapes=[pltpu.VMEM((bb, HG_WIDTH), F32)],
        compiler_params=_params(1),
        name="hgrn_step",
    )(s0, qc, fc, kc, v, xg, ng)


def _s5_params(lam_re, lam_im, log_dt, b_re, b_im, c_re, c_im):
    lam = lax.complex(lam_re.astype(F32), lam_im.astype(F32))
    dt = jnp.exp(log_dt.astype(F32))[:, None]
    abar = jnp.exp(lam * dt)
    bbar = ((abar - 1.0) / lam)[..., None] * lax.complex(b_re.astype(F32), b_im.astype(F32))
    gpb = S5_GROUPS // S5_BLOCKS
    eye = jnp.eye(gpb, dtype=F32)

    def pack_in(w):
        w = w.reshape(S5_BLOCKS, gpb, S5_STATE, S5_GROUP)
        out = jnp.einsum('jgph,gk->jghkp', w, eye)
        return out.reshape(S5_BLOCKS, gpb * S5_GROUP, gpb * S5_STATE).astype(BF16)

    def pack_out(w):
        w = w.reshape(S5_BLOCKS, gpb, S5_GROUP, S5_STATE)
        out = jnp.einsum('jghp,gk->jgpkh', w, eye)
        return out.reshape(S5_BLOCKS, gpb * S5_STATE, gpb * S5_GROUP).astype(BF16)

    return (jnp.real(abar).reshape(1, S5_N), jnp.imag(abar).reshape(1, S5_N),
            pack_in(jnp.real(bbar)), pack_in(jnp.imag(bbar)),
            pack_out(c_re.astype(F32)), pack_out(c_im.astype(F32)))


def kernel(x_prompt, x_sample, state_hgrn, state_s5_re, state_s5_im, norm_g, ffn_w_gate, ffn_w_up, ffn_w_down, w_in, hgrn_lb_logits, hgrn_norm_g, gmlp_ws, gmlp_bs, gmlp_norm_g, gmlp_norm_b, s5_lam_re, s5_lam_im, s5_log_dt, s5_b_re, s5_b_im, s5_c_re, s5_c_im, s5_d, s5_glu_w, s5_glu_b, w_branch, w_out):
    nb, seq, _ = x_prompt.shape
    ns = x_sample.shape[0]
    depth = norm_g.shape[0]
    tm = 512
    tl = 256
    tg = 512
    tt = 64
    bb = 8
    nt = seq // tm

    bmajor = lambda b, t: (b * nt + t, 0)
    tmajor = lambda b, t: (t, b)
    rows1 = lambda i: (i, 0)

    yp = x_prompt.reshape(nb * seq, D_MODEL)
    ys = x_sample.reshape(ns, D_MODEL)
    logits = hgrn_lb_logits.astype(F32)
    tril = jnp.tril(jnp.ones((GM_CHUNK, GM_CHUNK), F32))
    o_hg = 4 * HG_WIDTH
    o_gm = o_hg + 2 * GM_WIDTH
    o_s5 = o_gm + S5_WIDTH

    outs = {k: [] for k in ('hg_p', 're_p', 'im_p', 'hg_s', 're_s', 'im_s', 'v_s')}
    for l in range(depth):
        g = [norm_g[l, i].reshape(1, D_MODEL) for i in range(6)]
        wg = ffn_w_gate[l].astype(BF16)
        wu = ffn_w_up[l].astype(BF16)
        wd = ffn_w_down[l].astype(BF16)
        w_hg = w_in[l, :, :o_hg].astype(BF16)
        w_gm = w_in[l, :, o_hg:o_gm].astype(BF16)
        w_s5 = w_in[l, :, o_gm:o_s5].astype(BF16)
        w_gate = w_in[l, :, o_s5:].astype(BF16)
        wbr = w_branch[l].astype(BF16)
        wout = w_out[l].astype(BF16)
        ng = hgrn_norm_g[l].reshape(1, HG_WIDTH)
        lng = gmlp_norm_g[l].reshape(1, GM_WIDTH)
        lnb = gmlp_norm_b[l].reshape(1, GM_WIDTH)
        wm = (gmlp_ws[l] * tril).astype(BF16)
        bst = gmlp_bs[l].T
        gc = GM_WIDTH // GM_GROUPS
        w00 = jnp.repeat(gmlp_ws[l][:, 0, 0], gc).reshape(1, GM_WIDTH)
        b00 = jnp.repeat(gmlp_bs[l][:, 0], gc).reshape(1, GM_WIDTH)
        s5p = _s5_params(s5_lam_re[l], s5_lam_im[l], s5_log_dt[l], s5_b_re[l], s5_b_im[l],
                         s5_c_re[l], s5_c_im[l])
        s5d = s5_d[l].reshape(1, S5_WIDTH)
        gw = s5_glu_w[l].astype(BF16)
        gb = s5_glu_b[l].reshape(1, 2 * S5_WIDTH)

        x1 = _ffn_call(yp, g[0], g[1], wg[0], wu[0], wd[0], grid=(nb, nt), tm=tm,
                       in_map=bmajor if l == 0 else tmajor, out_map=tmajor,
                       out_shape=(seq, nb * D_MODEL))
        a_p, s_hg = _hgrn_prompt_call(x1, g[2], w_hg, logits, ng, layer=l, nb=nb, seq=seq, tl=tl)
        b_p = _gmlp_prompt_call(x1, g[2], w_gm, lng, lnb, wm, bst, nb=nb, seq=seq, tg=tg)
        c_p, h_re, h_im = _s5_call(x1.reshape(seq * nb, D_MODEL), g[2], w_s5, *s5p, s5d, gw, gb,
                                   nb=nb, seq=seq, tt=tt)
        x2 = _merge_call(x1, a_p, b_p, c_p.reshape(seq, nb * S5_WIDTH), g[2], g[3], w_gate, wbr, wout,
                         grid=(nb, nt), tm=tm, xmap=tmajor, bmap=tmajor,
                         out_shape=(seq, nb * D_MODEL))
        last = l == depth - 1
        yp = _ffn_call(x2, g[4], g[5], wg[1], wu[1], wd[1], grid=(nb, nt), tm=tm,
                       in_map=tmajor, out_map=bmajor if last else tmajor,
                       out_shape=(nb * seq, D_MODEL) if last else (seq, nb * D_MODEL))
        outs['hg_p'].append(s_hg)
        outs['re_p'].append(h_re.reshape(nb, S5_GROUPS, S5_STATE))
        outs['im_p'].append(h_im.reshape(nb, S5_GROUPS, S5_STATE))

        xs1 = _ffn_call(ys, g[0], g[1], wg[0], wu[0], wd[0], grid=(1,), tm=ns,
                        in_map=rows1, out_map=rows1, out_shape=(ns, D_MODEL))
        q, f, k, v, xg, b_s, v_rows = _sample_proj_call(
            xs1, g[2], w_hg, w_gm, logits, lng, lnb, w00, b00, layer=l, n=ns)

        def cols(a):
            return a.reshape(ns // bb, bb, HG_HEADS, HG_DK).transpose(0, 2, 3, 1)

        s_new, a_s = _hgrn_step_call(state_hgrn[l], cols(q), cols(f), cols(k), v, xg, ng, n=ns, bb=bb)
        h0 = (state_s5_re[l].reshape(ns, S5_N), state_s5_im[l].reshape(ns, S5_N))
        c_s, hs_re, hs_im = _s5_call(xs1, g[2], w_s5, *s5p, s5d, gw, gb, h0, nb=ns, seq=1, tt=1)
        xs2 = _merge_call(xs1, a_s, b_s, c_s, g[2], g[3], w_gate, wbr, wout,
                          grid=(1,), tm=ns, xmap=rows1, bmap=rows1, out_shape=(ns, D_MODEL))
        ys = _ffn_call(xs2, g[4], g[5], wg[1], wu[1], wd[1], grid=(1,), tm=ns,
                       in_map=rows1, out_map=rows1, out_shape=(ns, D_MODEL))
        outs['hg_s'].append(s_new)
        outs['re_s'].append(hs_re.reshape(ns, S5_GROUPS, S5_STATE))
        outs['im_s'].append(hs_im.reshape(ns, S5_GROUPS, S5_STATE))
        outs['v_s'].append(v_rows.reshape(ns, 1, GM_WIDTH))

    return (yp.reshape(nb, seq, D_MODEL), ys.reshape(ns, 1, D_MODEL),
            jnp.stack(outs['hg_p']), jnp.stack(outs['re_p']), jnp.stack(outs['im_p']),
            jnp.stack(outs['hg_s']), jnp.stack(outs['re_s']), jnp.stack(outs['im_s']),
            jnp.stack(outs['v_s']))
```

```python
import functools
import math

import jax
import jax.numpy as jnp
from jax import lax
from jax.experimental import pallas as pl
from jax.experimental.pallas import tpu as pltpu

F32 = jnp.float32
BF16 = jnp.bfloat16

D_MODEL = 1024
D_FF = 2816
HG_HEADS = 4
HG_DK = 128
HG_WIDTH = HG_HEADS * HG_DK
GM_GROUPS = 4
GM_CHUNK = 128
GM_WIDTH = 512
S5_WIDTH = 512
S5_GROUP = 16
S5_GROUPS = S5_WIDTH // S5_GROUP
S5_STATE = 64
S5_N = S5_GROUPS * S5_STATE
S5_BLOCKS = 4
N_BRANCH = 3
EPS = 1e-6

GLA_CHUNK = 32
VMEM_LIMIT = 56 * 1024 * 1024


def _rms(x, g):
    return x * lax.rsqrt(jnp.mean(x * x, axis=-1, keepdims=True) + EPS) * g


def _gelu(x):
    c = math.sqrt(2.0 / math.pi)
    return 0.5 * x * (1.0 + jnp.tanh(c * (x + 0.044715 * (x * x * x))))


def _sigmoid(x):
    return 1.0 / (1.0 + jnp.exp(-x))


def _dot(a, b):
    return jnp.dot(a, b, preferred_element_type=F32)


def _dot_nt(a, b):
    return lax.dot_general(a, b, (((1,), (1,)), ((), ())), preferred_element_type=F32)


def _dot_tn(a, b):
    return lax.dot_general(a, b, (((0,), (0,)), ((), ())), preferred_element_type=F32)


def _const_spec(shape):
    n = len(shape)
    return pl.BlockSpec(shape, lambda *_: (0,) * n, pipeline_mode=pl.Buffered(1))


def _params(n_grid):
    return pltpu.CompilerParams(
        dimension_semantics=("arbitrary",) * n_grid, vmem_limit_bytes=VMEM_LIMIT)


def _ffn_body(x, g_pre, g_post, wg_ref, wu_ref, wd_ref):
    h = _rms(x, g_pre).astype(BF16)
    a = _dot(h, wg_ref[...])
    u = _dot(h, wu_ref[...])
    m = (a * _sigmoid(a) * u).astype(BF16)
    d = _dot(m, wd_ref[...])
    return x + 0.5 * _rms(d, g_post)


def _ffn_kernel(x_ref, gpre_ref, gpost_ref, wg_ref, wu_ref, wd_ref, o_ref):
    o_ref[...] = _ffn_body(x_ref[...], gpre_ref[...], gpost_ref[...], wg_ref, wu_ref, wd_ref)


def _ffn_call(x2d, g_pre, g_post, wg, wu, wd, *, grid, tm, in_map, out_map, out_shape):
    return pl.pallas_call(
        _ffn_kernel,
        grid=grid,
        in_specs=[
            pl.BlockSpec((tm, D_MODEL), in_map),
            _const_spec((1, D_MODEL)),
            _const_spec((1, D_MODEL)),
            _const_spec((D_MODEL, D_FF)),
            _const_spec((D_MODEL, D_FF)),
            _const_spec((D_FF, D_MODEL)),
        ],
        out_specs=pl.BlockSpec((tm, D_MODEL), out_map),
        out_shape=jax.ShapeDtypeStruct(out_shape, F32),
        compiler_params=_params(len(grid)),
        name="ffn",
    )(x2d, g_pre, g_post, wg, wu, wd)


def _hgrn_gates(z, lb):
    ls = jnp.minimum(z, 0.0) - jnp.log1p(jnp.exp(-jnp.abs(z)))
    c = jnp.log1p(-lb) + ls
    a = jnp.log(jnp.maximum(lb, 1e-37))
    lae = jnp.maximum(a, c) + jnp.log1p(jnp.exp(-jnp.abs(a - c)))
    logf = jnp.where(lb > 0.0, lae, c)
    k = (1.0 - lb) * _sigmoid(-z)
    return logf, k


def _hgrn_out(o, xg, ng):
    parts = []
    for hd in range(HG_HEADS):
        oh = o[:, hd * HG_DK:(hd + 1) * HG_DK]
        parts.append(oh * lax.rsqrt(jnp.mean(oh * oh, axis=-1, keepdims=True) + EPS))
    on = jnp.concatenate(parts, axis=-1)
    return on * ng * (xg * _sigmoid(xg))


def _lower_bound(logits_ref, layer):
    lg = logits_ref[...]
    m = jnp.max(lg, axis=0, keepdims=True)
    e = jnp.exp(lg - m)
    den = jnp.sum(e, axis=0, keepdims=True)
    lb = jnp.zeros_like(den)
    for j in range(1, layer + 1):
        lb = lb + e[j:j + 1] / den
    return lb


def _hgrn_prompt_kernel(layer, nb, tl, x_ref, g_ref, w_ref, logits_ref, ng_ref, tri_ref,
                        a_ref, sfin_ref, st_ref, q_s, k_s, v_s, b_s, o_s):
    t = pl.program_id(0)
    nt = pl.num_programs(0)

    @pl.when(t == 0)
    def _():
        st_ref[...] = jnp.zeros_like(st_ref)

    xs = jnp.concatenate([x_ref[:, b * D_MODEL:(b + 1) * D_MODEL] for b in range(nb)], axis=0)
    h = _rms(xs, g_ref[...]).astype(BF16)
    p = _dot(h, w_ref[...])
    xq = p[:, 0 * HG_WIDTH:1 * HG_WIDTH]
    z = p[:, 1 * HG_WIDTH:2 * HG_WIDTH]
    xi = p[:, 2 * HG_WIDTH:3 * HG_WIDTH]
    xg = p[:, 3 * HG_WIDTH:4 * HG_WIDTH]
    lb = _lower_bound(logits_ref, layer)
    logf, kk = _hgrn_gates(z, lb)

    tri = tri_ref[...]
    tr = tri.shape[0]
    l0 = logf.astype(BF16)
    r1 = logf - l0.astype(F32)
    l1 = r1.astype(BF16)
    l2 = (r1 - l1.astype(F32)).astype(BF16)
    for i in range(nb * tl // tr):
        rs = slice(i * tr, (i + 1) * tr)
        b_s[rs, :] = _dot(tri, l0[rs]) + _dot(tri, l1[rs]) + _dot(tri, l2[rs])

    q_s[...] = xq
    k_s[...] = kk
    v_s[...] = xi

    c = GLA_CHUNK
    mid = c // 2 - 1
    row = lax.broadcasted_iota(jnp.int32, (c, c), 0)
    col = lax.broadcasted_iota(jnp.int32, (c, c), 1)
    causal = col <= row

    def chunk(ci, carry):
        pending = []
        for b in range(nb):
            r = pl.ds(pl.multiple_of(b * tl + ci * c, c), c)
            for hd in range(HG_HEADS):
                cs = slice(hd * HG_DK, (hd + 1) * HG_DK)
                bq = b_s[r, cs]
                qc = q_s[r, cs]
                kc = k_s[r, cs]
                vc = v_s[r, cs].astype(BF16)
                b_mid = bq[mid:mid + 1]
                b_last = bq[c - 1:c]
                st = st_ref[b, hd]
                q_in = (qc * jnp.exp(bq)).astype(BF16)
                q_md = (qc * jnp.exp(bq - b_mid)).astype(BF16)
                k_md = (kc * jnp.exp(b_mid - bq)).astype(BF16)
                k_ls = (kc * jnp.exp(b_last - bq)).astype(BF16)
                scores = _dot_nt(q_md, k_md)
                o_inter = _dot_nt(q_in, st.astype(BF16))
                st_ref[b, hd] = st * jnp.exp(b_last) + _dot_tn(vc, k_ls)
                pending.append((r, cs, scores, o_inter, vc))
        for r, cs, scores, o_inter, vc in pending:
            sc = jnp.where(causal, scores, 0.0).astype(BF16)
            o_s[r, cs] = o_inter + _dot(sc, vc)
        return carry

    lax.fori_loop(0, tl // c, chunk, 0)

    a = _hgrn_out(o_s[...], xg, ng_ref[...]).astype(BF16)
    for b in range(nb):
        a_ref[:, b * HG_WIDTH:(b + 1) * HG_WIDTH] = a[b * tl:(b + 1) * tl]

    @pl.when(t == nt - 1)
    def _():
        for b in range(nb):
            for hd in range(HG_HEADS):
                sfin_ref[b, hd] = st_ref[b, hd].T


def _hgrn_prompt_call(x_tm, g, w_hg, logits, ng, *, layer, nb, seq, tl):
    c = GLA_CHUNK
    tr = min(256, nb * tl)
    idx = jnp.arange(tr)
    tri = ((idx[:, None] >= idx[None, :]) & (idx[:, None] // c == idx[None, :] // c)).astype(BF16)
    depth = logits.shape[0]
    rows = nb * tl
    return pl.pallas_call(
        functools.partial(_hgrn_prompt_kernel, layer, nb, tl),
        grid=(seq // tl,),
        in_specs=[
            pl.BlockSpec((tl, nb * D_MODEL), lambda t: (t, 0)),
            _const_spec((1, D_MODEL)),
            _const_spec((D_MODEL, 4 * HG_WIDTH)),
            _const_spec((depth, HG_WIDTH)),
            _const_spec((1, HG_WIDTH)),
            _const_spec((tr, tr)),
        ],
        out_specs=[
            pl.BlockSpec((tl, nb * HG_WIDTH), lambda t: (t, 0)),
            pl.BlockSpec((nb, HG_HEADS, HG_DK, HG_DK), lambda t: (0, 0, 0, 0)),
        ],
        out_shape=[
            jax.ShapeDtypeStruct((seq, nb * HG_WIDTH), BF16),
            jax.ShapeDtypeStruct((nb, HG_HEADS, HG_DK, HG_DK), F32),
        ],
        scratch_shapes=[
            pltpu.VMEM((nb, HG_HEADS, HG_DK, HG_DK), F32),
            pltpu.VMEM((rows, HG_WIDTH), F32),
            pltpu.VMEM((rows, HG_WIDTH), F32),
            pltpu.VMEM((rows, HG_WIDTH), F32),
            pltpu.VMEM((rows, HG_WIDTH), F32),
            pltpu.VMEM((rows, HG_WIDTH), F32),
        ],
        compiler_params=_params(1),
        name="hgrn_prompt",
    )(x_tm, g, w_hg, logits, ng, tri)


def _gmlp_uv(x, g, w_ref, lng, lnb):
    h = _rms(x, g).astype(BF16)
    p = _dot(h, w_ref[...])
    u = _gelu(p[:, :GM_WIDTH])
    gv = _gelu(p[:, GM_WIDTH:])
    mu = jnp.mean(gv, axis=-1, keepdims=True)
    dv = gv - mu
    var = jnp.mean(dv * dv, axis=-1, keepdims=True)
    v = dv * lax.rsqrt(var + EPS) * lng + lnb
    return u, v


def _gmlp_prompt_kernel(tg, x_ref, g_ref, w_ref, lng_ref, lnb_ref, wm_ref, bst_ref, o_ref):
    u, v = _gmlp_uv(x_ref[...], g_ref[...], w_ref, lng_ref[...], lnb_ref[...])
    vb = v.astype(BF16)
    gc = GM_WIDTH // GM_GROUPS
    rows = []
    for j in range(tg // GM_CHUNK):
        parts = []
        for gi in range(GM_GROUPS):
            vj = vb[j * GM_CHUNK:(j + 1) * GM_CHUNK, gi * gc:(gi + 1) * gc]
            parts.append(_dot(wm_ref[gi], vj) + bst_ref[:, gi:gi + 1])
        rows.append(jnp.concatenate(parts, axis=-1))
    mix = jnp.concatenate(rows, axis=0)
    o_ref[...] = (u * mix).astype(BF16)


def _gmlp_prompt_call(x_tm, g, w_gm, lng, lnb, wm, bst, *, nb, seq, tg):
    return pl.pallas_call(
        functools.partial(_gmlp_prompt_kernel, tg),
        grid=(nb, seq // tg),
        in_specs=[
            pl.BlockSpec((tg, D_MODEL), lambda b, t: (t, b)),
            _const_spec((1, D_MODEL)),
            _const_spec((D_MODEL, 2 * GM_WIDTH)),
            _const_spec((1, GM_WIDTH)),
            _const_spec((1, GM_WIDTH)),
            _const_spec((GM_GROUPS, GM_CHUNK, GM_CHUNK)),
            _const_spec((GM_CHUNK, GM_GROUPS)),
        ],
        out_specs=pl.BlockSpec((tg, GM_WIDTH), lambda b, t: (t, b)),
        out_shape=jax.ShapeDtypeStruct((seq, nb * GM_WIDTH), BF16),
        compiler_params=_params(2),
        name="gmlp_prompt",
    )(x_tm, g, w_gm, lng, lnb, wm, bst)


def _s5_kernel(nb, tt, has_h0, relayout, *refs):
    (x_ref, g_ref, w_ref, are_ref, aim_ref, bre_ref, bim_ref, cre_ref, cim_ref, d_ref,
     gw_ref, gb_ref) = refs[:12]
    h0re_ref, h0im_ref = refs[12:14] if has_h0 else (None, None)
    c_ref, hre_ref, him_ref, sre, sim, lay = refs[-6:]
    t = pl.program_id(0)
    wblk = S5_WIDTH // S5_BLOCKS
    sblk = S5_N // S5_BLOCKS

    @pl.when(t == 0)
    def _():
        if has_h0:
            hre_ref[...] = h0re_ref[...]
            him_ref[...] = h0im_ref[...]
        else:
            hre_ref[...] = jnp.zeros_like(hre_ref)
            him_ref[...] = jnp.zeros_like(him_ref)

    if relayout:
        xs = jnp.concatenate([x_ref[:, b * D_MODEL:(b + 1) * D_MODEL] for b in range(nb)], axis=0)
    else:
        xs = x_ref[...]
    h = _rms(xs, g_ref[...]).astype(BF16)
    su = _dot(h, w_ref[...])
    if relayout:
        for b in range(nb):
            for j in range(S5_BLOCKS):
                lay[j, pl.ds(b, tt, stride=nb), :] = su[b * tt:(b + 1) * tt, j * wblk:(j + 1) * wblk]
        su = jnp.concatenate([lay[j] for j in range(S5_BLOCKS)], axis=-1)
    sub = su.astype(BF16)
    for j in range(S5_BLOCKS):
        uj = sub[:, j * wblk:(j + 1) * wblk]
        sre[:, j * sblk:(j + 1) * sblk] = _dot(uj, bre_ref[j])
        sim[:, j * sblk:(j + 1) * sblk] = _dot(uj, bim_ref[j])

    a_re = jnp.broadcast_to(are_ref[...], (nb, S5_N))
    a_im = jnp.broadcast_to(aim_ref[...], (nb, S5_N))

    def step(i, carry):
        h_re, h_im = carry
        r = pl.ds(pl.multiple_of(i * nb, nb), nb)
        n_re = a_re * h_re - a_im * h_im + sre[r, :]
        n_im = a_re * h_im + a_im * h_re + sim[r, :]
        sre[r, :] = n_re
        sim[r, :] = n_im
        return n_re, n_im

    h_re, h_im = lax.fori_loop(0, tt, step, (hre_ref[...], him_ref[...]))
    hre_ref[...] = h_re
    him_ref[...] = h_im

    ys = []
    for j in range(S5_BLOCKS):
        hr = sre[:, j * sblk:(j + 1) * sblk].astype(BF16)
        hi = sim[:, j * sblk:(j + 1) * sblk].astype(BF16)
        ys.append(_dot(hr, cre_ref[j]) - _dot(hi, cim_ref[j]))
    y = jnp.concatenate(ys, axis=-1) + d_ref[...] * su
    zz = _dot(_gelu(y).astype(BF16), gw_ref[...]) + gb_ref[...]
    c = zz[:, :S5_WIDTH] * _sigmoid(zz[:, S5_WIDTH:])
    if relayout:
        for j in range(S5_BLOCKS):
            lay[j] = c[:, j * wblk:(j + 1) * wblk]
        for b in range(nb):
            cb = jnp.concatenate([lay[j, pl.ds(b, tt, stride=nb), :] for j in range(S5_BLOCKS)], axis=-1)
            c_ref[:, b * S5_WIDTH:(b + 1) * S5_WIDTH] = cb.astype(BF16)
    else:
        c_ref[...] = c.astype(BF16)


def _s5_call(x2d, g, w_s5, a_re, a_im, b_re, b_im, c_re, c_im, d, gw, gb, h0=None,
             *, nb, seq, tt):
    rows = tt * nb
    relayout = seq > 1
    wblk = S5_WIDTH // S5_BLOCKS
    sblk = S5_N // S5_BLOCKS
    x_block = (tt, nb * D_MODEL) if relayout else (rows, D_MODEL)
    c_block = (tt, nb * S5_WIDTH) if relayout else (rows, S5_WIDTH)
    in_specs = [
        pl.BlockSpec(x_block, lambda t: (t, 0)),
        _const_spec((1, D_MODEL)),
        _const_spec((D_MODEL, S5_WIDTH)),
        _const_spec((1, S5_N)),
        _const_spec((1, S5_N)),
        _const_spec((S5_BLOCKS, wblk, sblk)),
        _const_spec((S5_BLOCKS, wblk, sblk)),
        _const_spec((S5_BLOCKS, sblk, wblk)),
        _const_spec((S5_BLOCKS, sblk, wblk)),
        _const_spec((1, S5_WIDTH)),
        _const_spec((S5_WIDTH, 2 * S5_WIDTH)),
        _const_spec((1, 2 * S5_WIDTH)),
    ]
    args = [x2d, g, w_s5, a_re, a_im, b_re, b_im, c_re, c_im, d, gw, gb]
    if h0 is not None:
        in_specs += [_const_spec((nb, S5_N)), _const_spec((nb, S5_N))]
        args += list(h0)
    state_spec = pl.BlockSpec((nb, S5_N), lambda t: (0, 0))
    c_shape = (seq, nb * S5_WIDTH) if relayout else (nb, S5_WIDTH)
    return pl.pallas_call(
        functools.partial(_s5_kernel, nb, tt, h0 is not None, relayout),
        grid=(seq // tt,),
        in_specs=in_specs,
        out_specs=[pl.BlockSpec(c_block, lambda t: (t, 0)), state_spec, state_spec],
        out_shape=[
            jax.ShapeDtypeStruct(c_shape, BF16),
            jax.ShapeDtypeStruct((nb, S5_N), F32),
            jax.ShapeDtypeStruct((nb, S5_N), F32),
        ],
        scratch_shapes=[pltpu.VMEM((rows, S5_N), F32), pltpu.VMEM((rows, S5_N), F32),
                        pltpu.VMEM((S5_BLOCKS, rows, wblk), F32)],
        compiler_params=_params(1),
        name="s5",
    )(*args)


def _merge_kernel(x_ref, a_ref, b_ref, c_ref, g2_ref, g3_ref, wgate_ref, wbr_ref, wout_ref, o_ref):
    x = x_ref[...]
    h = _rms(x, g2_ref[...]).astype(BF16)
    gates = _sigmoid(_dot(h, wgate_ref[...]))
    merged = None
    for n, br in enumerate((a_ref, b_ref, c_ref)):
        term = gates[:, n * D_MODEL:(n + 1) * D_MODEL] * _dot(br[...], wbr_ref[n])
        merged = term if merged is None else merged + term
    o_ref[...] = x + _rms(_dot(merged.astype(BF16), wout_ref[...]), g3_ref[...])


def _merge_call(x2d, a, b, c, g2, g3, wgate, wbr, wout, *, grid, tm, xmap, bmap, out_shape):
    return pl.pallas_call(
        _merge_kernel,
        grid=grid,
        in_specs=[
            pl.BlockSpec((tm, D_MODEL), xmap),
            pl.BlockSpec((tm, HG_WIDTH), bmap),
            pl.BlockSpec((tm, GM_WIDTH), bmap),
            pl.BlockSpec((tm, S5_WIDTH), bmap),
            _const_spec((1, D_MODEL)),
            _const_spec((1, D_MODEL)),
            _const_spec((D_MODEL, N_BRANCH * D_MODEL)),
            _const_spec((N_BRANCH, HG_WIDTH, D_MODEL)),
            _const_spec((D_MODEL, D_MODEL)),
        ],
        out_specs=pl.BlockSpec((tm, D_MODEL), xmap),
        out_shape=jax.ShapeDtypeStruct(out_shape, F32),
        compiler_params=_params(len(grid)),
        name="merge",
    )(x2d, a, b, c, g2, g3, wgate, wbr, wout)


def _sample_proj_kernel(layer, x_ref, g_ref, whg_ref, wgm_ref, logits_ref, lng_ref, lnb_ref,
                        w00_ref, b00_ref, q_ref, f_ref, k_ref, v_ref, xg_ref, bout_ref, vrow_ref):
    x = x_ref[...]
    h = _rms(x, g_ref[...]).astype(BF16)
    p = _dot(h, whg_ref[...])
    z = p[:, 1 * HG_WIDTH:2 * HG_WIDTH]
    logf, kk = _hgrn_gates(z, _lower_bound(logits_ref, layer))
    q_ref[...] = p[:, 0 * HG_WIDTH:1 * HG_WIDTH]
    f_ref[...] = jnp.exp(logf)
    k_ref[...] = kk
    v_ref[...] = p[:, 2 * HG_WIDTH:3 * HG_WIDTH]
    xg_ref[...] = p[:, 3 * HG_WIDTH:4 * HG_WIDTH]
    u, v = _gmlp_uv(x, g_ref[...], wgm_ref, lng_ref[...], lnb_ref[...])
    bout_ref[...] = (u * (v * w00_ref[...] + b00_ref[...])).astype(BF16)
    vrow_ref[...] = v


def _sample_proj_call(x, g, w_hg, w_gm, logits, lng, lnb, w00, b00, *, layer, n):
    depth = logits.shape[0]
    f32_out = jax.ShapeDtypeStruct((n, HG_WIDTH), F32)
    return pl.pallas_call(
        functools.partial(_sample_proj_kernel, layer),
        grid=(1,),
        in_specs=[
            _const_spec((n, D_MODEL)),
            _const_spec((1, D_MODEL)),
            _const_spec((D_MODEL, 4 * HG_WIDTH)),
            _const_spec((D_MODEL, 2 * GM_WIDTH)),
            _const_spec((depth, HG_WIDTH)),
            _const_spec((1, GM_WIDTH)),
            _const_spec((1, GM_WIDTH)),
            _const_spec((1, GM_WIDTH)),
            _const_spec((1, GM_WIDTH)),
        ],
        out_specs=[pl.BlockSpec((n, HG_WIDTH), lambda i: (0, 0))] * 7,
        out_shape=[f32_out] * 5 + [jax.ShapeDtypeStruct((n, GM_WIDTH), BF16), f32_out],
        compiler_params=_params(1),
        name="sample_proj",
    )(x, g, w_hg, w_gm, logits, lng, lnb, w00, b00)


def _hgrn_step_kernel(bb, *refs):
    s_ref, qc_ref, fc_ref, kc_ref, v_ref, xg_ref, ng_ref = refs[:7]
    snew_ref, a_ref, o_s = refs[-3:]
    for i in range(bb):
        for hd in range(HG_HEADS):
            cs = slice(hd * HG_DK, (hd + 1) * HG_DK)
            f_col = fc_ref[0, hd, :, i:i + 1]
            k_col = kc_ref[0, hd, :, i:i + 1]
            q_col = qc_ref[0, hd, :, i:i + 1]
            v_row = v_ref[i:i + 1, cs]
            s_new = f_col * s_ref[i, hd] + k_col * v_row
            snew_ref[i, hd] = s_new
            o_s[i:i + 1, cs] = jnp.sum(q_col * s_new, axis=0, keepdims=True)
    a_ref[...] = _hgrn_out(o_s[...], xg_ref[...], ng_ref[...]).astype(BF16)


def _hgrn_step_call(s_all, prev, qc, fc, kc, v, xg, ng, *, layer, n, bb):
    col_spec = pl.BlockSpec((1, HG_HEADS, HG_DK, bb), lambda i: (i, 0, 0, 0))
    row_spec = pl.BlockSpec((bb, HG_WIDTH), lambda i: (i, 0))
    st_spec = pl.BlockSpec((None, bb, HG_HEADS, HG_DK, HG_DK), lambda i: (layer, i, 0, 0, 0))
    in_specs = [st_spec, col_spec, col_spec, col_spec, row_spec, row_spec, _const_spec((1, HG_WIDTH))]
    args = [s_all, qc, fc, kc, v, xg, ng]
    aliases = {}
    if prev is not None:
        in_specs.append(pl.BlockSpec(memory_space=pl.ANY))
        args.append(prev)
        aliases = {len(args) - 1: 0}
    return pl.pallas_call(
        functools.partial(_hgrn_step_kernel, bb),
        grid=(n // bb,),
        in_specs=in_specs,
        out_specs=[st_spec, row_spec],
        out_shape=[
            jax.ShapeDtypeStruct(s_all.shape, F32),
            jax.ShapeDtypeStruct((n, HG_WIDTH), BF16),
        ],
        scratch_shapes=[pltpu.VMEM((bb, HG_WIDTH), F32)],
        input_output_aliases=aliases,
        compiler_params=_params(1),
        name="hgrn_step",
    )(*args)


def _s5_params(lam_re, lam_im, log_dt, b_re, b_im, c_re, c_im):
    lr = lam_re.astype(F32)
    li = lam_im.astype(F32)
    dt = jnp.exp(log_dt.astype(F32))[:, None]
    mag = jnp.exp(lr * dt)
    a_re = mag * jnp.cos(li * dt)
    a_im = mag * jnp.sin(li * dt)
    den = lr * lr + li * li
    k_re = ((a_re - 1.0) * lr + a_im * li) / den
    k_im = (a_im * lr - (a_re - 1.0) * li) / den
    bre = b_re.astype(F32)
    bim = b_im.astype(F32)
    bb_re = k_re[..., None] * bre - k_im[..., None] * bim
    bb_im = k_re[..., None] * bim + k_im[..., None] * bre
    gpb = S5_GROUPS // S5_BLOCKS
    eye = jnp.eye(gpb, dtype=F32)

    def pack_in(w):
        w = w.reshape(S5_BLOCKS, gpb, S5_STATE, S5_GROUP)
        out = jnp.einsum('jgph,gk->jghkp', w, eye)
        return out.reshape(S5_BLOCKS, gpb * S5_GROUP, gpb * S5_STATE).astype(BF16)

    def pack_out(w):
        w = w.reshape(S5_BLOCKS, gpb, S5_GROUP, S5_STATE)
        out = jnp.einsum('jghp,gk->jgpkh', w, eye)
        return out.reshape(S5_BLOCKS, gpb * S5_STATE, gpb * S5_GROUP).astype(BF16)

    return (a_re.reshape(1, S5_N), a_im.reshape(1, S5_N), pack_in(bb_re), pack_in(bb_im),
            pack_out(c_re.astype(F32)), pack_out(c_im.astype(F32)))


def kernel(x_prompt, x_sample, state_hgrn, state_s5_re, state_s5_im, norm_g, ffn_w_gate, ffn_w_up, ffn_w_down, w_in, hgrn_lb_logits, hgrn_norm_g, gmlp_ws, gmlp_bs, gmlp_norm_g, gmlp_norm_b, s5_lam_re, s5_lam_im, s5_log_dt, s5_b_re, s5_b_im, s5_c_re, s5_c_im, s5_d, s5_glu_w, s5_glu_b, w_branch, w_out):
    nb, seq, _ = x_prompt.shape
    ns = x_sample.shape[0]
    depth = norm_g.shape[0]
    tm = 512
    tl = 128
    tg = 512
    tt = 64
    bb = 8
    nt = seq // tm

    bmajor = lambda b, t: (b * nt + t, 0)
    tmajor = lambda b, t: (t, b)
    rows1 = lambda i: (i, 0)

    yp = x_prompt.reshape(nb * seq, D_MODEL)
    ys = x_sample.reshape(ns, D_MODEL)
    logits = hgrn_lb_logits.astype(F32)
    tril = jnp.tril(jnp.ones((GM_CHUNK, GM_CHUNK), F32))
    o_hg = 4 * HG_WIDTH
    o_gm = o_hg + 2 * GM_WIDTH
    o_s5 = o_gm + S5_WIDTH

    outs = {k: [] for k in ('hg_p', 're_p', 'im_p', 're_s', 'im_s', 'v_s')}
    hg_s = None
    for l in range(depth):
        g = [norm_g[l, i].reshape(1, D_MODEL) for i in range(6)]
        wg = ffn_w_gate[l].astype(BF16)
        wu = ffn_w_up[l].astype(BF16)
        wd = ffn_w_down[l].astype(BF16)
        w_hg = w_in[l, :, :o_hg].astype(BF16)
        w_gm = w_in[l, :, o_hg:o_gm].astype(BF16)
        w_s5 = w_in[l, :, o_gm:o_s5].astype(BF16)
        w_gate = w_in[l, :, o_s5:].astype(BF16)
        wbr = w_branch[l].astype(BF16)
        wout = w_out[l].astype(BF16)
        ng = hgrn_norm_g[l].reshape(1, HG_WIDTH)
        lng = gmlp_norm_g[l].reshape(1, GM_WIDTH)
        lnb = gmlp_norm_b[l].reshape(1, GM_WIDTH)
        wm = (gmlp_ws[l] * tril).astype(BF16)
        bst = gmlp_bs[l].T
        gc = GM_WIDTH // GM_GROUPS
        w00 = jnp.repeat(gmlp_ws[l][:, 0, 0], gc).reshape(1, GM_WIDTH)
        b00 = jnp.repeat(gmlp_bs[l][:, 0], gc).reshape(1, GM_WIDTH)
        s5p = _s5_params(s5_lam_re[l], s5_lam_im[l], s5_log_dt[l], s5_b_re[l], s5_b_im[l],
                         s5_c_re[l], s5_c_im[l])
        s5d = s5_d[l].reshape(1, S5_WIDTH)
        gw = s5_glu_w[l].astype(BF16)
        gb = s5_glu_b[l].reshape(1, 2 * S5_WIDTH)

        x1 = _ffn_call(yp, g[0], g[1], wg[0], wu[0], wd[0], grid=(nb, nt), tm=tm,
                       in_map=bmajor if l == 0 else tmajor, out_map=tmajor,
                       out_shape=(seq, nb * D_MODEL))
        a_p, s_hg = _hgrn_prompt_call(x1, g[2], w_hg, logits, ng, layer=l, nb=nb, seq=seq, tl=tl)
        b_p = _gmlp_prompt_call(x1, g[2], w_gm, lng, lnb, wm, bst, nb=nb, seq=seq, tg=tg)
        c_p, h_re, h_im = _s5_call(x1, g[2], w_s5, *s5p, s5d, gw, gb, nb=nb, seq=seq, tt=tt)
        x2 = _merge_call(x1, a_p, b_p, c_p, g[2], g[3], w_gate, wbr, wout,
                         grid=(nb, nt), tm=tm, xmap=tmajor, bmap=tmajor,
                         out_shape=(seq, nb * D_MODEL))
        last = l == depth - 1
        yp = _ffn_call(x2, g[4], g[5], wg[1], wu[1], wd[1], grid=(nb, nt), tm=tm,
                       in_map=tmajor, out_map=bmajor if last else tmajor,
                       out_shape=(nb * seq, D_MODEL) if last else (seq, nb * D_MODEL))
        outs['hg_p'].append(s_hg)
        outs['re_p'].append(h_re.reshape(nb, S5_GROUPS, S5_STATE))
        outs['im_p'].append(h_im.reshape(nb, S5_GROUPS, S5_STATE))

        xs1 = _ffn_call(ys, g[0], g[1], wg[0], wu[0], wd[0], grid=(1,), tm=ns,
                        in_map=rows1, out_map=rows1, out_shape=(ns, D_MODEL))
        q, f, k, v, xg, b_s, v_rows = _sample_proj_call(
            xs1, g[2], w_hg, w_gm, logits, lng, lnb, w00, b00, layer=l, n=ns)

        def cols(a):
            return a.reshape(ns // bb, bb, HG_HEADS, HG_DK).transpose(0, 2, 3, 1)

        hg_s, a_s = _hgrn_step_call(state_hgrn, hg_s, cols(q), cols(f), cols(k), v, xg, ng,
                                    layer=l, n=ns, bb=bb)
        h0 = (state_s5_re[l].reshape(ns, S5_N), state_s5_im[l].reshape(ns, S5_N))
        c_s, hs_re, hs_im = _s5_call(xs1, g[2], w_s5, *s5p, s5d, gw, gb, h0, nb=ns, seq=1, tt=1)
        xs2 = _merge_call(xs1, a_s, b_s, c_s, g[2], g[3], w_gate, wbr, wout,
                          grid=(1,), tm=ns, xmap=rows1, bmap=rows1, out_shape=(ns, D_MODEL))
        ys = _ffn_call(xs2, g[4], g[5], wg[1], wu[1], wd[1], grid=(1,), tm=ns,
                       in_map=rows1, out_map=rows1, out_shape=(ns, D_MODEL))
        outs['re_s'].append(hs_re.reshape(ns, S5_GROUPS, S5_STATE))
        outs['im_s'].append(hs_im.reshape(ns, S5_GROUPS, S5_STATE))
        outs['v_s'].append(v_rows.reshape(ns, 1, GM_WIDTH))

    return (yp.reshape(nb, seq, D_MODEL), ys.reshape(ns, 1, D_MODEL),
            jnp.stack(outs['hg_p']), jnp.stack(outs['re_p']), jnp.stack(outs['im_p']),
            hg_s, jnp.stack(outs['re_s']), jnp.stack(outs['im_s']),
            jnp.stack(outs['v_s']))
```

```python
import functools
import math

import jax
import jax.numpy as jnp
from jax import lax
from jax.experimental import pallas as pl
from jax.experimental.pallas import tpu as pltpu

F32 = jnp.float32
BF16 = jnp.bfloat16

D_MODEL = 1024
D_FF = 2816
HG_HEADS = 4
HG_DK = 128
HG_WIDTH = HG_HEADS * HG_DK
GM_GROUPS = 4
GM_CHUNK = 128
GM_WIDTH = 512
S5_WIDTH = 512
S5_GROUP = 16
S5_GROUPS = S5_WIDTH // S5_GROUP
S5_STATE = 64
S5_N = S5_GROUPS * S5_STATE
S5_BLOCKS = 4
N_BRANCH = 3
EPS = 1e-6

GLA_CHUNK = 32
VMEM_LIMIT = 56 * 1024 * 1024


def _rms(x, g):
    return x * lax.rsqrt(jnp.mean(x * x, axis=-1, keepdims=True) + EPS) * g


def _gelu(x):
    c = math.sqrt(2.0 / math.pi)
    return 0.5 * x * (1.0 + jnp.tanh(c * (x + 0.044715 * (x * x * x))))


def _sigmoid(x):
    return 1.0 / (1.0 + jnp.exp(-x))


def _dot(a, b):
    return jnp.dot(a, b, preferred_element_type=F32)


def _dot_nt(a, b):
    return lax.dot_general(a, b, (((1,), (1,)), ((), ())), preferred_element_type=F32)


def _dot_tn(a, b):
    return lax.dot_general(a, b, (((0,), (0,)), ((), ())), preferred_element_type=F32)


def _const_spec(shape):
    n = len(shape)
    return pl.BlockSpec(shape, lambda *_: (0,) * n, pipeline_mode=pl.Buffered(1))


def _params(n_grid):
    return pltpu.CompilerParams(
        dimension_semantics=("arbitrary",) * n_grid, vmem_limit_bytes=VMEM_LIMIT)


def _stream_cast(src, dst, stage, sem):
    chunk = stage.shape[1]
    n = dst.shape[0] // chunk

    def copy(i):
        return pltpu.make_async_copy(src.at[pl.ds(i * chunk, chunk), :], stage.at[i % 2], sem.at[i % 2])

    copy(0).start()
    for i in range(n):
        if i + 1 < n:
            copy(i + 1).start()
        copy(i).wait()
        dst[pl.ds(i * chunk, chunk), :] = stage[i % 2].astype(BF16)


def _hbm_spec():
    return pl.BlockSpec(memory_space=pl.ANY)


def _stage(chunk, width):
    return pltpu.VMEM((2, chunk, width), F32)


def _ffn_body(x, g_pre, g_post, wg_ref, wu_ref, wd_ref):
    h = _rms(x, g_pre).astype(BF16)
    a = _dot(h, wg_ref[...])
    u = _dot(h, wu_ref[...])
    m = (a * _sigmoid(a) * u).astype(BF16)
    d = _dot(m, wd_ref[...])
    return x + 0.5 * _rms(d, g_post)


def _ffn_kernel(layer, which, n_p, x_ref, xs_ref, gpre_ref, gpost_ref, wg_hbm, wu_hbm, wd_hbm,
                o_ref, os_ref, wg, wu, wd, stage_up, stage_down, sem):
    i = pl.program_id(0)

    @pl.when(i == 0)
    def _():
        _stream_cast(wg_hbm.at[layer, which], wg, stage_up, sem)
        _stream_cast(wu_hbm.at[layer, which], wu, stage_up, sem)
        _stream_cast(wd_hbm.at[layer, which], wd, stage_down, sem)

    @pl.when(i < n_p)
    def _():
        o_ref[...] = _ffn_body(x_ref[...], gpre_ref[...], gpost_ref[...], wg, wu, wd)

    @pl.when(i == n_p)
    def _():
        os_ref[...] = _ffn_body(xs_ref[...], gpre_ref[...], gpost_ref[...], wg, wu, wd)


def _ffn_call(x2d, xs, g_pre, g_post, wg, wu, wd, *, layer, which, n_p, tm, in_map, out_map, out_shape):
    ns = xs.shape[0]
    return pl.pallas_call(
        functools.partial(_ffn_kernel, layer, which, n_p),
        grid=(n_p + 1,),
        in_specs=[
            pl.BlockSpec((tm, D_MODEL), in_map),
            _const_spec((ns, D_MODEL)),
            _const_spec((1, D_MODEL)),
            _const_spec((1, D_MODEL)),
            _hbm_spec(), _hbm_spec(), _hbm_spec(),
        ],
        out_specs=[pl.BlockSpec((tm, D_MODEL), out_map),
                   pl.BlockSpec((ns, D_MODEL), lambda i: (0, 0))],
        out_shape=[jax.ShapeDtypeStruct(out_shape, F32), jax.ShapeDtypeStruct((ns, D_MODEL), F32)],
        scratch_shapes=[
            pltpu.VMEM((D_MODEL, D_FF), BF16),
            pltpu.VMEM((D_MODEL, D_FF), BF16),
            pltpu.VMEM((D_FF, D_MODEL), BF16),
            _stage(64, D_FF),
            _stage(176, D_MODEL),
            pltpu.SemaphoreType.DMA((2,)),
        ],
        compiler_params=_params(1),
        name="ffn",
    )(x2d, xs, g_pre, g_post, wg, wu, wd)


def _hgrn_gates(z, lb):
    ls = jnp.minimum(z, 0.0) - jnp.log1p(jnp.exp(-jnp.abs(z)))
    c = jnp.log1p(-lb) + ls
    a = jnp.log(jnp.maximum(lb, 1e-37))
    lae = jnp.maximum(a, c) + jnp.log1p(jnp.exp(-jnp.abs(a - c)))
    logf = jnp.where(lb > 0.0, lae, c)
    k = (1.0 - lb) * _sigmoid(-z)
    return logf, k


def _hgrn_out(o, xg, ng):
    parts = []
    for hd in range(HG_HEADS):
        oh = o[:, hd * HG_DK:(hd + 1) * HG_DK]
        parts.append(oh * lax.rsqrt(jnp.mean(oh * oh, axis=-1, keepdims=True) + EPS))
    on = jnp.concatenate(parts, axis=-1)
    return on * ng * (xg * _sigmoid(xg))


def _lower_bound(logits_ref, layer):
    lg = logits_ref[...]
    m = jnp.max(lg, axis=0, keepdims=True)
    e = jnp.exp(lg - m)
    den = jnp.sum(e, axis=0, keepdims=True)
    lb = jnp.zeros_like(den)
    for j in range(1, layer + 1):
        lb = lb + e[j:j + 1] / den
    return lb


def _hgrn_prompt_kernel(layer, nb, tl, x_ref, g_ref, win_hbm, logits_ref, ng_ref, tri_ref,
                        a_ref, sfin_ref, st_ref, q_s, k_s, v_s, b_s, o_s, w_ref, stage, sem):
    t = pl.program_id(0)
    nt = pl.num_programs(0)

    @pl.when(t == 0)
    def _():
        st_ref[...] = jnp.zeros_like(st_ref)
        _stream_cast(win_hbm.at[layer, :, pl.ds(0, 4 * HG_WIDTH)], w_ref, stage, sem)

    xs = jnp.concatenate([x_ref[:, b * D_MODEL:(b + 1) * D_MODEL] for b in range(nb)], axis=0)
    h = _rms(xs, g_ref[...]).astype(BF16)
    p = _dot(h, w_ref[...])
    xq = p[:, 0 * HG_WIDTH:1 * HG_WIDTH]
    z = p[:, 1 * HG_WIDTH:2 * HG_WIDTH]
    xi = p[:, 2 * HG_WIDTH:3 * HG_WIDTH]
    xg = p[:, 3 * HG_WIDTH:4 * HG_WIDTH]
    lb = _lower_bound(logits_ref, layer)
    logf, kk = _hgrn_gates(z, lb)

    tri = tri_ref[...]
    tr = tri.shape[0]
    l0 = logf.astype(BF16)
    r1 = logf - l0.astype(F32)
    l1 = r1.astype(BF16)
    l2 = (r1 - l1.astype(F32)).astype(BF16)
    for i in range(nb * tl // tr):
        rs = slice(i * tr, (i + 1) * tr)
        b_s[rs, :] = _dot(tri, l0[rs]) + _dot(tri, l1[rs]) + _dot(tri, l2[rs])

    q_s[...] = xq
    k_s[...] = kk
    v_s[...] = xi

    c = GLA_CHUNK
    mid = c // 2 - 1
    row = lax.broadcasted_iota(jnp.int32, (c, c), 0)
    col = lax.broadcasted_iota(jnp.int32, (c, c), 1)
    causal = col <= row

    def chunk(ci, carry):
        pending = []
        for b in range(nb):
            r = pl.ds(pl.multiple_of(b * tl + ci * c, c), c)
            for hd in range(HG_HEADS):
                cs = slice(hd * HG_DK, (hd + 1) * HG_DK)
                bq = b_s[r, cs]
                qc = q_s[r, cs]
                kc = k_s[r, cs]
                vc = v_s[r, cs].astype(BF16)
                b_mid = bq[mid:mid + 1]
                b_last = bq[c - 1:c]
                st = st_ref[b, hd]
                q_in = (qc * jnp.exp(bq)).astype(BF16)
                q_md = (qc * jnp.exp(bq - b_mid)).astype(BF16)
                k_md = (kc * jnp.exp(b_mid - bq)).astype(BF16)
                k_ls = (kc * jnp.exp(b_last - bq)).astype(BF16)
                scores = _dot_nt(q_md, k_md)
                o_inter = _dot_nt(q_in, st.astype(BF16))
                st_ref[b, hd] = st * jnp.exp(b_last) + _dot_tn(vc, k_ls)
                pending.append((r, cs, scores, o_inter, vc))
        for r, cs, scores, o_inter, vc in pending:
            sc = jnp.where(causal, scores, 0.0).astype(BF16)
            o_s[r, cs] = o_inter + _dot(sc, vc)
        return carry

    lax.fori_loop(0, tl // c, chunk, 0)

    a = _hgrn_out(o_s[...], xg, ng_ref[...]).astype(BF16)
    for b in range(nb):
        a_ref[:, b * HG_WIDTH:(b + 1) * HG_WIDTH] = a[b * tl:(b + 1) * tl]

    @pl.when(t == nt - 1)
    def _():
        for b in range(nb):
            for hd in range(HG_HEADS):
                sfin_ref[b, hd] = st_ref[b, hd].T


def _hgrn_prompt_call(x_tm, g, w_in, logits, ng, *, layer, nb, seq, tl):
    c = GLA_CHUNK
    tr = min(256, nb * tl)
    idx = jnp.arange(tr)
    tri = ((idx[:, None] >= idx[None, :]) & (idx[:, None] // c == idx[None, :] // c)).astype(BF16)
    depth = logits.shape[0]
    rows = nb * tl
    return pl.pallas_call(
        functools.partial(_hgrn_prompt_kernel, layer, nb, tl),
        grid=(seq // tl,),
        in_specs=[
            pl.BlockSpec((tl, nb * D_MODEL), lambda t: (t, 0)),
            _const_spec((1, D_MODEL)),
            _hbm_spec(),
            _const_spec((depth, HG_WIDTH)),
            _const_spec((1, HG_WIDTH)),
            _const_spec((tr, tr)),
        ],
        out_specs=[
            pl.BlockSpec((tl, nb * HG_WIDTH), lambda t: (t, 0)),
            pl.BlockSpec((nb, HG_HEADS, HG_DK, HG_DK), lambda t: (0, 0, 0, 0)),
        ],
        out_shape=[
            jax.ShapeDtypeStruct((seq, nb * HG_WIDTH), BF16),
            jax.ShapeDtypeStruct((nb, HG_HEADS, HG_DK, HG_DK), F32),
        ],
        scratch_shapes=[
            pltpu.VMEM((nb, HG_HEADS, HG_DK, HG_DK), F32),
            pltpu.VMEM((rows, HG_WIDTH), F32),
            pltpu.VMEM((rows, HG_WIDTH), F32),
            pltpu.VMEM((rows, HG_WIDTH), F32),
            pltpu.VMEM((rows, HG_WIDTH), F32),
            pltpu.VMEM((rows, HG_WIDTH), F32),
            pltpu.VMEM((D_MODEL, 4 * HG_WIDTH), BF16),
            _stage(128, 4 * HG_WIDTH),
            pltpu.SemaphoreType.DMA((2,)),
        ],
        compiler_params=_params(1),
        name="hgrn_prompt",
    )(x_tm, g, w_in, logits, ng, tri)


def _gmlp_uv(x, g, w_ref, lng, lnb):
    h = _rms(x, g).astype(BF16)
    p = _dot(h, w_ref[...])
    u = _gelu(p[:, :GM_WIDTH])
    gv = _gelu(p[:, GM_WIDTH:])
    mu = jnp.mean(gv, axis=-1, keepdims=True)
    dv = gv - mu
    var = jnp.mean(dv * dv, axis=-1, keepdims=True)
    v = dv * lax.rsqrt(var + EPS) * lng + lnb
    return u, v


def _gmlp_prompt_kernel(layer, tg, x_ref, g_ref, win_hbm, lng_ref, lnb_ref, wm_ref, bst_ref, o_ref,
                        w_ref, stage, sem):
    @pl.when((pl.program_id(0) == 0) & (pl.program_id(1) == 0))
    def _():
        _stream_cast(win_hbm.at[layer, :, pl.ds(4 * HG_WIDTH, 2 * GM_WIDTH)], w_ref, stage, sem)

    u, v = _gmlp_uv(x_ref[...], g_ref[...], w_ref, lng_ref[...], lnb_ref[...])
    vb = v.astype(BF16)
    gc = GM_WIDTH // GM_GROUPS
    rows = []
    for j in range(tg // GM_CHUNK):
        parts = []
        for gi in range(GM_GROUPS):
            vj = vb[j * GM_CHUNK:(j + 1) * GM_CHUNK, gi * gc:(gi + 1) * gc]
            parts.append(_dot(wm_ref[gi], vj) + bst_ref[:, gi:gi + 1])
        rows.append(jnp.concatenate(parts, axis=-1))
    mix = jnp.concatenate(rows, axis=0)
    o_ref[...] = (u * mix).astype(BF16)


def _gmlp_prompt_call(x_tm, g, w_in, lng, lnb, wm, bst, *, layer, nb, seq, tg):
    return pl.pallas_call(
        functools.partial(_gmlp_prompt_kernel, layer, tg),
        grid=(nb, seq // tg),
        in_specs=[
            pl.BlockSpec((tg, D_MODEL), lambda b, t: (t, b)),
            _const_spec((1, D_MODEL)),
            _hbm_spec(),
            _const_spec((1, GM_WIDTH)),
            _const_spec((1, GM_WIDTH)),
            _const_spec((GM_GROUPS, GM_CHUNK, GM_CHUNK)),
            _const_spec((GM_CHUNK, GM_GROUPS)),
        ],
        out_specs=pl.BlockSpec((tg, GM_WIDTH), lambda b, t: (t, b)),
        out_shape=jax.ShapeDtypeStruct((seq, nb * GM_WIDTH), BF16),
        scratch_shapes=[
            pltpu.VMEM((D_MODEL, 2 * GM_WIDTH), BF16),
            _stage(128, 2 * GM_WIDTH),
            pltpu.SemaphoreType.DMA((2,)),
        ],
        compiler_params=_params(2),
        name="gmlp_prompt",
    )(x_tm, g, w_in, lng, lnb, wm, bst)


def _s5_kernel(layer, nb, tt, has_h0, relayout, *refs):
    (x_ref, g_ref, win_hbm, are_ref, aim_ref, bre_ref, bim_ref, cre_ref, cim_ref, d_ref,
     gw_hbm, gb_ref) = refs[:12]
    h0re_ref, h0im_ref = refs[12:14] if has_h0 else (None, None)
    c_ref, hre_ref, him_ref, sre, sim, lay, w_ref, gw_ref, stage_w, stage_g, sem = refs[-11:]
    t = pl.program_id(0)
    wblk = S5_WIDTH // S5_BLOCKS
    sblk = S5_N // S5_BLOCKS

    @pl.when(t == 0)
    def _():
        s5_0 = 4 * HG_WIDTH + 2 * GM_WIDTH
        _stream_cast(win_hbm.at[layer, :, pl.ds(s5_0, S5_WIDTH)], w_ref, stage_w, sem)
        _stream_cast(gw_hbm.at[layer], gw_ref, stage_g, sem)
        if has_h0:
            hre_ref[...] = h0re_ref[...]
            him_ref[...] = h0im_ref[...]
        else:
            hre_ref[...] = jnp.zeros_like(hre_ref)
            him_ref[...] = jnp.zeros_like(him_ref)

    if relayout:
        xs = jnp.concatenate([x_ref[:, b * D_MODEL:(b + 1) * D_MODEL] for b in range(nb)], axis=0)
    else:
        xs = x_ref[...]
    h = _rms(xs, g_ref[...]).astype(BF16)
    su = _dot(h, w_ref[...])
    if relayout:
        for b in range(nb):
            for j in range(S5_BLOCKS):
                lay[j, pl.ds(b, tt, stride=nb), :] = su[b * tt:(b + 1) * tt, j * wblk:(j + 1) * wblk]
        su = jnp.concatenate([lay[j] for j in range(S5_BLOCKS)], axis=-1)
    sub = su.astype(BF16)
    for j in range(S5_BLOCKS):
        uj = sub[:, j * wblk:(j + 1) * wblk]
        sre[:, j * sblk:(j + 1) * sblk] = _dot(uj, bre_ref[j])
        sim[:, j * sblk:(j + 1) * sblk] = _dot(uj, bim_ref[j])

    a_re = jnp.broadcast_to(are_ref[...], (nb, S5_N))
    a_im = jnp.broadcast_to(aim_ref[...], (nb, S5_N))

    def step(i, carry):
        h_re, h_im = carry
        r = pl.ds(pl.multiple_of(i * nb, nb), nb)
        n_re = a_re * h_re - a_im * h_im + sre[r, :]
        n_im = a_re * h_im + a_im * h_re + sim[r, :]
        sre[r, :] = n_re
        sim[r, :] = n_im
        return n_re, n_im

    h_re, h_im = lax.fori_loop(0, tt, step, (hre_ref[...], him_ref[...]))
    hre_ref[...] = h_re
    him_ref[...] = h_im

    ys = []
    for j in range(S5_BLOCKS):
        hr = sre[:, j * sblk:(j + 1) * sblk].astype(BF16)
        hi = sim[:, j * sblk:(j + 1) * sblk].astype(BF16)
        ys.append(_dot(hr, cre_ref[j]) - _dot(hi, cim_ref[j]))
    y = jnp.concatenate(ys, axis=-1) + d_ref[...] * su
    zz = _dot(_gelu(y).astype(BF16), gw_ref[...]) + gb_ref[...]
    c = zz[:, :S5_WIDTH] * _sigmoid(zz[:, S5_WIDTH:])
    if relayout:
        for j in range(S5_BLOCKS):
            lay[j] = c[:, j * wblk:(j + 1) * wblk]
        for b in range(nb):
            cb = jnp.concatenate([lay[j, pl.ds(b, tt, stride=nb), :] for j in range(S5_BLOCKS)], axis=-1)
            c_ref[:, b * S5_WIDTH:(b + 1) * S5_WIDTH] = cb.astype(BF16)
    else:
        c_ref[...] = c.astype(BF16)


def _s5_call(x2d, g, w_in, a_re, a_im, b_re, b_im, c_re, c_im, d, glu_w, gb, h0=None,
             *, layer, nb, seq, tt):
    rows = tt * nb
    relayout = seq > 1
    wblk = S5_WIDTH // S5_BLOCKS
    sblk = S5_N // S5_BLOCKS
    x_block = (tt, nb * D_MODEL) if relayout else (rows, D_MODEL)
    c_block = (tt, nb * S5_WIDTH) if relayout else (rows, S5_WIDTH)
    in_specs = [
        pl.BlockSpec(x_block, lambda t: (t, 0)),
        _const_spec((1, D_MODEL)),
        _hbm_spec(),
        _const_spec((1, S5_N)),
        _const_spec((1, S5_N)),
        _const_spec((S5_BLOCKS, wblk, sblk)),
        _const_spec((S5_BLOCKS, wblk, sblk)),
        _const_spec((S5_BLOCKS, sblk, wblk)),
        _const_spec((S5_BLOCKS, sblk, wblk)),
        _const_spec((1, S5_WIDTH)),
        _hbm_spec(),
        _const_spec((1, 2 * S5_WIDTH)),
    ]
    args = [x2d, g, w_in, a_re, a_im, b_re, b_im, c_re, c_im, d, glu_w, gb]
    if h0 is not None:
        in_specs += [_const_spec((nb, S5_N)), _const_spec((nb, S5_N))]
        args += list(h0)
    state_spec = pl.BlockSpec((nb, S5_N), lambda t: (0, 0))
    c_shape = (seq, nb * S5_WIDTH) if relayout else (nb, S5_WIDTH)
    return pl.pallas_call(
        functools.partial(_s5_kernel, layer, nb, tt, h0 is not None, relayout),
        grid=(seq // tt,),
        in_specs=in_specs,
        out_specs=[pl.BlockSpec(c_block, lambda t: (t, 0)), state_spec, state_spec],
        out_shape=[
            jax.ShapeDtypeStruct(c_shape, BF16),
            jax.ShapeDtypeStruct((nb, S5_N), F32),
            jax.ShapeDtypeStruct((nb, S5_N), F32),
        ],
        scratch_shapes=[pltpu.VMEM((rows, S5_N), F32), pltpu.VMEM((rows, S5_N), F32),
                        pltpu.VMEM((S5_BLOCKS, rows, wblk), F32),
                        pltpu.VMEM((D_MODEL, S5_WIDTH), BF16),
                        pltpu.VMEM((S5_WIDTH, 2 * S5_WIDTH), BF16),
                        _stage(128, S5_WIDTH), _stage(64, 2 * S5_WIDTH),
                        pltpu.SemaphoreType.DMA((2,))],
        compiler_params=_params(1),
        name="s5",
    )(*args)


def _merge_body(x, branches, g2, g3, wgate, wbr, wout):
    h = _rms(x, g2).astype(BF16)
    gates = _sigmoid(_dot(h, wgate[...]))
    merged = None
    for n, br in enumerate(branches):
        term = gates[:, n * D_MODEL:(n + 1) * D_MODEL] * _dot(br, wbr[n])
        merged = term if merged is None else merged + term
    return x + _rms(_dot(merged.astype(BF16), wout[...]), g3)


def _merge_kernel(layer, n_p, x_ref, a_ref, b_ref, c_ref, xs_ref, as_ref, bs_ref, cs_ref,
                  g2_ref, g3_ref, win_hbm, wbr_hbm, wout_hbm, o_ref, os_ref,
                  wgate, wbr, wout, stage_gate, stage_sq, sem):
    i = pl.program_id(0)

    @pl.when(i == 0)
    def _():
        gate0 = 4 * HG_WIDTH + 2 * GM_WIDTH + S5_WIDTH
        _stream_cast(win_hbm.at[layer, :, pl.ds(gate0, N_BRANCH * D_MODEL)], wgate, stage_gate, sem)
        for n in range(N_BRANCH):
            _stream_cast(wbr_hbm.at[layer, n], wbr.at[n], stage_sq, sem)
        _stream_cast(wout_hbm.at[layer], wout, stage_sq, sem)

    @pl.when(i < n_p)
    def _():
        o_ref[...] = _merge_body(x_ref[...], (a_ref[...], b_ref[...], c_ref[...]),
                                 g2_ref[...], g3_ref[...], wgate, wbr, wout)

    @pl.when(i == n_p)
    def _():
        os_ref[...] = _merge_body(xs_ref[...], (as_ref[...], bs_ref[...], cs_ref[...]),
                                  g2_ref[...], g3_ref[...], wgate, wbr, wout)


def _merge_call(x2d, a, b, c, xs, a_s, b_s, c_s, g2, g3, w_in, w_branch, w_out,
                *, layer, n_p, tm, xmap, out_shape):
    ns = xs.shape[0]
    return pl.pallas_call(
        functools.partial(_merge_kernel, layer, n_p),
        grid=(n_p + 1,),
        in_specs=[
            pl.BlockSpec((tm, D_MODEL), xmap),
            pl.BlockSpec((tm, HG_WIDTH), xmap),
            pl.BlockSpec((tm, GM_WIDTH), xmap),
            pl.BlockSpec((tm, S5_WIDTH), xmap),
            _const_spec((ns, D_MODEL)),
            _const_spec((ns, HG_WIDTH)),
            _const_spec((ns, GM_WIDTH)),
            _const_spec((ns, S5_WIDTH)),
            _const_spec((1, D_MODEL)),
            _const_spec((1, D_MODEL)),
            _hbm_spec(), _hbm_spec(), _hbm_spec(),
        ],
        out_specs=[pl.BlockSpec((tm, D_MODEL), xmap),
                   pl.BlockSpec((ns, D_MODEL), lambda i: (0, 0))],
        out_shape=[jax.ShapeDtypeStruct(out_shape, F32), jax.ShapeDtypeStruct((ns, D_MODEL), F32)],
        scratch_shapes=[
            pltpu.VMEM((D_MODEL, N_BRANCH * D_MODEL), BF16),
            pltpu.VMEM((N_BRANCH, HG_WIDTH, D_MODEL), BF16),
            pltpu.VMEM((D_MODEL, D_MODEL), BF16),
            _stage(64, N_BRANCH * D_MODEL),
            _stage(128, D_MODEL),
            pltpu.SemaphoreType.DMA((2,)),
        ],
        compiler_params=_params(1),
        name="merge",
    )(x2d, a, b, c, xs, a_s, b_s, c_s, g2, g3, w_in, w_branch, w_out)


def _sample_proj_kernel(layer, x_ref, g_ref, win_hbm, logits_ref, lng_ref, lnb_ref,
                        w00_ref, b00_ref, q_ref, f_ref, k_ref, v_ref, xg_ref, bout_ref, vrow_ref,
                        whg_ref, wgm_ref, stage_hg, stage_gm, sem):
    _stream_cast(win_hbm.at[layer, :, pl.ds(0, 4 * HG_WIDTH)], whg_ref, stage_hg, sem)
    _stream_cast(win_hbm.at[layer, :, pl.ds(4 * HG_WIDTH, 2 * GM_WIDTH)], wgm_ref, stage_gm, sem)
    x = x_ref[...]
    h = _rms(x, g_ref[...]).astype(BF16)
    p = _dot(h, whg_ref[...])
    z = p[:, 1 * HG_WIDTH:2 * HG_WIDTH]
    logf, kk = _hgrn_gates(z, _lower_bound(logits_ref, layer))
    q_ref[...] = p[:, 0 * HG_WIDTH:1 * HG_WIDTH]
    f_ref[...] = jnp.exp(logf)
    k_ref[...] = kk
    v_ref[...] = p[:, 2 * HG_WIDTH:3 * HG_WIDTH]
    xg_ref[...] = p[:, 3 * HG_WIDTH:4 * HG_WIDTH]
    u, v = _gmlp_uv(x, g_ref[...], wgm_ref, lng_ref[...], lnb_ref[...])
    bout_ref[...] = (u * (v * w00_ref[...] + b00_ref[...])).astype(BF16)
    vrow_ref[...] = v


def _sample_proj_call(x, g, w_in, logits, lng, lnb, w00, b00, *, layer, n):
    depth = logits.shape[0]
    f32_out = jax.ShapeDtypeStruct((n, HG_WIDTH), F32)
    return pl.pallas_call(
        functools.partial(_sample_proj_kernel, layer),
        grid=(1,),
        in_specs=[
            _const_spec((n, D_MODEL)),
            _const_spec((1, D_MODEL)),
            _hbm_spec(),
            _const_spec((depth, HG_WIDTH)),
            _const_spec((1, GM_WIDTH)),
            _const_spec((1, GM_WIDTH)),
            _const_spec((1, GM_WIDTH)),
            _const_spec((1, GM_WIDTH)),
        ],
        out_specs=[pl.BlockSpec((n, HG_WIDTH), lambda i: (0, 0))] * 7,
        out_shape=[f32_out] * 5 + [jax.ShapeDtypeStruct((n, GM_WIDTH), BF16), f32_out],
        scratch_shapes=[
            pltpu.VMEM((D_MODEL, 4 * HG_WIDTH), BF16),
            pltpu.VMEM((D_MODEL, 2 * GM_WIDTH), BF16),
            _stage(128, 4 * HG_WIDTH), _stage(128, 2 * GM_WIDTH),
            pltpu.SemaphoreType.DMA((2,)),
        ],
        compiler_params=_params(1),
        name="sample_proj",
    )(x, g, w_in, logits, lng, lnb, w00, b00)


def _hgrn_step_kernel(bb, layer, whole_stack, *refs):
    s_ref, qc_ref, fc_ref, kc_ref, v_ref, xg_ref, ng_ref = refs[:7]
    snew_ref, a_ref, o_s = refs[-3:]
    if whole_stack:
        for j in range(snew_ref.shape[0]):
            if j != layer:
                snew_ref[j] = jnp.zeros(snew_ref.shape[1:], F32)
        snew_ref = snew_ref.at[layer]
    for i in range(bb):
        for hd in range(HG_HEADS):
            cs = slice(hd * HG_DK, (hd + 1) * HG_DK)
            f_col = fc_ref[0, hd, :, i:i + 1]
            k_col = kc_ref[0, hd, :, i:i + 1]
            q_col = qc_ref[0, hd, :, i:i + 1]
            v_row = v_ref[i:i + 1, cs]
            s_new = f_col * s_ref[i, hd] + k_col * v_row
            snew_ref[i, hd] = s_new
            o_s[i:i + 1, cs] = jnp.sum(q_col * s_new, axis=0, keepdims=True)
    a_ref[...] = _hgrn_out(o_s[...], xg_ref[...], ng_ref[...]).astype(BF16)


def _hgrn_step_call(s_all, prev, qc, fc, kc, v, xg, ng, *, layer, n, bb):
    col_spec = pl.BlockSpec((1, HG_HEADS, HG_DK, bb), lambda i: (i, 0, 0, 0))
    row_spec = pl.BlockSpec((bb, HG_WIDTH), lambda i: (i, 0))
    st_spec = pl.BlockSpec((None, bb, HG_HEADS, HG_DK, HG_DK), lambda i: (layer, i, 0, 0, 0))
    in_specs = [st_spec, col_spec, col_spec, col_spec, row_spec, row_spec, _const_spec((1, HG_WIDTH))]
    args = [s_all, qc, fc, kc, v, xg, ng]
    aliases = {}
    out_st_spec = st_spec
    if prev is not None:
        in_specs.append(pl.BlockSpec(memory_space=pl.ANY))
        args.append(prev)
        aliases = {len(args) - 1: 0}
    else:
        out_st_spec = pl.BlockSpec((s_all.shape[0], bb, HG_HEADS, HG_DK, HG_DK), lambda i: (0, i, 0, 0, 0))
    return pl.pallas_call(
        functools.partial(_hgrn_step_kernel, bb, layer, prev is None),
        grid=(n // bb,),
        in_specs=in_specs,
        out_specs=[out_st_spec, row_spec],
        out_shape=[
            jax.ShapeDtypeStruct(s_all.shape, F32),
            jax.ShapeDtypeStruct((n, HG_WIDTH), BF16),
        ],
        scratch_shapes=[pltpu.VMEM((bb, HG_WIDTH), F32)],
        input_output_aliases=aliases,
        compiler_params=_params(1),
        name="hgrn_step",
    )(*args)


def _s5_params(lam_re, lam_im, log_dt, b_re, b_im, c_re, c_im):
    lr = lam_re.astype(F32)
    li = lam_im.astype(F32)
    dt = jnp.exp(log_dt.astype(F32))[:, None]
    mag = jnp.exp(lr * dt)
    a_re = mag * jnp.cos(li * dt)
    a_im = mag * jnp.sin(li * dt)
    den = lr * lr + li * li
    k_re = ((a_re - 1.0) * lr + a_im * li) / den
    k_im = (a_im * lr - (a_re - 1.0) * li) / den
    bre = b_re.astype(F32)
    bim = b_im.astype(F32)
    bb_re = k_re[..., None] * bre - k_im[..., None] * bim
    bb_im = k_re[..., None] * bim + k_im[..., None] * bre
    gpb = S5_GROUPS // S5_BLOCKS
    eye = jnp.eye(gpb, dtype=F32)

    def pack_in(w):
        w = w.reshape(S5_BLOCKS, gpb, S5_STATE, S5_GROUP)
        out = jnp.einsum('jgph,gk->jghkp', w, eye)
        return out.reshape(S5_BLOCKS, gpb * S5_GROUP, gpb * S5_STATE).astype(BF16)

    def pack_out(w):
        w = w.reshape(S5_BLOCKS, gpb, S5_GROUP, S5_STATE)
        out = jnp.einsum('jghp,gk->jgpkh', w, eye)
        return out.reshape(S5_BLOCKS, gpb * S5_STATE, gpb * S5_GROUP).astype(BF16)

    return (a_re.reshape(1, S5_N), a_im.reshape(1, S5_N), pack_in(bb_re), pack_in(bb_im),
            pack_out(c_re.astype(F32)), pack_out(c_im.astype(F32)))


def kernel(x_prompt, x_sample, state_hgrn, state_s5_re, state_s5_im, norm_g, ffn_w_gate, ffn_w_up, ffn_w_down, w_in, hgrn_lb_logits, hgrn_norm_g, gmlp_ws, gmlp_bs, gmlp_norm_g, gmlp_norm_b, s5_lam_re, s5_lam_im, s5_log_dt, s5_b_re, s5_b_im, s5_c_re, s5_c_im, s5_d, s5_glu_w, s5_glu_b, w_branch, w_out):
    nb, seq, _ = x_prompt.shape
    ns = x_sample.shape[0]
    depth = norm_g.shape[0]
    tm = 512
    tl = 128
    tg = 512
    tt = 64
    bb = 8
    nt = seq // tm
    n_p = nb * nt

    def bmajor(i):
        return (jnp.minimum(i, n_p - 1), 0)

    def tmajor(i):
        j = jnp.minimum(i, n_p - 1)
        return (j % nt, j // nt)

    yp = x_prompt.reshape(nb * seq, D_MODEL)
    ys = x_sample.reshape(ns, D_MODEL)
    logits = hgrn_lb_logits.astype(F32)
    tril = jnp.tril(jnp.ones((GM_CHUNK, GM_CHUNK), F32))

    outs = {k: [] for k in ('hg_p', 're_p', 'im_p', 're_s', 'im_s', 'v_s')}
    hg_s = None
    for l in range(depth):
        g = [norm_g[l, i].reshape(1, D_MODEL) for i in range(6)]
        ng = hgrn_norm_g[l].reshape(1, HG_WIDTH)
        lng = gmlp_norm_g[l].reshape(1, GM_WIDTH)
        lnb = gmlp_norm_b[l].reshape(1, GM_WIDTH)
        wm = (gmlp_ws[l] * tril).astype(BF16)
        bst = gmlp_bs[l].T
        gc = GM_WIDTH // GM_GROUPS
        w00 = jnp.repeat(gmlp_ws[l][:, 0, 0], gc).reshape(1, GM_WIDTH)
        b00 = jnp.repeat(gmlp_bs[l][:, 0], gc).reshape(1, GM_WIDTH)
        s5p = _s5_params(s5_lam_re[l], s5_lam_im[l], s5_log_dt[l], s5_b_re[l], s5_b_im[l],
                         s5_c_re[l], s5_c_im[l])
        s5d = s5_d[l].reshape(1, S5_WIDTH)
        gb = s5_glu_b[l].reshape(1, 2 * S5_WIDTH)
        tm_shape = (seq, nb * D_MODEL)

        x1, xs1 = _ffn_call(yp, ys, g[0], g[1], ffn_w_gate, ffn_w_up, ffn_w_down, layer=l, which=0,
                            n_p=n_p, tm=tm, in_map=bmajor if l == 0 else tmajor, out_map=tmajor,
                            out_shape=tm_shape)
        a_p, s_hg = _hgrn_prompt_call(x1, g[2], w_in, logits, ng, layer=l, nb=nb, seq=seq, tl=tl)
        b_p = _gmlp_prompt_call(x1, g[2], w_in, lng, lnb, wm, bst, layer=l, nb=nb, seq=seq, tg=tg)
        c_p, h_re, h_im = _s5_call(x1, g[2], w_in, *s5p, s5d, s5_glu_w, gb,
                                   layer=l, nb=nb, seq=seq, tt=tt)
        outs['hg_p'].append(s_hg)
        outs['re_p'].append(h_re.reshape(nb, S5_GROUPS, S5_STATE))
        outs['im_p'].append(h_im.reshape(nb, S5_GROUPS, S5_STATE))

        q, f, k, v, xg, b_s, v_rows = _sample_proj_call(
            xs1, g[2], w_in, logits, lng, lnb, w00, b00, layer=l, n=ns)

        def cols(a):
            return a.reshape(ns // bb, bb, HG_HEADS, HG_DK).transpose(0, 2, 3, 1)

        hg_s, a_s = _hgrn_step_call(state_hgrn, hg_s, cols(q), cols(f), cols(k), v, xg, ng,
                                    layer=l, n=ns, bb=bb)
        h0 = (state_s5_re[l].reshape(ns, S5_N), state_s5_im[l].reshape(ns, S5_N))
        c_s, hs_re, hs_im = _s5_call(xs1, g[2], w_in, *s5p, s5d, s5_glu_w, gb, h0,
                                     layer=l, nb=ns, seq=1, tt=1)

        x2, xs2 = _merge_call(x1, a_p, b_p, c_p, xs1, a_s, b_s, c_s, g[2], g[3], w_in, w_branch, w_out,
                              layer=l, n_p=n_p, tm=tm, xmap=tmajor, out_shape=tm_shape)
        last = l == depth - 1
        yp, ys = _ffn_call(x2, xs2, g[4], g[5], ffn_w_gate, ffn_w_up, ffn_w_down, layer=l, which=1,
                           n_p=n_p, tm=tm, in_map=tmajor, out_map=bmajor if last else tmajor,
                           out_shape=(nb * seq, D_MODEL) if last else tm_shape)
        outs['re_s'].append(hs_re.reshape(ns, S5_GROUPS, S5_STATE))
        outs['im_s'].append(hs_im.reshape(ns, S5_GROUPS, S5_STATE))
        outs['v_s'].append(v_rows.reshape(ns, 1, GM_WIDTH))

    return (yp.reshape(nb, seq, D_MODEL), ys.reshape(ns, 1, D_MODEL),
            jnp.stack(outs['hg_p']), jnp.stack(outs['re_p']), jnp.stack(outs['im_p']),
            hg_s, jnp.stack(outs['re_s']), jnp.stack(outs['im_s']),
            jnp.stack(outs['v_s']))
```

```python
import functools
import math

import jax
import jax.numpy as jnp
from jax import lax
from jax.experimental import pallas as pl
from jax.experimental.pallas import tpu as pltpu

F32 = jnp.float32
BF16 = jnp.bfloat16

D_MODEL = 1024
D_FF = 2816
HG_HEADS = 4
HG_DK = 128
HG_WIDTH = HG_HEADS * HG_DK
GM_GROUPS = 4
GM_CHUNK = 128
GM_WIDTH = 512
S5_WIDTH = 512
S5_GROUP = 16
S5_GROUPS = S5_WIDTH // S5_GROUP
S5_STATE = 64
S5_N = S5_GROUPS * S5_STATE
S5_BLOCKS = 4
N_BRANCH = 3
EPS = 1e-6

GLA_CHUNK = 32
FF_CHUNK = 512
VMEM_LIMIT = 56 * 1024 * 1024


def _rms(x, g):
    return x * lax.rsqrt(jnp.mean(x * x, axis=-1, keepdims=True) + EPS) * g


def _gelu(x):
    c = math.sqrt(2.0 / math.pi)
    return 0.5 * x * (1.0 + jnp.tanh(c * (x + 0.044715 * (x * x * x))))


def _sigmoid(x):
    return 0.5 + 0.5 * jnp.tanh(0.5 * x)


def _dot(a, b):
    return jnp.dot(a, b, preferred_element_type=F32)


def _dot_nt(a, b):
    return lax.dot_general(a, b, (((1,), (1,)), ((), ())), preferred_element_type=F32)


def _dot_tn(a, b):
    return lax.dot_general(a, b, (((0,), (0,)), ((), ())), preferred_element_type=F32)


def _const_spec(shape):
    n = len(shape)
    return pl.BlockSpec(shape, lambda *_: (0,) * n, pipeline_mode=pl.Buffered(1))


def _params(n_grid):
    return pltpu.CompilerParams(
        dimension_semantics=("arbitrary",) * n_grid, vmem_limit_bytes=VMEM_LIMIT)


def _stream_cast(src, dst, stage, sem):
    slots, chunk = stage.shape[0], stage.shape[1]
    n = dst.shape[0] // chunk

    def copy(i):
        s = i % slots
        return pltpu.make_async_copy(src.at[pl.ds(i * chunk, chunk), :], stage.at[s], sem.at[s])

    for i in range(min(slots - 1, n)):
        copy(i).start()
    for i in range(n):
        if i + slots - 1 < n:
            copy(i + slots - 1).start()
        copy(i).wait()
        dst[pl.ds(i * chunk, chunk), :] = stage[i % slots].astype(BF16)


STAGE_SLOTS = 3


def _hbm_spec():
    return pl.BlockSpec(memory_space=pl.ANY)


def _stage(chunk, width):
    return pltpu.VMEM((STAGE_SLOTS, chunk, width), F32)


def _stage_sems():
    return pltpu.SemaphoreType.DMA((STAGE_SLOTS,))


def _ffn_body(x, g_pre, g_post, wg_ref, wu_ref, wd_ref):
    h = _rms(x, g_pre).astype(BF16)
    d = None
    for c0 in range(0, D_FF, FF_CHUNK):
        c1 = min(c0 + FF_CHUNK, D_FF)
        a = _dot(h, wg_ref[:, c0:c1])
        u = _dot(h, wu_ref[:, c0:c1])
        m = (a * _sigmoid(a) * u).astype(BF16)
        part = _dot(m, wd_ref[c0:c1, :])
        d = part if d is None else d + part
    return x + 0.5 * _rms(d, g_post)


def _ffn_kernel(layer, which, n_p, x_ref, xs_ref, gpre_ref, gpost_ref, wg_hbm, wu_hbm, wd_hbm,
                o_ref, os_ref, wg, wu, wd, stage_up, stage_down, sem):
    i = pl.program_id(0)

    @pl.when(i == 0)
    def _():
        _stream_cast(wg_hbm.at[layer, which], wg, stage_up, sem)
        _stream_cast(wu_hbm.at[layer, which], wu, stage_up, sem)
        _stream_cast(wd_hbm.at[layer, which], wd, stage_down, sem)

    @pl.when(i < n_p)
    def _():
        o_ref[...] = _ffn_body(x_ref[...], gpre_ref[...], gpost_ref[...], wg, wu, wd)

    @pl.when(i == n_p)
    def _():
        os_ref[...] = _ffn_body(xs_ref[...], gpre_ref[...], gpost_ref[...], wg, wu, wd)


def _ffn_call(x2d, xs, g_pre, g_post, wg, wu, wd, *, layer, which, n_p, tm, in_map, out_map, out_shape):
    ns = xs.shape[0]
    return pl.pallas_call(
        functools.partial(_ffn_kernel, layer, which, n_p),
        grid=(n_p + 1,),
        in_specs=[
            pl.BlockSpec((tm, D_MODEL), in_map),
            _const_spec((ns, D_MODEL)),
            _const_spec((1, D_MODEL)),
            _const_spec((1, D_MODEL)),
            _hbm_spec(), _hbm_spec(), _hbm_spec(),
        ],
        out_specs=[pl.BlockSpec((tm, D_MODEL), out_map),
                   pl.BlockSpec((ns, D_MODEL), lambda i: (0, 0))],
        out_shape=[jax.ShapeDtypeStruct(out_shape, F32), jax.ShapeDtypeStruct((ns, D_MODEL), F32)],
        scratch_shapes=[
            pltpu.VMEM((D_MODEL, D_FF), BF16),
            pltpu.VMEM((D_MODEL, D_FF), BF16),
            pltpu.VMEM((D_FF, D_MODEL), BF16),
            _stage(64, D_FF),
            _stage(176, D_MODEL),
            _stage_sems(),
        ],
        compiler_params=_params(1),
        name="ffn",
    )(x2d, xs, g_pre, g_post, wg, wu, wd)


def _hgrn_gates(z, lb):
    ls = jnp.minimum(z, 0.0) - jnp.log1p(jnp.exp(-jnp.abs(z)))
    c = jnp.log1p(-lb) + ls
    a = jnp.log(jnp.maximum(lb, 1e-37))
    lae = jnp.maximum(a, c) + jnp.log1p(jnp.exp(-jnp.abs(a - c)))
    logf = jnp.where(lb > 0.0, lae, c)
    k = (1.0 - lb) * _sigmoid(-z)
    return logf, k


def _hgrn_out(o, xg, ng):
    parts = []
    for hd in range(HG_HEADS):
        oh = o[:, hd * HG_DK:(hd + 1) * HG_DK]
        parts.append(oh * lax.rsqrt(jnp.mean(oh * oh, axis=-1, keepdims=True) + EPS))
    on = jnp.concatenate(parts, axis=-1)
    return on * ng * (xg * _sigmoid(xg))


def _lower_bound(logits_ref, layer):
    lg = logits_ref[...]
    m = jnp.max(lg, axis=0, keepdims=True)
    e = jnp.exp(lg - m)
    den = jnp.sum(e, axis=0, keepdims=True)
    lb = jnp.zeros_like(den)
    for j in range(1, layer + 1):
        lb = lb + e[j:j + 1] / den
    return lb


def _hgrn_prompt_kernel(layer, nb, tl, x_ref, g_ref, win_hbm, logits_ref, ng_ref, tri_ref,
                        a_ref, sfin_ref, st_ref, q_s, k_s, v_s, b_s, o_s, w_ref, stage, sem):
    t = pl.program_id(0)
    nt = pl.num_programs(0)

    @pl.when(t == 0)
    def _():
        st_ref[...] = jnp.zeros_like(st_ref)
        _stream_cast(win_hbm.at[layer, :, pl.ds(0, 4 * HG_WIDTH)], w_ref, stage, sem)

    xs = jnp.concatenate([x_ref[:, b * D_MODEL:(b + 1) * D_MODEL] for b in range(nb)], axis=0)
    h = _rms(xs, g_ref[...]).astype(BF16)
    p = _dot(h, w_ref[...])
    xq = p[:, 0 * HG_WIDTH:1 * HG_WIDTH]
    z = p[:, 1 * HG_WIDTH:2 * HG_WIDTH]
    xi = p[:, 2 * HG_WIDTH:3 * HG_WIDTH]
    xg = p[:, 3 * HG_WIDTH:4 * HG_WIDTH]
    lb = _lower_bound(logits_ref, layer)
    logf, kk = _hgrn_gates(z, lb)

    tri = tri_ref[...]
    tr = tri.shape[0]
    l0 = logf.astype(BF16)
    r1 = logf - l0.astype(F32)
    l1 = r1.astype(BF16)
    l2 = (r1 - l1.astype(F32)).astype(BF16)
    for i in range(nb * tl // tr):
        rs = slice(i * tr, (i + 1) * tr)
        b_s[rs, :] = _dot(tri, l0[rs]) + _dot(tri, l1[rs]) + _dot(tri, l2[rs])

    q_s[...] = xq
    k_s[...] = kk
    v_s[...] = xi

    c = GLA_CHUNK
    mid = c // 2 - 1
    row = lax.broadcasted_iota(jnp.int32, (c, c), 0)
    col = lax.broadcasted_iota(jnp.int32, (c, c), 1)
    causal = col <= row

    def chunk(ci, carry):
        pending = []
        for b in range(nb):
            r = pl.ds(pl.multiple_of(b * tl + ci * c, c), c)
            for hd in range(HG_HEADS):
                cs = slice(hd * HG_DK, (hd + 1) * HG_DK)
                bq = b_s[r, cs]
                qc = q_s[r, cs]
                kc = k_s[r, cs]
                vc = v_s[r, cs].astype(BF16)
                b_mid = bq[mid:mid + 1]
                b_last = bq[c - 1:c]
                st = st_ref[b, hd]
                q_in = (qc * jnp.exp(bq)).astype(BF16)
                q_md = (qc * jnp.exp(bq - b_mid)).astype(BF16)
                k_md = (kc * jnp.exp(b_mid - bq)).astype(BF16)
                k_ls = (kc * jnp.exp(b_last - bq)).astype(BF16)
                scores = _dot_nt(q_md, k_md)
                o_inter = _dot_nt(q_in, st.astype(BF16))
                st_ref[b, hd] = st * jnp.exp(b_last) + _dot_tn(vc, k_ls)
                pending.append((r, cs, scores, o_inter, vc))
        for r, cs, scores, o_inter, vc in pending:
            sc = jnp.where(causal, scores, 0.0).astype(BF16)
            o_s[r, cs] = o_inter + _dot(sc, vc)
        return carry

    lax.fori_loop(0, tl // c, chunk, 0)

    a = _hgrn_out(o_s[...], xg, ng_ref[...]).astype(BF16)
    for b in range(nb):
        a_ref[:, b * HG_WIDTH:(b + 1) * HG_WIDTH] = a[b * tl:(b + 1) * tl]

    @pl.when(t == nt - 1)
    def _():
        for b in range(nb):
            for hd in range(HG_HEADS):
                sfin_ref[b, hd] = st_ref[b, hd].T


def _hgrn_prompt_call(x_tm, g, w_in, logits, ng, *, layer, nb, seq, tl):
    c = GLA_CHUNK
    tr = min(256, nb * tl)
    idx = jnp.arange(tr)
    tri = ((idx[:, None] >= idx[None, :]) & (idx[:, None] // c == idx[None, :] // c)).astype(BF16)
    depth = logits.shape[0]
    rows = nb * tl
    return pl.pallas_call(
        functools.partial(_hgrn_prompt_kernel, layer, nb, tl),
        grid=(seq // tl,),
        in_specs=[
            pl.BlockSpec((tl, nb * D_MODEL), lambda t: (t, 0)),
            _const_spec((1, D_MODEL)),
            _hbm_spec(),
            _const_spec((depth, HG_WIDTH)),
            _const_spec((1, HG_WIDTH)),
            _const_spec((tr, tr)),
        ],
        out_specs=[
            pl.BlockSpec((tl, nb * HG_WIDTH), lambda t: (t, 0)),
            pl.BlockSpec((nb, HG_HEADS, HG_DK, HG_DK), lambda t: (0, 0, 0, 0)),
        ],
        out_shape=[
            jax.ShapeDtypeStruct((seq, nb * HG_WIDTH), BF16),
            jax.ShapeDtypeStruct((nb, HG_HEADS, HG_DK, HG_DK), F32),
        ],
        scratch_shapes=[
            pltpu.VMEM((nb, HG_HEADS, HG_DK, HG_DK), F32),
            pltpu.VMEM((rows, HG_WIDTH), F32),
            pltpu.VMEM((rows, HG_WIDTH), F32),
            pltpu.VMEM((rows, HG_WIDTH), F32),
            pltpu.VMEM((rows, HG_WIDTH), F32),
            pltpu.VMEM((rows, HG_WIDTH), F32),
            pltpu.VMEM((D_MODEL, 4 * HG_WIDTH), BF16),
            _stage(128, 4 * HG_WIDTH),
            _stage_sems(),
        ],
        compiler_params=_params(1),
        name="hgrn_prompt",
    )(x_tm, g, w_in, logits, ng, tri)


def _gmlp_uv(x, g, w_ref, lng, lnb):
    h = _rms(x, g).astype(BF16)
    p = _dot(h, w_ref[...])
    u = _gelu(p[:, :GM_WIDTH])
    gv = _gelu(p[:, GM_WIDTH:])
    mu = jnp.mean(gv, axis=-1, keepdims=True)
    dv = gv - mu
    var = jnp.mean(dv * dv, axis=-1, keepdims=True)
    v = dv * lax.rsqrt(var + EPS) * lng + lnb
    return u, v


def _gmlp_prompt_kernel(layer, tg, x_ref, g_ref, win_hbm, lng_ref, lnb_ref, wm_ref, bst_ref, o_ref,
                        w_ref, stage, sem):
    @pl.when((pl.program_id(0) == 0) & (pl.program_id(1) == 0))
    def _():
        _stream_cast(win_hbm.at[layer, :, pl.ds(4 * HG_WIDTH, 2 * GM_WIDTH)], w_ref, stage, sem)

    u, v = _gmlp_uv(x_ref[...], g_ref[...], w_ref, lng_ref[...], lnb_ref[...])
    vb = v.astype(BF16)
    gc = GM_WIDTH // GM_GROUPS
    rows = []
    for j in range(tg // GM_CHUNK):
        parts = []
        for gi in range(GM_GROUPS):
            vj = vb[j * GM_CHUNK:(j + 1) * GM_CHUNK, gi * gc:(gi + 1) * gc]
            parts.append(_dot(wm_ref[gi], vj) + bst_ref[:, gi:gi + 1])
        rows.append(jnp.concatenate(parts, axis=-1))
    mix = jnp.concatenate(rows, axis=0)
    o_ref[...] = (u * mix).astype(BF16)


def _gmlp_prompt_call(x_tm, g, w_in, lng, lnb, wm, bst, *, layer, nb, seq, tg):
    return pl.pallas_call(
        functools.partial(_gmlp_prompt_kernel, layer, tg),
        grid=(nb, seq // tg),
        in_specs=[
            pl.BlockSpec((tg, D_MODEL), lambda b, t: (t, b)),
            _const_spec((1, D_MODEL)),
            _hbm_spec(),
            _const_spec((1, GM_WIDTH)),
            _const_spec((1, GM_WIDTH)),
            _const_spec((GM_GROUPS, GM_CHUNK, GM_CHUNK)),
            _const_spec((GM_CHUNK, GM_GROUPS)),
        ],
        out_specs=pl.BlockSpec((tg, GM_WIDTH), lambda b, t: (t, b)),
        out_shape=jax.ShapeDtypeStruct((seq, nb * GM_WIDTH), BF16),
        scratch_shapes=[
            pltpu.VMEM((D_MODEL, 2 * GM_WIDTH), BF16),
            _stage(128, 2 * GM_WIDTH),
            _stage_sems(),
        ],
        compiler_params=_params(2),
        name="gmlp_prompt",
    )(x_tm, g, w_in, lng, lnb, wm, bst)


def _s5_kernel(layer, nb, tt, has_h0, relayout, *refs):
    (x_ref, g_ref, win_hbm, are_ref, aim_ref, bre_ref, bim_ref, cre_ref, cim_ref, d_ref,
     gw_hbm, gb_ref) = refs[:12]
    h0re_ref, h0im_ref = refs[12:14] if has_h0 else (None, None)
    c_ref, hre_ref, him_ref, sre, sim, lay, w_ref, gw_ref, stage_w, stage_g, sem = refs[-11:]
    t = pl.program_id(0)
    wblk = S5_WIDTH // S5_BLOCKS
    sblk = S5_N // S5_BLOCKS

    @pl.when(t == 0)
    def _():
        s5_0 = 4 * HG_WIDTH + 2 * GM_WIDTH
        _stream_cast(win_hbm.at[layer, :, pl.ds(s5_0, S5_WIDTH)], w_ref, stage_w, sem)
        _stream_cast(gw_hbm.at[layer], gw_ref, stage_g, sem)
        if has_h0:
            hre_ref[...] = h0re_ref[...]
            him_ref[...] = h0im_ref[...]
        else:
            hre_ref[...] = jnp.zeros_like(hre_ref)
            him_ref[...] = jnp.zeros_like(him_ref)

    if relayout:
        xs = jnp.concatenate([x_ref[:, b * D_MODEL:(b + 1) * D_MODEL] for b in range(nb)], axis=0)
    else:
        xs = x_ref[...]
    h = _rms(xs, g_ref[...]).astype(BF16)
    su = _dot(h, w_ref[...])
    if relayout:
        for b in range(nb):
            for j in range(S5_BLOCKS):
                lay[j, pl.ds(b, tt, stride=nb), :] = su[b * tt:(b + 1) * tt, j * wblk:(j + 1) * wblk]
        su = jnp.concatenate([lay[j] for j in range(S5_BLOCKS)], axis=-1)
    sub = su.astype(BF16)

    def project(j):
        uj = sub[:, j * wblk:(j + 1) * wblk]
        sre[:, j * sblk:(j + 1) * sblk] = _dot(uj, bre_ref[j])
        sim[:, j * sblk:(j + 1) * sblk] = _dot(uj, bim_ref[j])

    def scan(j):
        cols = slice(j * sblk, (j + 1) * sblk)
        a_re = jnp.broadcast_to(are_ref[:, cols], (nb, sblk))
        a_im = jnp.broadcast_to(aim_ref[:, cols], (nb, sblk))
        h_re = hre_ref[:, cols]
        h_im = him_ref[:, cols]
        for i in range(tt):
            r = slice(i * nb, (i + 1) * nb)
            n_re = a_re * h_re - a_im * h_im + sre[r, cols]
            n_im = a_re * h_im + a_im * h_re + sim[r, cols]
            sre[r, cols] = n_re
            sim[r, cols] = n_im
            h_re, h_im = n_re, n_im
        hre_ref[:, cols] = h_re
        him_ref[:, cols] = h_im

    def readout(j):
        hr = sre[:, j * sblk:(j + 1) * sblk].astype(BF16)
        hi = sim[:, j * sblk:(j + 1) * sblk].astype(BF16)
        return _dot(hr, cre_ref[j]) - _dot(hi, cim_ref[j])

    ys = []
    project(0)
    for j in range(S5_BLOCKS):
        if j + 1 < S5_BLOCKS:
            project(j + 1)
        scan(j)
        ys.append(readout(j))
    y = jnp.concatenate(ys, axis=-1) + d_ref[...] * su
    zz = _dot(_gelu(y).astype(BF16), gw_ref[...]) + gb_ref[...]
    c = zz[:, :S5_WIDTH] * _sigmoid(zz[:, S5_WIDTH:])
    if relayout:
        for j in range(S5_BLOCKS):
            lay[j] = c[:, j * wblk:(j + 1) * wblk]
        for b in range(nb):
            cb = jnp.concatenate([lay[j, pl.ds(b, tt, stride=nb), :] for j in range(S5_BLOCKS)], axis=-1)
            c_ref[:, b * S5_WIDTH:(b + 1) * S5_WIDTH] = cb.astype(BF16)
    else:
        c_ref[...] = c.astype(BF16)


def _s5_call(x2d, g, w_in, a_re, a_im, b_re, b_im, c_re, c_im, d, glu_w, gb, h0=None,
             *, layer, nb, seq, tt):
    rows = tt * nb
    relayout = seq > 1
    wblk = S5_WIDTH // S5_BLOCKS
    sblk = S5_N // S5_BLOCKS
    x_block = (tt, nb * D_MODEL) if relayout else (rows, D_MODEL)
    c_block = (tt, nb * S5_WIDTH) if relayout else (rows, S5_WIDTH)
    in_specs = [
        pl.BlockSpec(x_block, lambda t: (t, 0)),
        _const_spec((1, D_MODEL)),
        _hbm_spec(),
        _const_spec((1, S5_N)),
        _const_spec((1, S5_N)),
        _const_spec((S5_BLOCKS, wblk, sblk)),
        _const_spec((S5_BLOCKS, wblk, sblk)),
        _const_spec((S5_BLOCKS, sblk, wblk)),
        _const_spec((S5_BLOCKS, sblk, wblk)),
        _const_spec((1, S5_WIDTH)),
        _hbm_spec(),
        _const_spec((1, 2 * S5_WIDTH)),
    ]
    args = [x2d, g, w_in, a_re, a_im, b_re, b_im, c_re, c_im, d, glu_w, gb]
    if h0 is not None:
        in_specs += [_const_spec((nb, S5_N)), _const_spec((nb, S5_N))]
        args += list(h0)
    state_spec = pl.BlockSpec((nb, S5_N), lambda t: (0, 0))
    c_shape = (seq, nb * S5_WIDTH) if relayout else (nb, S5_WIDTH)
    return pl.pallas_call(
        functools.partial(_s5_kernel, layer, nb, tt, h0 is not None, relayout),
        grid=(seq // tt,),
        in_specs=in_specs,
        out_specs=[pl.BlockSpec(c_block, lambda t: (t, 0)), state_spec, state_spec],
        out_shape=[
            jax.ShapeDtypeStruct(c_shape, BF16),
            jax.ShapeDtypeStruct((nb, S5_N), F32),
            jax.ShapeDtypeStruct((nb, S5_N), F32),
        ],
        scratch_shapes=[pltpu.VMEM((rows, S5_N), F32), pltpu.VMEM((rows, S5_N), F32),
                        pltpu.VMEM((S5_BLOCKS, rows, wblk), F32),
                        pltpu.VMEM((D_MODEL, S5_WIDTH), BF16),
                        pltpu.VMEM((S5_WIDTH, 2 * S5_WIDTH), BF16),
                        _stage(128, S5_WIDTH), _stage(64, 2 * S5_WIDTH),
                        _stage_sems()],
        compiler_params=_params(1),
        name="s5",
    )(*args)


def _merge_body(x, branches, g2, g3, wgate, wbr, wout):
    h = _rms(x, g2).astype(BF16)
    merged = None
    for n, br in enumerate(branches):
        gate = _sigmoid(_dot(h, wgate[:, n * D_MODEL:(n + 1) * D_MODEL]))
        term = gate * _dot(br, wbr[n])
        merged = term if merged is None else merged + term
    return x + _rms(_dot(merged.astype(BF16), wout[...]), g3)


def _merge_kernel(layer, n_p, x_ref, a_ref, b_ref, c_ref, xs_ref, as_ref, bs_ref, cs_ref,
                  g2_ref, g3_ref, win_hbm, wbr_hbm, wout_hbm, o_ref, os_ref,
                  wgate, wbr, wout, stage_gate, stage_sq, sem):
    i = pl.program_id(0)

    @pl.when(i == 0)
    def _():
        gate0 = 4 * HG_WIDTH + 2 * GM_WIDTH + S5_WIDTH
        _stream_cast(win_hbm.at[layer, :, pl.ds(gate0, N_BRANCH * D_MODEL)], wgate, stage_gate, sem)
        for n in range(N_BRANCH):
            _stream_cast(wbr_hbm.at[layer, n], wbr.at[n], stage_sq, sem)
        _stream_cast(wout_hbm.at[layer], wout, stage_sq, sem)

    @pl.when(i < n_p)
    def _():
        o_ref[...] = _merge_body(x_ref[...], (a_ref[...], b_ref[...], c_ref[...]),
                                 g2_ref[...], g3_ref[...], wgate, wbr, wout)

    @pl.when(i == n_p)
    def _():
        os_ref[...] = _merge_body(xs_ref[...], (as_ref[...], bs_ref[...], cs_ref[...]),
                                  g2_ref[...], g3_ref[...], wgate, wbr, wout)


def _merge_call(x2d, a, b, c, xs, a_s, b_s, c_s, g2, g3, w_in, w_branch, w_out,
                *, layer, n_p, tm, xmap, out_shape):
    ns = xs.shape[0]
    return pl.pallas_call(
        functools.partial(_merge_kernel, layer, n_p),
        grid=(n_p + 1,),
        in_specs=[
            pl.BlockSpec((tm, D_MODEL), xmap),
            pl.BlockSpec((tm, HG_WIDTH), xmap),
            pl.BlockSpec((tm, GM_WIDTH), xmap),
            pl.BlockSpec((tm, S5_WIDTH), xmap),
            _const_spec((ns, D_MODEL)),
            _const_spec((ns, HG_WIDTH)),
            _const_spec((ns, GM_WIDTH)),
            _const_spec((ns, S5_WIDTH)),
            _const_spec((1, D_MODEL)),
            _const_spec((1, D_MODEL)),
            _hbm_spec(), _hbm_spec(), _hbm_spec(),
        ],
        out_specs=[pl.BlockSpec((tm, D_MODEL), xmap),
                   pl.BlockSpec((ns, D_MODEL), lambda i: (0, 0))],
        out_shape=[jax.ShapeDtypeStruct(out_shape, F32), jax.ShapeDtypeStruct((ns, D_MODEL), F32)],
        scratch_shapes=[
            pltpu.VMEM((D_MODEL, N_BRANCH * D_MODEL), BF16),
            pltpu.VMEM((N_BRANCH, HG_WIDTH, D_MODEL), BF16),
            pltpu.VMEM((D_MODEL, D_MODEL), BF16),
            _stage(64, N_BRANCH * D_MODEL),
            _stage(128, D_MODEL),
            _stage_sems(),
        ],
        compiler_params=_params(1),
        name="merge",
    )(x2d, a, b, c, xs, a_s, b_s, c_s, g2, g3, w_in, w_branch, w_out)


def _sample_proj_kernel(layer, x_ref, g_ref, win_hbm, logits_ref, lng_ref, lnb_ref,
                        w00_ref, b00_ref, q_ref, f_ref, k_ref, v_ref, xg_ref, bout_ref, vrow_ref,
                        whg_ref, wgm_ref, stage_hg, stage_gm, sem):
    _stream_cast(win_hbm.at[layer, :, pl.ds(0, 4 * HG_WIDTH)], whg_ref, stage_hg, sem)
    _stream_cast(win_hbm.at[layer, :, pl.ds(4 * HG_WIDTH, 2 * GM_WIDTH)], wgm_ref, stage_gm, sem)
    x = x_ref[...]
    h = _rms(x, g_ref[...]).astype(BF16)
    p = _dot(h, whg_ref[...])
    z = p[:, 1 * HG_WIDTH:2 * HG_WIDTH]
    logf, kk = _hgrn_gates(z, _lower_bound(logits_ref, layer))
    q_ref[...] = p[:, 0 * HG_WIDTH:1 * HG_WIDTH]
    f_ref[...] = jnp.exp(logf)
    k_ref[...] = kk
    v_ref[...] = p[:, 2 * HG_WIDTH:3 * HG_WIDTH]
    xg_ref[...] = p[:, 3 * HG_WIDTH:4 * HG_WIDTH]
    u, v = _gmlp_uv(x, g_ref[...], wgm_ref, lng_ref[...], lnb_ref[...])
    bout_ref[...] = (u * (v * w00_ref[...] + b00_ref[...])).astype(BF16)
    vrow_ref[...] = v


def _sample_proj_call(x, g, w_in, logits, lng, lnb, w00, b00, *, layer, n):
    depth = logits.shape[0]
    f32_out = jax.ShapeDtypeStruct((n, HG_WIDTH), F32)
    return pl.pallas_call(
        functools.partial(_sample_proj_kernel, layer),
        grid=(1,),
        in_specs=[
            _const_spec((n, D_MODEL)),
            _const_spec((1, D_MODEL)),
            _hbm_spec(),
            _const_spec((depth, HG_WIDTH)),
            _const_spec((1, GM_WIDTH)),
            _const_spec((1, GM_WIDTH)),
            _const_spec((1, GM_WIDTH)),
            _const_spec((1, GM_WIDTH)),
        ],
        out_specs=[pl.BlockSpec((n, HG_WIDTH), lambda i: (0, 0))] * 7,
        out_shape=[f32_out] * 5 + [jax.ShapeDtypeStruct((n, GM_WIDTH), BF16), f32_out],
        scratch_shapes=[
            pltpu.VMEM((D_MODEL, 4 * HG_WIDTH), BF16),
            pltpu.VMEM((D_MODEL, 2 * GM_WIDTH), BF16),
            _stage(128, 4 * HG_WIDTH), _stage(128, 2 * GM_WIDTH),
            _stage_sems(),
        ],
        compiler_params=_params(1),
        name="sample_proj",
    )(x, g, w_in, logits, lng, lnb, w00, b00)


def _hgrn_step_kernel(bb, layer, whole_stack, *refs):
    s_ref, qc_ref, fc_ref, kc_ref, v_ref, xg_ref, ng_ref = refs[:7]
    snew_ref, a_ref, o_s = refs[-3:]
    if whole_stack:
        for j in range(snew_ref.shape[0]):
            if j != layer:
                snew_ref[j] = jnp.zeros(snew_ref.shape[1:], F32)
        snew_ref = snew_ref.at[layer]
    for i in range(bb):
        for hd in range(HG_HEADS):
            cs = slice(hd * HG_DK, (hd + 1) * HG_DK)
            f_col = fc_ref[0, hd, :, i:i + 1]
            k_col = kc_ref[0, hd, :, i:i + 1]
            q_col = qc_ref[0, hd, :, i:i + 1]
            v_row = v_ref[i:i + 1, cs]
            s_new = f_col * s_ref[i, hd] + k_col * v_row
            snew_ref[i, hd] = s_new
            o_s[i:i + 1, cs] = jnp.sum(q_col * s_new, axis=0, keepdims=True)
    a_ref[...] = _hgrn_out(o_s[...], xg_ref[...], ng_ref[...]).astype(BF16)


def _hgrn_step_call(s_all, prev, qc, fc, kc, v, xg, ng, *, layer, n, bb):
    col_spec = pl.BlockSpec((1, HG_HEADS, HG_DK, bb), lambda i: (i, 0, 0, 0))
    row_spec = pl.BlockSpec((bb, HG_WIDTH), lambda i: (i, 0))
    st_spec = pl.BlockSpec((None, bb, HG_HEADS, HG_DK, HG_DK), lambda i: (layer, i, 0, 0, 0))
    in_specs = [st_spec, col_spec, col_spec, col_spec, row_spec, row_spec, _const_spec((1, HG_WIDTH))]
    args = [s_all, qc, fc, kc, v, xg, ng]
    aliases = {}
    out_st_spec = st_spec
    if prev is not None:
        in_specs.append(pl.BlockSpec(memory_space=pl.ANY))
        args.append(prev)
        aliases = {len(args) - 1: 0}
    else:
        out_st_spec = pl.BlockSpec((s_all.shape[0], bb, HG_HEADS, HG_DK, HG_DK), lambda i: (0, i, 0, 0, 0))
    return pl.pallas_call(
        functools.partial(_hgrn_step_kernel, bb, layer, prev is None),
        grid=(n // bb,),
        in_specs=in_specs,
        out_specs=[out_st_spec, row_spec],
        out_shape=[
            jax.ShapeDtypeStruct(s_all.shape, F32),
            jax.ShapeDtypeStruct((n, HG_WIDTH), BF16),
        ],
        scratch_shapes=[pltpu.VMEM((bb, HG_WIDTH), F32)],
        input_output_aliases=aliases,
        compiler_params=_params(1),
        name="hgrn_step",
    )(*args)


def _s5_params(lam_re, lam_im, log_dt, b_re, b_im, c_re, c_im):
    lr = lam_re.astype(F32)
    li = lam_im.astype(F32)
    dt = jnp.exp(log_dt.astype(F32))[:, None]
    mag = jnp.exp(lr * dt)
    a_re = mag * jnp.cos(li * dt)
    a_im = mag * jnp.sin(li * dt)
    den = lr * lr + li * li
    k_re = ((a_re - 1.0) * lr + a_im * li) / den
    k_im = (a_im * lr - (a_re - 1.0) * li) / den
    bre = b_re.astype(F32)
    bim = b_im.astype(F32)
    bb_re = k_re[..., None] * bre - k_im[..., None] * bim
    bb_im = k_re[..., None] * bim + k_im[..., None] * bre
    gpb = S5_GROUPS // S5_BLOCKS
    eye = jnp.eye(gpb, dtype=F32)

    def pack_in(w):
        w = w.reshape(S5_BLOCKS, gpb, S5_STATE, S5_GROUP)
        out = jnp.einsum('jgph,gk->jghkp', w, eye)
        return out.reshape(S5_BLOCKS, gpb * S5_GROUP, gpb * S5_STATE).astype(BF16)

    def pack_out(w):
        w = w.reshape(S5_BLOCKS, gpb, S5_GROUP, S5_STATE)
        out = jnp.einsum('jghp,gk->jgpkh', w, eye)
        return out.reshape(S5_BLOCKS, gpb * S5_STATE, gpb * S5_GROUP).astype(BF16)

    return (a_re.reshape(1, S5_N), a_im.reshape(1, S5_N), pack_in(bb_re), pack_in(bb_im),
            pack_out(c_re.astype(F32)), pack_out(c_im.astype(F32)))


def kernel(x_prompt, x_sample, state_hgrn, state_s5_re, state_s5_im, norm_g, ffn_w_gate, ffn_w_up, ffn_w_down, w_in, hgrn_lb_logits, hgrn_norm_g, gmlp_ws, gmlp_bs, gmlp_norm_g, gmlp_norm_b, s5_lam_re, s5_lam_im, s5_log_dt, s5_b_re, s5_b_im, s5_c_re, s5_c_im, s5_d, s5_glu_w, s5_glu_b, w_branch, w_out):
    nb, seq, _ = x_prompt.shape
    ns = x_sample.shape[0]
    depth = norm_g.shape[0]
    tm = 1024
    tl = 128
    tg = 512
    tt = 64
    bb = 8
    nt = seq // tm
    n_p = nb * nt

    def bmajor(i):
        return (jnp.minimum(i, n_p - 1), 0)

    def tmajor(i):
        j = jnp.minimum(i, n_p - 1)
        return (j % nt, j // nt)

    yp = x_prompt.reshape(nb * seq, D_MODEL)
    ys = x_sample.reshape(ns, D_MODEL)
    logits = hgrn_lb_logits.astype(F32)
    tril = jnp.tril(jnp.ones((GM_CHUNK, GM_CHUNK), F32))

    outs = {k: [] for k in ('hg_p', 're_p', 'im_p', 're_s', 'im_s', 'v_s')}
    hg_s = None
    for l in range(depth):
        g = [norm_g[l, i].reshape(1, D_MODEL) for i in range(6)]
        ng = hgrn_norm_g[l].reshape(1, HG_WIDTH)
        lng = gmlp_norm_g[l].reshape(1, GM_WIDTH)
        lnb = gmlp_norm_b[l].reshape(1, GM_WIDTH)
        wm = (gmlp_ws[l] * tril).astype(BF16)
        bst = gmlp_bs[l].T
        gc = GM_WIDTH // GM_GROUPS
        w00 = jnp.repeat(gmlp_ws[l][:, 0, 0], gc).reshape(1, GM_WIDTH)
        b00 = jnp.repeat(gmlp_bs[l][:, 0], gc).reshape(1, GM_WIDTH)
        s5p = _s5_params(s5_lam_re[l], s5_lam_im[l], s5_log_dt[l], s5_b_re[l], s5_b_im[l],
                         s5_c_re[l], s5_c_im[l])
        s5d = s5_d[l].reshape(1, S5_WIDTH)
        gb = s5_glu_b[l].reshape(1, 2 * S5_WIDTH)
        tm_shape = (seq, nb * D_MODEL)

        x1, xs1 = _ffn_call(yp, ys, g[0], g[1], ffn_w_gate, ffn_w_up, ffn_w_down, layer=l, which=0,
                            n_p=n_p, tm=tm, in_map=bmajor if l == 0 else tmajor, out_map=tmajor,
                            out_shape=tm_shape)
        a_p, s_hg = _hgrn_prompt_call(x1, g[2], w_in, logits, ng, layer=l, nb=nb, seq=seq, tl=tl)
        b_p = _gmlp_prompt_call(x1, g[2], w_in, lng, lnb, wm, bst, layer=l, nb=nb, seq=seq, tg=tg)
        c_p, h_re, h_im = _s5_call(x1, g[2], w_in, *s5p, s5d, s5_glu_w, gb,
                                   layer=l, nb=nb, seq=seq, tt=tt)
        outs['hg_p'].append(s_hg)
        outs['re_p'].append(h_re.reshape(nb, S5_GROUPS, S5_STATE))
        outs['im_p'].append(h_im.reshape(nb, S5_GROUPS, S5_STATE))

        q, f, k, v, xg, b_s, v_rows = _sample_proj_call(
            xs1, g[2], w_in, logits, lng, lnb, w00, b00, layer=l, n=ns)

        def cols(a):
            return a.reshape(ns // bb, bb, HG_HEADS, HG_DK).transpose(0, 2, 3, 1)

        hg_s, a_s = _hgrn_step_call(state_hgrn, hg_s, cols(q), cols(f), cols(k), v, xg, ng,
                                    layer=l, n=ns, bb=bb)
        h0 = (state_s5_re[l].reshape(ns, S5_N), state_s5_im[l].reshape(ns, S5_N))
        c_s, hs_re, hs_im = _s5_call(xs1, g[2], w_in, *s5p, s5d, s5_glu_w, gb, h0,
                                     layer=l, nb=ns, seq=1, tt=1)

        x2, xs2 = _merge_call(x1, a_p, b_p, c_p, xs1, a_s, b_s, c_s, g[2], g[3], w_in, w_branch, w_out,
                              layer=l, n_p=n_p, tm=tm, xmap=tmajor, out_shape=tm_shape)
        last = l == depth - 1
        yp, ys = _ffn_call(x2, xs2, g[4], g[5], ffn_w_gate, ffn_w_up, ffn_w_down, layer=l, which=1,
                           n_p=n_p, tm=tm, in_map=tmajor, out_map=bmajor if last else tmajor,
                           out_shape=(nb * seq, D_MODEL) if last else tm_shape)
        outs['re_s'].append(hs_re.reshape(ns, S5_GROUPS, S5_STATE))
        outs['im_s'].append(hs_im.reshape(ns, S5_GROUPS, S5_STATE))
        outs['v_s'].append(v_rows.reshape(ns, 1, GM_WIDTH))

    return (yp.reshape(nb, seq, D_MODEL), ys.reshape(ns, 1, D_MODEL),
            jnp.stack(outs['hg_p']), jnp.stack(outs['re_p']), jnp.stack(outs['im_p']),
            hg_s, jnp.stack(outs['re_s']), jnp.stack(outs['im_s']),
            jnp.stack(outs['v_s']))
```

```python
import functools
import math

import jax
import jax.numpy as jnp
from jax import lax
from jax.experimental import pallas as pl
from jax.experimental.pallas import tpu as pltpu

F32 = jnp.float32
BF16 = jnp.bfloat16

D_MODEL = 1024
D_FF = 2816
HG_HEADS = 4
HG_DK = 128
HG_WIDTH = HG_HEADS * HG_DK
GM_GROUPS = 4
GM_CHUNK = 128
GM_WIDTH = 512
S5_WIDTH = 512
S5_GROUP = 16
S5_GROUPS = S5_WIDTH // S5_GROUP
S5_STATE = 64
S5_N = S5_GROUPS * S5_STATE
S5_BLOCKS = 4
N_BRANCH = 3
EPS = 1e-6

COL_GM = 4 * HG_WIDTH
COL_S5 = COL_GM + 2 * GM_WIDTH
COL_GATE = COL_S5 + S5_WIDTH
N_IN = COL_GATE + N_BRANCH * D_MODEL

GLA_CHUNK = 32
FF_CHUNK = 512
VMEM_LIMIT = 56 * 1024 * 1024


def _rms(x, g):
    return x * lax.rsqrt(jnp.mean(x * x, axis=-1, keepdims=True) + EPS) * g


def _gelu(x):
    c = math.sqrt(2.0 / math.pi)
    return 0.5 * x * (1.0 + jnp.tanh(c * (x + 0.044715 * (x * x * x))))


def _sigmoid(x):
    return 0.5 + 0.5 * jnp.tanh(0.5 * x)


def _dot(a, b):
    return jnp.dot(a, b, preferred_element_type=F32)


def _dot_nt(a, b):
    return lax.dot_general(a, b, (((1,), (1,)), ((), ())), preferred_element_type=F32)


def _dot_tn(a, b):
    return lax.dot_general(a, b, (((0,), (0,)), ((), ())), preferred_element_type=F32)


def _const_spec(shape):
    n = len(shape)
    return pl.BlockSpec(shape, lambda *_: (0,) * n, pipeline_mode=pl.Buffered(1))


def _params(n_grid):
    return pltpu.CompilerParams(
        dimension_semantics=("arbitrary",) * n_grid, vmem_limit_bytes=VMEM_LIMIT)


def _rider(src, lead=(), splits=None):
    k, n = src.shape[-2:]
    return dict(src=src, lead=tuple(lead), k=k, n=n, splits=tuple(splits or ((0, n),)))


def _rider_specs(riders, n_chunks):
    in_specs, out_specs, out_shapes = [], [], []
    for r in riders:
        rows = r['k'] // n_chunks
        assert rows * n_chunks == r['k'] and rows % 16 == 0, (r['k'], n_chunks)

        def src_map(i, lead=r['lead']):
            return lead + (jnp.minimum(i, n_chunks - 1), 0)

        in_specs.append(pl.BlockSpec((None,) * len(r['lead']) + (rows, r['n']), src_map))
        for c0, c1 in r['splits']:
            out_specs.append(pl.BlockSpec((rows, c1 - c0), lambda i: (jnp.minimum(i, n_chunks - 1), 0)))
            out_shapes.append(jax.ShapeDtypeStruct((r['k'], c1 - c0), BF16))
    return in_specs, out_specs, out_shapes


def _run_riders(src_refs, out_refs, splits):
    k = 0
    for src, sp in zip(src_refs, splits):
        for c0, c1 in sp:
            out_refs[k][...] = src[:, c0:c1].astype(BF16)
            k += 1


def _cast_kernel(splits, *refs):
    n = len(splits)
    _run_riders(refs[:n], refs[n:], splits)


def _cast_call(riders, n_chunks):
    in_specs, out_specs, out_shapes = _rider_specs(riders, n_chunks)
    return pl.pallas_call(
        functools.partial(_cast_kernel, tuple(r['splits'] for r in riders)),
        grid=(n_chunks,),
        in_specs=in_specs,
        out_specs=out_specs,
        out_shape=out_shapes,
        compiler_params=_params(1),
        name="cast",
    )(*[r['src'] for r in riders])


def _ffn_body(x, g_pre, g_post, wg_ref, wu_ref, wd_ref):
    h = _rms(x, g_pre).astype(BF16)
    d = None
    for c0 in range(0, D_FF, FF_CHUNK):
        c1 = min(c0 + FF_CHUNK, D_FF)
        a = _dot(h, wg_ref[:, c0:c1])
        u = _dot(h, wu_ref[:, c0:c1])
        m = (a * _sigmoid(a) * u).astype(BF16)
        part = _dot(m, wd_ref[c0:c1, :])
        d = part if d is None else d + part
    return x + 0.5 * _rms(d, g_post)


def _ffn_kernel(n_p, splits, *refs):
    x_ref, xs_ref, gpre_ref, gpost_ref, wg, wu, wd = refs[:7]
    n_r = len(splits)
    rider_src = refs[7:7 + n_r]
    o_ref, os_ref = refs[7 + n_r:9 + n_r]
    rider_out = refs[9 + n_r:]
    i = pl.program_id(0)
    _run_riders(rider_src, rider_out, splits)

    @pl.when(i < n_p)
    def _():
        o_ref[...] = _ffn_body(x_ref[...], gpre_ref[...], gpost_ref[...], wg, wu, wd)

    @pl.when(i == n_p)
    def _():
        os_ref[...] = _ffn_body(xs_ref[...], gpre_ref[...], gpost_ref[...], wg, wu, wd)


def _ffn_call(x2d, xs, g_pre, g_post, wg, wu, wd, riders, *, n_p, tm, in_map, out_map, out_shape):
    ns = xs.shape[0]
    r_in, r_out, r_shapes = _rider_specs(riders, n_p)
    return pl.pallas_call(
        functools.partial(_ffn_kernel, n_p, tuple(r['splits'] for r in riders)),
        grid=(n_p + 1,),
        in_specs=[
            pl.BlockSpec((tm, D_MODEL), in_map),
            _const_spec((ns, D_MODEL)),
            _const_spec((1, D_MODEL)),
            _const_spec((1, D_MODEL)),
            _const_spec((D_MODEL, D_FF)),
            _const_spec((D_MODEL, D_FF)),
            _const_spec((D_FF, D_MODEL)),
        ] + r_in,
        out_specs=[pl.BlockSpec((tm, D_MODEL), out_map),
                   pl.BlockSpec((ns, D_MODEL), lambda i: (0, 0))] + r_out,
        out_shape=[jax.ShapeDtypeStruct(out_shape, F32),
                   jax.ShapeDtypeStruct((ns, D_MODEL), F32)] + r_shapes,
        compiler_params=_params(1),
        name="ffn",
    )(x2d, xs, g_pre, g_post, wg, wu, wd, *[r['src'] for r in riders])


def _hgrn_gates(z, lb):
    ls = jnp.minimum(z, 0.0) - jnp.log1p(jnp.exp(-jnp.abs(z)))
    c = jnp.log1p(-lb) + ls
    a = jnp.log(jnp.maximum(lb, 1e-37))
    lae = jnp.maximum(a, c) + jnp.log1p(jnp.exp(-jnp.abs(a - c)))
    logf = jnp.where(lb > 0.0, lae, c)
    k = (1.0 - lb) * _sigmoid(-z)
    return logf, k


def _hgrn_out(o, xg, ng):
    parts = []
    for hd in range(HG_HEADS):
        oh = o[:, hd * HG_DK:(hd + 1) * HG_DK]
        parts.append(oh * lax.rsqrt(jnp.mean(oh * oh, axis=-1, keepdims=True) + EPS))
    on = jnp.concatenate(parts, axis=-1)
    return on * ng * (xg * _sigmoid(xg))


def _lower_bound(logits_ref, layer):
    lg = logits_ref[...]
    m = jnp.max(lg, axis=0, keepdims=True)
    e = jnp.exp(lg - m)
    den = jnp.sum(e, axis=0, keepdims=True)
    lb = jnp.zeros_like(den)
    for j in range(1, layer + 1):
        lb = lb + e[j:j + 1] / den
    return lb


def _hgrn_prompt_kernel(layer, nb, tl, x_ref, g_ref, w_ref, logits_ref, ng_ref, tri_ref,
                        a_ref, sfin_ref, st_ref, q_s, k_s, v_s, b_s, o_s):
    t = pl.program_id(0)
    nt = pl.num_programs(0)

    @pl.when(t == 0)
    def _():
        st_ref[...] = jnp.zeros_like(st_ref)

    xs = jnp.concatenate([x_ref[:, b * D_MODEL:(b + 1) * D_MODEL] for b in range(nb)], axis=0)
    h = _rms(xs, g_ref[...]).astype(BF16)
    p = _dot(h, w_ref[...])
    xq = p[:, 0 * HG_WIDTH:1 * HG_WIDTH]
    z = p[:, 1 * HG_WIDTH:2 * HG_WIDTH]
    xi = p[:, 2 * HG_WIDTH:3 * HG_WIDTH]
    xg = p[:, 3 * HG_WIDTH:4 * HG_WIDTH]
    lb = _lower_bound(logits_ref, layer)
    logf, kk = _hgrn_gates(z, lb)

    tri = tri_ref[...]
    tr = tri.shape[0]
    l0 = logf.astype(BF16)
    r1 = logf - l0.astype(F32)
    l1 = r1.astype(BF16)
    l2 = (r1 - l1.astype(F32)).astype(BF16)
    for i in range(nb * tl // tr):
        rs = slice(i * tr, (i + 1) * tr)
        b_s[rs, :] = _dot(tri, l0[rs]) + _dot(tri, l1[rs]) + _dot(tri, l2[rs])

    q_s[...] = xq
    k_s[...] = kk
    v_s[...] = xi

    c = GLA_CHUNK
    mid = c // 2 - 1
    row = lax.broadcasted_iota(jnp.int32, (c, c), 0)
    col = lax.broadcasted_iota(jnp.int32, (c, c), 1)
    causal = col <= row

    def chunk(ci, carry):
        pending = []
        for b in range(nb):
            r = pl.ds(pl.multiple_of(b * tl + ci * c, c), c)
            for hd in range(HG_HEADS):
                cs = slice(hd * HG_DK, (hd + 1) * HG_DK)
                bq = b_s[r, cs]
                qc = q_s[r, cs]
                kc = k_s[r, cs]
                vc = v_s[r, cs].astype(BF16)
                b_mid = bq[mid:mid + 1]
                b_last = bq[c - 1:c]
                st = st_ref[b, hd]
                q_in = (qc * jnp.exp(bq)).astype(BF16)
                q_md = (qc * jnp.exp(bq - b_mid)).astype(BF16)
                k_md = (kc * jnp.exp(b_mid - bq)).astype(BF16)
                k_ls = (kc * jnp.exp(b_last - bq)).astype(BF16)
                scores = _dot_nt(q_md, k_md)
                o_inter = _dot_nt(q_in, st.astype(BF16))
                st_ref[b, hd] = st * jnp.exp(b_last) + _dot_tn(vc, k_ls)
                pending.append((r, cs, scores, o_inter, vc))
        for r, cs, scores, o_inter, vc in pending:
            sc = jnp.where(causal, scores, 0.0).astype(BF16)
            o_s[r, cs] = o_inter + _dot(sc, vc)
        return carry

    lax.fori_loop(0, tl // c, chunk, 0)

    a = _hgrn_out(o_s[...], xg, ng_ref[...]).astype(BF16)
    for b in range(nb):
        a_ref[:, b * HG_WIDTH:(b + 1) * HG_WIDTH] = a[b * tl:(b + 1) * tl]

    @pl.when(t == nt - 1)
    def _():
        for b in range(nb):
            for hd in range(HG_HEADS):
                sfin_ref[b, hd] = st_ref[b, hd].T


def _hgrn_prompt_call(x_tm, g, w_hg, logits, ng, *, layer, nb, seq, tl):
    c = GLA_CHUNK
    tr = min(256, nb * tl)
    idx = jnp.arange(tr)
    tri = ((idx[:, None] >= idx[None, :]) & (idx[:, None] // c == idx[None, :] // c)).astype(BF16)
    depth = logits.shape[0]
    rows = nb * tl
    return pl.pallas_call(
        functools.partial(_hgrn_prompt_kernel, layer, nb, tl),
        grid=(seq // tl,),
        in_specs=[
            pl.BlockSpec((tl, nb * D_MODEL), lambda t: (t, 0)),
            _const_spec((1, D_MODEL)),
            _const_spec((D_MODEL, 4 * HG_WIDTH)),
            _const_spec((depth, HG_WIDTH)),
            _const_spec((1, HG_WIDTH)),
            _const_spec((tr, tr)),
        ],
        out_specs=[
            pl.BlockSpec((tl, nb * HG_WIDTH), lambda t: (t, 0)),
            pl.BlockSpec((nb, HG_HEADS, HG_DK, HG_DK), lambda t: (0, 0, 0, 0)),
        ],
        out_shape=[
            jax.ShapeDtypeStruct((seq, nb * HG_WIDTH), BF16),
            jax.ShapeDtypeStruct((nb, HG_HEADS, HG_DK, HG_DK), F32),
        ],
        scratch_shapes=[
            pltpu.VMEM((nb, HG_HEADS, HG_DK, HG_DK), F32),
            pltpu.VMEM((rows, HG_WIDTH), F32),
            pltpu.VMEM((rows, HG_WIDTH), F32),
            pltpu.VMEM((rows, HG_WIDTH), F32),
            pltpu.VMEM((rows, HG_WIDTH), F32),
            pltpu.VMEM((rows, HG_WIDTH), F32),
        ],
        compiler_params=_params(1),
        name="hgrn_prompt",
    )(x_tm, g, w_hg, logits, ng, tri)


def _gmlp_uv(x, g, w_ref, lng, lnb):
    h = _rms(x, g).astype(BF16)
    p = _dot(h, w_ref[...])
    u = _gelu(p[:, :GM_WIDTH])
    gv = _gelu(p[:, GM_WIDTH:])
    mu = jnp.mean(gv, axis=-1, keepdims=True)
    dv = gv - mu
    var = jnp.mean(dv * dv, axis=-1, keepdims=True)
    v = dv * lax.rsqrt(var + EPS) * lng + lnb
    return u, v


def _gmlp_prompt_kernel(tg, x_ref, g_ref, w_ref, lng_ref, lnb_ref, wm_ref, bst_ref, o_ref):
    u, v = _gmlp_uv(x_ref[...], g_ref[...], w_ref, lng_ref[...], lnb_ref[...])
    vb = v.astype(BF16)
    gc = GM_WIDTH // GM_GROUPS
    rows = []
    for j in range(tg // GM_CHUNK):
        parts = []
        for gi in range(GM_GROUPS):
            vj = vb[j * GM_CHUNK:(j + 1) * GM_CHUNK, gi * gc:(gi + 1) * gc]
            parts.append(_dot(wm_ref[gi], vj) + bst_ref[:, gi:gi + 1])
        rows.append(jnp.concatenate(parts, axis=-1))
    mix = jnp.concatenate(rows, axis=0)
    o_ref[...] = (u * mix).astype(BF16)


def _gmlp_prompt_call(x_tm, g, w_gm, lng, lnb, wm, bst, *, nb, seq, tg):
    return pl.pallas_call(
        functools.partial(_gmlp_prompt_kernel, tg),
        grid=(nb, seq // tg),
        in_specs=[
            pl.BlockSpec((tg, D_MODEL), lambda b, t: (t, b)),
            _const_spec((1, D_MODEL)),
            _const_spec((D_MODEL, 2 * GM_WIDTH)),
            _const_spec((1, GM_WIDTH)),
            _const_spec((1, GM_WIDTH)),
            _const_spec((GM_GROUPS, GM_CHUNK, GM_CHUNK)),
            _const_spec((GM_CHUNK, GM_GROUPS)),
        ],
        out_specs=pl.BlockSpec((tg, GM_WIDTH), lambda b, t: (t, b)),
        out_shape=jax.ShapeDtypeStruct((seq, nb * GM_WIDTH), BF16),
        compiler_params=_params(2),
        name="gmlp_prompt",
    )(x_tm, g, w_gm, lng, lnb, wm, bst)


def _s5_kernel(nb, tt, has_h0, relayout, *refs):
    (x_ref, g_ref, w_ref, are_ref, aim_ref, bre_ref, bim_ref, cre_ref, cim_ref, d_ref,
     gw_ref, gb_ref) = refs[:12]
    h0re_ref, h0im_ref = refs[12:14] if has_h0 else (None, None)
    c_ref, hre_ref, him_ref, sre, sim, lay = refs[-6:]
    t = pl.program_id(0)
    wblk = S5_WIDTH // S5_BLOCKS
    sblk = S5_N // S5_BLOCKS

    @pl.when(t == 0)
    def _():
        if has_h0:
            hre_ref[...] = h0re_ref[...]
            him_ref[...] = h0im_ref[...]
        else:
            hre_ref[...] = jnp.zeros_like(hre_ref)
            him_ref[...] = jnp.zeros_like(him_ref)

    if relayout:
        xs = jnp.concatenate([x_ref[:, b * D_MODEL:(b + 1) * D_MODEL] for b in range(nb)], axis=0)
    else:
        xs = x_ref[...]
    h = _rms(xs, g_ref[...]).astype(BF16)
    su = _dot(h, w_ref[...])
    if relayout:
        for b in range(nb):
            for j in range(S5_BLOCKS):
                lay[j, pl.ds(b, tt, stride=nb), :] = su[b * tt:(b + 1) * tt, j * wblk:(j + 1) * wblk]
        su = jnp.concatenate([lay[j] for j in range(S5_BLOCKS)], axis=-1)
    sub = su.astype(BF16)

    def project(j):
        uj = sub[:, j * wblk:(j + 1) * wblk]
        sre[:, j * sblk:(j + 1) * sblk] = _dot(uj, bre_ref[j])
        sim[:, j * sblk:(j + 1) * sblk] = _dot(uj, bim_ref[j])

    def scan(j):
        cols = slice(j * sblk, (j + 1) * sblk)
        a_re = jnp.broadcast_to(are_ref[:, cols], (nb, sblk))
        a_im = jnp.broadcast_to(aim_ref[:, cols], (nb, sblk))
        h_re = hre_ref[:, cols]
        h_im = him_ref[:, cols]
        for i in range(tt):
            r = slice(i * nb, (i + 1) * nb)
            n_re = a_re * h_re - a_im * h_im + sre[r, cols]
            n_im = a_re * h_im + a_im * h_re + sim[r, cols]
            sre[r, cols] = n_re
            sim[r, cols] = n_im
            h_re, h_im = n_re, n_im
        hre_ref[:, cols] = h_re
        him_ref[:, cols] = h_im

    def readout(j):
        hr = sre[:, j * sblk:(j + 1) * sblk].astype(BF16)
        hi = sim[:, j * sblk:(j + 1) * sblk].astype(BF16)
        return _dot(hr, cre_ref[j]) - _dot(hi, cim_ref[j])

    ys = []
    project(0)
    for j in range(S5_BLOCKS):
        if j + 1 < S5_BLOCKS:
            project(j + 1)
        scan(j)
        ys.append(readout(j))
    y = jnp.concatenate(ys, axis=-1) + d_ref[...] * su
    zz = _dot(_gelu(y).astype(BF16), gw_ref[...]) + gb_ref[...]
    c = zz[:, :S5_WIDTH] * _sigmoid(zz[:, S5_WIDTH:])
    if relayout:
        for j in range(S5_BLOCKS):
            lay[j] = c[:, j * wblk:(j + 1) * wblk]
        for b in range(nb):
            cb = jnp.concatenate([lay[j, pl.ds(b, tt, stride=nb), :] for j in range(S5_BLOCKS)], axis=-1)
            c_ref[:, b * S5_WIDTH:(b + 1) * S5_WIDTH] = cb.astype(BF16)
    else:
        c_ref[...] = c.astype(BF16)


def _s5_call(x2d, g, w_s5, a_re, a_im, b_re, b_im, c_re, c_im, d, gw, gb, h0=None,
             *, nb, seq, tt):
    rows = tt * nb
    relayout = seq > 1
    wblk = S5_WIDTH // S5_BLOCKS
    sblk = S5_N // S5_BLOCKS
    x_block = (tt, nb * D_MODEL) if relayout else (rows, D_MODEL)
    c_block = (tt, nb * S5_WIDTH) if relayout else (rows, S5_WIDTH)
    in_specs = [
        pl.BlockSpec(x_block, lambda t: (t, 0)),
        _const_spec((1, D_MODEL)),
        _const_spec((D_MODEL, S5_WIDTH)),
        _const_spec((1, S5_N)),
        _const_spec((1, S5_N)),
        _const_spec((S5_BLOCKS, wblk, sblk)),
        _const_spec((S5_BLOCKS, wblk, sblk)),
        _const_spec((S5_BLOCKS, sblk, wblk)),
        _const_spec((S5_BLOCKS, sblk, wblk)),
        _const_spec((1, S5_WIDTH)),
        _const_spec((S5_WIDTH, 2 * S5_WIDTH)),
        _const_spec((1, 2 * S5_WIDTH)),
    ]
    args = [x2d, g, w_s5, a_re, a_im, b_re, b_im, c_re, c_im, d, gw, gb]
    if h0 is not None:
        in_specs += [_const_spec((nb, S5_N)), _const_spec((nb, S5_N))]
        args += list(h0)
    state_spec = pl.BlockSpec((nb, S5_N), lambda t: (0, 0))
    c_shape = (seq, nb * S5_WIDTH) if relayout else (nb, S5_WIDTH)
    return pl.pallas_call(
        functools.partial(_s5_kernel, nb, tt, h0 is not None, relayout),
        grid=(seq // tt,),
        in_specs=in_specs,
        out_specs=[pl.BlockSpec(c_block, lambda t: (t, 0)), state_spec, state_spec],
        out_shape=[
            jax.ShapeDtypeStruct(c_shape, BF16),
            jax.ShapeDtypeStruct((nb, S5_N), F32),
            jax.ShapeDtypeStruct((nb, S5_N), F32),
        ],
        scratch_shapes=[pltpu.VMEM((rows, S5_N), F32), pltpu.VMEM((rows, S5_N), F32),
                        pltpu.VMEM((S5_BLOCKS, rows, wblk), F32)],
        compiler_params=_params(1),
        name="s5",
    )(*args)


def _merge_body(x, branches, g2, g3, wgate, wbr, wout):
    h = _rms(x, g2).astype(BF16)
    merged = None
    for n, br in enumerate(branches):
        gate = _sigmoid(_dot(h, wgate[:, n * D_MODEL:(n + 1) * D_MODEL]))
        term = gate * _dot(br, wbr[n][...])
        merged = term if merged is None else merged + term
    return x + _rms(_dot(merged.astype(BF16), wout[...]), g3)


def _merge_kernel(n_p, splits, *refs):
    (x_ref, a_ref, b_ref, c_ref, xs_ref, as_ref, bs_ref, cs_ref, g2_ref, g3_ref,
     wgate, wb0, wb1, wb2, wout) = refs[:15]
    n_r = len(splits)
    rider_src = refs[15:15 + n_r]
    o_ref, os_ref = refs[15 + n_r:17 + n_r]
    rider_out = refs[17 + n_r:]
    wbr = (wb0, wb1, wb2)
    i = pl.program_id(0)
    _run_riders(rider_src, rider_out, splits)

    @pl.when(i < n_p)
    def _():
        o_ref[...] = _merge_body(x_ref[...], (a_ref[...], b_ref[...], c_ref[...]),
                                 g2_ref[...], g3_ref[...], wgate, wbr, wout)

    @pl.when(i == n_p)
    def _():
        os_ref[...] = _merge_body(xs_ref[...], (as_ref[...], bs_ref[...], cs_ref[...]),
                                  g2_ref[...], g3_ref[...], wgate, wbr, wout)


def _merge_call(x2d, a, b, c, xs, a_s, b_s, c_s, g2, g3, wgate, wbr, wout, riders,
                *, n_p, tm, xmap, out_shape):
    ns = xs.shape[0]
    r_in, r_out, r_shapes = _rider_specs(riders, n_p)
    return pl.pallas_call(
        functools.partial(_merge_kernel, n_p, tuple(r['splits'] for r in riders)),
        grid=(n_p + 1,),
        in_specs=[
            pl.BlockSpec((tm, D_MODEL), xmap),
            pl.BlockSpec((tm, HG_WIDTH), xmap),
            pl.BlockSpec((tm, GM_WIDTH), xmap),
            pl.BlockSpec((tm, S5_WIDTH), xmap),
            _const_spec((ns, D_MODEL)),
            _const_spec((ns, HG_WIDTH)),
            _const_spec((ns, GM_WIDTH)),
            _const_spec((ns, S5_WIDTH)),
            _const_spec((1, D_MODEL)),
            _const_spec((1, D_MODEL)),
            _const_spec((D_MODEL, N_BRANCH * D_MODEL)),
            _const_spec((HG_WIDTH, D_MODEL)),
            _const_spec((GM_WIDTH, D_MODEL)),
            _const_spec((S5_WIDTH, D_MODEL)),
            _const_spec((D_MODEL, D_MODEL)),
        ] + r_in,
        out_specs=[pl.BlockSpec((tm, D_MODEL), xmap),
                   pl.BlockSpec((ns, D_MODEL), lambda i: (0, 0))] + r_out,
        out_shape=[jax.ShapeDtypeStruct(out_shape, F32),
                   jax.ShapeDtypeStruct((ns, D_MODEL), F32)] + r_shapes,
        compiler_params=_params(1),
        name="merge",
    )(x2d, a, b, c, xs, a_s, b_s, c_s, g2, g3, wgate, *wbr, wout, *[r['src'] for r in riders])


def _sample_proj_kernel(layer, x_ref, g_ref, whg_ref, wgm_ref, logits_ref, lng_ref, lnb_ref,
                        w00_ref, b00_ref, q_ref, f_ref, k_ref, v_ref, xg_ref, bout_ref, vrow_ref):
    x = x_ref[...]
    h = _rms(x, g_ref[...]).astype(BF16)
    p = _dot(h, whg_ref[...])
    z = p[:, 1 * HG_WIDTH:2 * HG_WIDTH]
    logf, kk = _hgrn_gates(z, _lower_bound(logits_ref, layer))
    q_ref[...] = p[:, 0 * HG_WIDTH:1 * HG_WIDTH]
    f_ref[...] = jnp.exp(logf)
    k_ref[...] = kk
    v_ref[...] = p[:, 2 * HG_WIDTH:3 * HG_WIDTH]
    xg_ref[...] = p[:, 3 * HG_WIDTH:4 * HG_WIDTH]
    u, v = _gmlp_uv(x, g_ref[...], wgm_ref, lng_ref[...], lnb_ref[...])
    bout_ref[...] = (u * (v * w00_ref[...] + b00_ref[...])).astype(BF16)
    vrow_ref[...] = v


def _sample_proj_call(x, g, w_hg, w_gm, logits, lng, lnb, w00, b00, *, layer, n):
    depth = logits.shape[0]
    f32_out = jax.ShapeDtypeStruct((n, HG_WIDTH), F32)
    return pl.pallas_call(
        functools.partial(_sample_proj_kernel, layer),
        grid=(1,),
        in_specs=[
            _const_spec((n, D_MODEL)),
            _const_spec((1, D_MODEL)),
            _const_spec((D_MODEL, 4 * HG_WIDTH)),
            _const_spec((D_MODEL, 2 * GM_WIDTH)),
            _const_spec((depth, HG_WIDTH)),
            _const_spec((1, GM_WIDTH)),
            _const_spec((1, GM_WIDTH)),
            _const_spec((1, GM_WIDTH)),
            _const_spec((1, GM_WIDTH)),
        ],
        out_specs=[pl.BlockSpec((n, HG_WIDTH), lambda i: (0, 0))] * 7,
        out_shape=[f32_out] * 5 + [jax.ShapeDtypeStruct((n, GM_WIDTH), BF16), f32_out],
        compiler_params=_params(1),
        name="sample_proj",
    )(x, g, w_hg, w_gm, logits, lng, lnb, w00, b00)


def _hgrn_step_kernel(bb, layer, whole_stack, *refs):
    s_ref, qc_ref, fc_ref, kc_ref, v_ref, xg_ref, ng_ref = refs[:7]
    snew_ref, a_ref, o_s = refs[-3:]
    if whole_stack:
        for j in range(snew_ref.shape[0]):
            if j != layer:
                snew_ref[j] = jnp.zeros(snew_ref.shape[1:], F32)
        snew_ref = snew_ref.at[layer]
    for i in range(bb):
        for hd in range(HG_HEADS):
            cs = slice(hd * HG_DK, (hd + 1) * HG_DK)
            f_col = fc_ref[0, hd, :, i:i + 1]
            k_col = kc_ref[0, hd, :, i:i + 1]
            q_col = qc_ref[0, hd, :, i:i + 1]
            v_row = v_ref[i:i + 1, cs]
            s_new = f_col * s_ref[i, hd] + k_col * v_row
            snew_ref[i, hd] = s_new
            o_s[i:i + 1, cs] = jnp.sum(q_col * s_new, axis=0, keepdims=True)
    a_ref[...] = _hgrn_out(o_s[...], xg_ref[...], ng_ref[...]).astype(BF16)


def _hgrn_step_call(s_all, prev, qc, fc, kc, v, xg, ng, *, layer, n, bb):
    col_spec = pl.BlockSpec((1, HG_HEADS, HG_DK, bb), lambda i: (i, 0, 0, 0))
    row_spec = pl.BlockSpec((bb, HG_WIDTH), lambda i: (i, 0))
    st_spec = pl.BlockSpec((None, bb, HG_HEADS, HG_DK, HG_DK), lambda i: (layer, i, 0, 0, 0))
    in_specs = [st_spec, col_spec, col_spec, col_spec, row_spec, row_spec, _const_spec((1, HG_WIDTH))]
    args = [s_all, qc, fc, kc, v, xg, ng]
    aliases = {}
    out_st_spec = st_spec
    if prev is not None:
        in_specs.append(pl.BlockSpec(memory_space=pl.ANY))
        args.append(prev)
        aliases = {len(args) - 1: 0}
    else:
        out_st_spec = pl.BlockSpec((s_all.shape[0], bb, HG_HEADS, HG_DK, HG_DK), lambda i: (0, i, 0, 0, 0))
    return pl.pallas_call(
        functools.partial(_hgrn_step_kernel, bb, layer, prev is None),
        grid=(n // bb,),
        in_specs=in_specs,
        out_specs=[out_st_spec, row_spec],
        out_shape=[
            jax.ShapeDtypeStruct(s_all.shape, F32),
            jax.ShapeDtypeStruct((n, HG_WIDTH), BF16),
        ],
        scratch_shapes=[pltpu.VMEM((bb, HG_WIDTH), F32)],
        input_output_aliases=aliases,
        compiler_params=_params(1),
        name="hgrn_step",
    )(*args)


def _s5_params(lam_re, lam_im, log_dt, b_re, b_im, c_re, c_im):
    lr = lam_re.astype(F32)
    li = lam_im.astype(F32)
    dt = jnp.exp(log_dt.astype(F32))[:, None]
    mag = jnp.exp(lr * dt)
    a_re = mag * jnp.cos(li * dt)
    a_im = mag * jnp.sin(li * dt)
    den = lr * lr + li * li
    k_re = ((a_re - 1.0) * lr + a_im * li) / den
    k_im = (a_im * lr - (a_re - 1.0) * li) / den
    bre = b_re.astype(F32)
    bim = b_im.astype(F32)
    bb_re = k_re[..., None] * bre - k_im[..., None] * bim
    bb_im = k_re[..., None] * bim + k_im[..., None] * bre
    gpb = S5_GROUPS // S5_BLOCKS
    eye = jnp.eye(gpb, dtype=F32)

    def pack_in(w):
        w = w.reshape(S5_BLOCKS, gpb, S5_STATE, S5_GROUP)
        out = jnp.einsum('jgph,gk->jghkp', w, eye)
        return out.reshape(S5_BLOCKS, gpb * S5_GROUP, gpb * S5_STATE).astype(BF16)

    def pack_out(w):
        w = w.reshape(S5_BLOCKS, gpb, S5_GROUP, S5_STATE)
        out = jnp.einsum('jghp,gk->jgpkh', w, eye)
        return out.reshape(S5_BLOCKS, gpb * S5_STATE, gpb * S5_GROUP).astype(BF16)

    return (a_re.reshape(1, S5_N), a_im.reshape(1, S5_N), pack_in(bb_re), pack_in(bb_im),
            pack_out(c_re.astype(F32)), pack_out(c_im.astype(F32)))


def kernel(x_prompt, x_sample, state_hgrn, state_s5_re, state_s5_im, norm_g, ffn_w_gate, ffn_w_up, ffn_w_down, w_in, hgrn_lb_logits, hgrn_norm_g, gmlp_ws, gmlp_bs, gmlp_norm_g, gmlp_norm_b, s5_lam_re, s5_lam_im, s5_log_dt, s5_b_re, s5_b_im, s5_c_re, s5_c_im, s5_d, s5_glu_w, s5_glu_b, w_branch, w_out):
    nb, seq, _ = x_prompt.shape
    ns = x_sample.shape[0]
    depth = norm_g.shape[0]
    tm = 1024
    tl = 128
    tg = 512
    tt = 64
    bb = 8
    nt = seq // tm
    n_p = nb * nt

    def bmajor(i):
        return (jnp.minimum(i, n_p - 1), 0)

    def tmajor(i):
        j = jnp.minimum(i, n_p - 1)
        return (j % nt, j // nt)

    def ffn_riders(l, which):
        return [_rider(ffn_w_gate, (l, which)), _rider(ffn_w_up, (l, which)),
                _rider(ffn_w_down, (l, which))]

    def mixer_riders(l):
        in_splits = ((0, COL_GM), (COL_GM, COL_S5), (COL_S5, COL_GATE), (COL_GATE, N_IN))
        return ([_rider(w_in, (l,), in_splits)]
                + [_rider(w_branch, (l, n)) for n in range(N_BRANCH)]
                + [_rider(w_out, (l,)), _rider(s5_glu_w, (l,))])

    yp = x_prompt.reshape(nb * seq, D_MODEL)
    ys = x_sample.reshape(ns, D_MODEL)
    logits = hgrn_lb_logits.astype(F32)
    tril = jnp.tril(jnp.ones((GM_CHUNK, GM_CHUNK), F32))
    ffn_w = _cast_call(ffn_riders(0, 0), n_p)

    outs = {k: [] for k in ('hg_p', 're_p', 'im_p', 're_s', 'im_s', 'v_s')}
    hg_s = None
    for l in range(depth):
        g = [norm_g[l, i].reshape(1, D_MODEL) for i in range(6)]
        ng = hgrn_norm_g[l].reshape(1, HG_WIDTH)
        lng = gmlp_norm_g[l].reshape(1, GM_WIDTH)
        lnb = gmlp_norm_b[l].reshape(1, GM_WIDTH)
        wm = (gmlp_ws[l] * tril).astype(BF16)
        bst = gmlp_bs[l].T
        gc = GM_WIDTH // GM_GROUPS
        w00 = jnp.repeat(gmlp_ws[l][:, 0, 0], gc).reshape(1, GM_WIDTH)
        b00 = jnp.repeat(gmlp_bs[l][:, 0], gc).reshape(1, GM_WIDTH)
        s5p = _s5_params(s5_lam_re[l], s5_lam_im[l], s5_log_dt[l], s5_b_re[l], s5_b_im[l],
                         s5_c_re[l], s5_c_im[l])
        s5d = s5_d[l].reshape(1, S5_WIDTH)
        gb = s5_glu_b[l].reshape(1, 2 * S5_WIDTH)
        tm_shape = (seq, nb * D_MODEL)

        x1, xs1, w_hg, w_gm, w_s5, w_gate, wb0, wb1, wb2, wout, gw = _ffn_call(
            yp, ys, g[0], g[1], *ffn_w, mixer_riders(l), n_p=n_p, tm=tm,
            in_map=bmajor if l == 0 else tmajor, out_map=tmajor, out_shape=tm_shape)
        a_p, s_hg = _hgrn_prompt_call(x1, g[2], w_hg, logits, ng, layer=l, nb=nb, seq=seq, tl=tl)
        b_p = _gmlp_prompt_call(x1, g[2], w_gm, lng, lnb, wm, bst, nb=nb, seq=seq, tg=tg)
        c_p, h_re, h_im = _s5_call(x1, g[2], w_s5, *s5p, s5d, gw, gb, nb=nb, seq=seq, tt=tt)
        outs['hg_p'].append(s_hg)
        outs['re_p'].append(h_re.reshape(nb, S5_GROUPS, S5_STATE))
        outs['im_p'].append(h_im.reshape(nb, S5_GROUPS, S5_STATE))

        q, f, k, v, xg, b_s, v_rows = _sample_proj_call(
            xs1, g[2], w_hg, w_gm, logits, lng, lnb, w00, b00, layer=l, n=ns)

        def cols(a):
            return a.reshape(ns // bb, bb, HG_HEADS, HG_DK).transpose(0, 2, 3, 1)

        hg_s, a_s = _hgrn_step_call(state_hgrn, hg_s, cols(q), cols(f), cols(k), v, xg, ng,
                                    layer=l, n=ns, bb=bb)
        h0 = (state_s5_re[l].reshape(ns, S5_N), state_s5_im[l].reshape(ns, S5_N))
        c_s, hs_re, hs_im = _s5_call(xs1, g[2], w_s5, *s5p, s5d, gw, gb, h0, nb=ns, seq=1, tt=1)

        x2, xs2, *ffn_w = _merge_call(x1, a_p, b_p, c_p, xs1, a_s, b_s, c_s, g[2], g[3],
                                      w_gate, (wb0, wb1, wb2), wout, ffn_riders(l, 1),
                                      n_p=n_p, tm=tm, xmap=tmajor, out_shape=tm_shape)
        last = l == depth - 1
        yp, ys, *ffn_next = _ffn_call(x2, xs2, g[4], g[5], *ffn_w,
                                      [] if last else ffn_riders(l + 1, 0), n_p=n_p, tm=tm,
                                      in_map=tmajor, out_map=bmajor if last else tmajor,
                                      out_shape=(nb * seq, D_MODEL) if last else tm_shape)
        ffn_w = ffn_next
        outs['re_s'].append(hs_re.reshape(ns, S5_GROUPS, S5_STATE))
        outs['im_s'].append(hs_im.reshape(ns, S5_GROUPS, S5_STATE))
        outs['v_s'].append(v_rows.reshape(ns, 1, GM_WIDTH))

    return (yp.reshape(nb, seq, D_MODEL), ys.reshape(ns, 1, D_MODEL),
            jnp.stack(outs['hg_p']), jnp.stack(outs['re_p']), jnp.stack(outs['im_p']),
            hg_s, jnp.stack(outs['re_s']), jnp.stack(outs['im_s']),
            jnp.stack(outs['v_s']))
```

```python
import functools
import math

import jax
import jax.numpy as jnp
from jax import lax
from jax.experimental import pallas as pl
from jax.experimental.pallas import tpu as pltpu

F32 = jnp.float32
BF16 = jnp.bfloat16

D_MODEL = 1024
D_FF = 2816
HG_HEADS = 4
HG_DK = 128
HG_WIDTH = HG_HEADS * HG_DK
GM_GROUPS = 4
GM_CHUNK = 128
GM_WIDTH = 512
S5_WIDTH = 512
S5_GROUP = 16
S5_GROUPS = S5_WIDTH // S5_GROUP
S5_STATE = 64
S5_N = S5_GROUPS * S5_STATE
S5_BLOCKS = 4
N_BRANCH = 3
EPS = 1e-6

COL_GM = 4 * HG_WIDTH
COL_S5 = COL_GM + 2 * GM_WIDTH
COL_GATE = COL_S5 + S5_WIDTH
N_IN = COL_GATE + N_BRANCH * D_MODEL

GLA_CHUNK = 32
FF_CHUNK = 512
VMEM_LIMIT = 56 * 1024 * 1024


def _rms(x, g):
    return x * lax.rsqrt(jnp.mean(x * x, axis=-1, keepdims=True) + EPS) * g


def _gelu(x):
    c = math.sqrt(2.0 / math.pi)
    hx = 0.5 * x
    return hx + hx * jnp.tanh(x * (c + (c * 0.044715) * (x * x)))


def _sigmoid(x):
    return 0.5 + 0.5 * jnp.tanh(0.5 * x)


def _dot(a, b):
    return jnp.dot(a, b, preferred_element_type=F32)


def _dot_nt(a, b):
    return lax.dot_general(a, b, (((1,), (1,)), ((), ())), preferred_element_type=F32)


def _dot_tn(a, b):
    return lax.dot_general(a, b, (((0,), (0,)), ((), ())), preferred_element_type=F32)


def _const_spec(shape):
    n = len(shape)
    return pl.BlockSpec(shape, lambda *_: (0,) * n, pipeline_mode=pl.Buffered(1))


def _params(n_grid):
    return pltpu.CompilerParams(
        dimension_semantics=("arbitrary",) * n_grid, vmem_limit_bytes=VMEM_LIMIT)


def _rider(src, lead=(), splits=None):
    k, n = src.shape[-2:]
    return dict(src=src, lead=tuple(lead), k=k, n=n, splits=tuple(splits or ((0, n),)))


def _rider_specs(riders, n_chunks):
    in_specs, out_specs, out_shapes = [], [], []
    for r in riders:
        rows = r['k'] // n_chunks
        assert rows * n_chunks == r['k'] and rows % 16 == 0, (r['k'], n_chunks)

        def src_map(i, lead=r['lead']):
            return lead + (jnp.minimum(i, n_chunks - 1), 0)

        in_specs.append(pl.BlockSpec((None,) * len(r['lead']) + (rows, r['n']), src_map))
        for c0, c1 in r['splits']:
            out_specs.append(pl.BlockSpec((rows, c1 - c0), lambda i: (jnp.minimum(i, n_chunks - 1), 0)))
            out_shapes.append(jax.ShapeDtypeStruct((r['k'], c1 - c0), BF16))
    return in_specs, out_specs, out_shapes


def _run_riders(src_refs, out_refs, splits):
    k = 0
    for src, sp in zip(src_refs, splits):
        for c0, c1 in sp:
            out_refs[k][...] = src[:, c0:c1].astype(BF16)
            k += 1


def _cast_kernel(splits, *refs):
    n = len(splits)
    _run_riders(refs[:n], refs[n:], splits)


def _cast_call(riders, n_chunks):
    in_specs, out_specs, out_shapes = _rider_specs(riders, n_chunks)
    return pl.pallas_call(
        functools.partial(_cast_kernel, tuple(r['splits'] for r in riders)),
        grid=(n_chunks,),
        in_specs=in_specs,
        out_specs=out_specs,
        out_shape=out_shapes,
        compiler_params=_params(1),
        name="cast",
    )(*[r['src'] for r in riders])


def _ffn_body(x, g_pre, g_post, wg_ref, wu_ref, wd_ref):
    h = _rms(x, g_pre).astype(BF16)
    d = None
    for c0 in range(0, D_FF, FF_CHUNK):
        c1 = min(c0 + FF_CHUNK, D_FF)
        a = _dot(h, wg_ref[:, c0:c1])
        u = _dot(h, wu_ref[:, c0:c1])
        m = (a * _sigmoid(a) * u).astype(BF16)
        part = _dot(m, wd_ref[c0:c1, :])
        d = part if d is None else d + part
    return x + 0.5 * _rms(d, g_post)


def _ffn_kernel(n_p, splits, *refs):
    x_ref, xs_ref, gpre_ref, gpost_ref, wg, wu, wd = refs[:7]
    n_r = len(splits)
    rider_src = refs[7:7 + n_r]
    o_ref, os_ref = refs[7 + n_r:9 + n_r]
    rider_out = refs[9 + n_r:]
    i = pl.program_id(0)
    _run_riders(rider_src, rider_out, splits)

    @pl.when(i < n_p)
    def _():
        o_ref[...] = _ffn_body(x_ref[...], gpre_ref[...], gpost_ref[...], wg, wu, wd)

    @pl.when(i == n_p)
    def _():
        os_ref[...] = _ffn_body(xs_ref[...], gpre_ref[...], gpost_ref[...], wg, wu, wd)


def _ffn_call(x2d, xs, g_pre, g_post, wg, wu, wd, riders, *, n_p, tm, in_map, out_map, out_shape):
    ns = xs.shape[0]
    r_in, r_out, r_shapes = _rider_specs(riders, n_p)
    return pl.pallas_call(
        functools.partial(_ffn_kernel, n_p, tuple(r['splits'] for r in riders)),
        grid=(n_p + 1,),
        in_specs=[
            pl.BlockSpec((tm, D_MODEL), in_map),
            _const_spec((ns, D_MODEL)),
            _const_spec((1, D_MODEL)),
            _const_spec((1, D_MODEL)),
            _const_spec((D_MODEL, D_FF)),
            _const_spec((D_MODEL, D_FF)),
            _const_spec((D_FF, D_MODEL)),
        ] + r_in,
        out_specs=[pl.BlockSpec((tm, D_MODEL), out_map),
                   pl.BlockSpec((ns, D_MODEL), lambda i: (0, 0))] + r_out,
        out_shape=[jax.ShapeDtypeStruct(out_shape, F32),
                   jax.ShapeDtypeStruct((ns, D_MODEL), F32)] + r_shapes,
        compiler_params=_params(1),
        name="ffn",
    )(x2d, xs, g_pre, g_post, wg, wu, wd, *[r['src'] for r in riders])


def _hgrn_gates(z, lb):
    ls = jnp.minimum(z, 0.0) - jnp.log1p(jnp.exp(-jnp.abs(z)))
    c = jnp.log1p(-lb) + ls
    a = jnp.log(jnp.maximum(lb, 1e-37))
    lae = jnp.maximum(a, c) + jnp.log1p(jnp.exp(-jnp.abs(a - c)))
    logf = jnp.where(lb > 0.0, lae, c)
    k = (1.0 - lb) * _sigmoid(-z)
    return logf, k


def _hgrn_out(o, xg, ng):
    parts = []
    for hd in range(HG_HEADS):
        oh = o[:, hd * HG_DK:(hd + 1) * HG_DK]
        parts.append(oh * lax.rsqrt(jnp.mean(oh * oh, axis=-1, keepdims=True) + EPS))
    on = jnp.concatenate(parts, axis=-1)
    return on * ng * (xg * _sigmoid(xg))


def _lower_bound(logits_ref, layer):
    lg = logits_ref[...]
    m = jnp.max(lg, axis=0, keepdims=True)
    e = jnp.exp(lg - m)
    den = jnp.sum(e, axis=0, keepdims=True)
    lb = jnp.zeros_like(den)
    for j in range(1, layer + 1):
        lb = lb + e[j:j + 1] / den
    return lb


def _hgrn_prompt_kernel(layer, nb, tl, x_ref, g_ref, w_ref, logits_ref, ng_ref, tri_ref,
                        a_ref, sfin_ref, st_ref, q_s, k_s, v_s, b_s, o_s):
    t = pl.program_id(0)
    nt = pl.num_programs(0)

    @pl.when(t == 0)
    def _():
        st_ref[...] = jnp.zeros_like(st_ref)

    xs = jnp.concatenate([x_ref[:, b * D_MODEL:(b + 1) * D_MODEL] for b in range(nb)], axis=0)
    h = _rms(xs, g_ref[...]).astype(BF16)
    p = _dot(h, w_ref[...])
    xq = p[:, 0 * HG_WIDTH:1 * HG_WIDTH]
    z = p[:, 1 * HG_WIDTH:2 * HG_WIDTH]
    xi = p[:, 2 * HG_WIDTH:3 * HG_WIDTH]
    xg = p[:, 3 * HG_WIDTH:4 * HG_WIDTH]
    lb = _lower_bound(logits_ref, layer)
    logf, kk = _hgrn_gates(z, lb)

    tri = tri_ref[...]
    tr = tri.shape[0]
    l0 = logf.astype(BF16)
    r1 = logf - l0.astype(F32)
    l1 = r1.astype(BF16)
    l2 = (r1 - l1.astype(F32)).astype(BF16)
    for i in range(nb * tl // tr):
        rs = slice(i * tr, (i + 1) * tr)
        b_s[rs, :] = _dot(tri, l0[rs]) + _dot(tri, l1[rs]) + _dot(tri, l2[rs])

    q_s[...] = xq
    k_s[...] = kk
    v_s[...] = xi

    c = GLA_CHUNK
    mid = c // 2 - 1
    row = lax.broadcasted_iota(jnp.int32, (c, c), 0)
    col = lax.broadcasted_iota(jnp.int32, (c, c), 1)
    causal = col <= row

    def chunk(ci, carry):
        pending = []
        for b in range(nb):
            r = pl.ds(pl.multiple_of(b * tl + ci * c, c), c)
            for hd in range(HG_HEADS):
                cs = slice(hd * HG_DK, (hd + 1) * HG_DK)
                bq = b_s[r, cs]
                qc = q_s[r, cs]
                kc = k_s[r, cs]
                vc = v_s[r, cs].astype(BF16)
                b_mid = bq[mid:mid + 1]
                b_last = bq[c - 1:c]
                st = st_ref[b, hd]
                q_in = (qc * jnp.exp(bq)).astype(BF16)
                q_md = (qc * jnp.exp(bq - b_mid)).astype(BF16)
                k_md = (kc * jnp.exp(b_mid - bq)).astype(BF16)
                k_ls = (kc * jnp.exp(b_last - bq)).astype(BF16)
                scores = _dot_nt(q_md, k_md)
                o_inter = _dot_nt(q_in, st.astype(BF16))
                st_ref[b, hd] = st * jnp.exp(b_last) + _dot_tn(vc, k_ls)
                pending.append((r, cs, scores, o_inter, vc))
        for r, cs, scores, o_inter, vc in pending:
            sc = jnp.where(causal, scores, 0.0).astype(BF16)
            o_s[r, cs] = o_inter + _dot(sc, vc)
        return carry

    lax.fori_loop(0, tl // c, chunk, 0)

    a = _hgrn_out(o_s[...], xg, ng_ref[...]).astype(BF16)
    for b in range(nb):
        a_ref[:, b * HG_WIDTH:(b + 1) * HG_WIDTH] = a[b * tl:(b + 1) * tl]

    @pl.when(t == nt - 1)
    def _():
        for b in range(nb):
            for hd in range(HG_HEADS):
                sfin_ref[b, hd] = st_ref[b, hd].T


def _hgrn_prompt_call(x_tm, g, w_hg, logits, ng, *, layer, nb, seq, tl):
    c = GLA_CHUNK
    tr = min(256, nb * tl)
    idx = jnp.arange(tr)
    tri = ((idx[:, None] >= idx[None, :]) & (idx[:, None] // c == idx[None, :] // c)).astype(BF16)
    depth = logits.shape[0]
    rows = nb * tl
    return pl.pallas_call(
        functools.partial(_hgrn_prompt_kernel, layer, nb, tl),
        grid=(seq // tl,),
        in_specs=[
            pl.BlockSpec((tl, nb * D_MODEL), lambda t: (t, 0)),
            _const_spec((1, D_MODEL)),
            _const_spec((D_MODEL, 4 * HG_WIDTH)),
            _const_spec((depth, HG_WIDTH)),
            _const_spec((1, HG_WIDTH)),
            _const_spec((tr, tr)),
        ],
        out_specs=[
            pl.BlockSpec((tl, nb * HG_WIDTH), lambda t: (t, 0)),
            pl.BlockSpec((nb, HG_HEADS, HG_DK, HG_DK), lambda t: (0, 0, 0, 0)),
        ],
        out_shape=[
            jax.ShapeDtypeStruct((seq, nb * HG_WIDTH), BF16),
            jax.ShapeDtypeStruct((nb, HG_HEADS, HG_DK, HG_DK), F32),
        ],
        scratch_shapes=[
            pltpu.VMEM((nb, HG_HEADS, HG_DK, HG_DK), F32),
            pltpu.VMEM((rows, HG_WIDTH), F32),
            pltpu.VMEM((rows, HG_WIDTH), F32),
            pltpu.VMEM((rows, HG_WIDTH), F32),
            pltpu.VMEM((rows, HG_WIDTH), F32),
            pltpu.VMEM((rows, HG_WIDTH), F32),
        ],
        compiler_params=_params(1),
        name="hgrn_prompt",
    )(x_tm, g, w_hg, logits, ng, tri)


def _gmlp_uv(x, g, w_ref, lng, lnb):
    h = _rms(x, g).astype(BF16)
    p = _dot(h, w_ref[...])
    u = _gelu(p[:, :GM_WIDTH])
    gv = _gelu(p[:, GM_WIDTH:])
    mu = jnp.mean(gv, axis=-1, keepdims=True)
    dv = gv - mu
    var = jnp.mean(dv * dv, axis=-1, keepdims=True)
    v = dv * lax.rsqrt(var + EPS) * lng + lnb
    return u, v


def _gmlp_prompt_kernel(tg, x_ref, g_ref, w_ref, lng_ref, lnb_ref, wm_ref, bst_ref, o_ref):
    u, v = _gmlp_uv(x_ref[...], g_ref[...], w_ref, lng_ref[...], lnb_ref[...])
    vb = v.astype(BF16)
    gc = GM_WIDTH // GM_GROUPS
    rows = []
    for j in range(tg // GM_CHUNK):
        parts = []
        for gi in range(GM_GROUPS):
            vj = vb[j * GM_CHUNK:(j + 1) * GM_CHUNK, gi * gc:(gi + 1) * gc]
            parts.append(_dot(wm_ref[gi], vj) + bst_ref[:, gi:gi + 1])
        rows.append(jnp.concatenate(parts, axis=-1))
    mix = jnp.concatenate(rows, axis=0)
    o_ref[...] = (u * mix).astype(BF16)


def _gmlp_prompt_call(x_tm, g, w_gm, lng, lnb, wm, bst, *, nb, seq, tg):
    return pl.pallas_call(
        functools.partial(_gmlp_prompt_kernel, tg),
        grid=(nb, seq // tg),
        in_specs=[
            pl.BlockSpec((tg, D_MODEL), lambda b, t: (t, b)),
            _const_spec((1, D_MODEL)),
            _const_spec((D_MODEL, 2 * GM_WIDTH)),
            _const_spec((1, GM_WIDTH)),
            _const_spec((1, GM_WIDTH)),
            _const_spec((GM_GROUPS, GM_CHUNK, GM_CHUNK)),
            _const_spec((GM_CHUNK, GM_GROUPS)),
        ],
        out_specs=pl.BlockSpec((tg, GM_WIDTH), lambda b, t: (t, b)),
        out_shape=jax.ShapeDtypeStruct((seq, nb * GM_WIDTH), BF16),
        compiler_params=_params(2),
        name="gmlp_prompt",
    )(x_tm, g, w_gm, lng, lnb, wm, bst)


def _s5_kernel(nb, tt, has_h0, relayout, *refs):
    (x_ref, g_ref, w_ref, are_ref, aim_ref, bre_ref, bim_ref, cre_ref, cim_ref, d_ref,
     gw_ref, gb_ref) = refs[:12]
    h0re_ref, h0im_ref = refs[12:14] if has_h0 else (None, None)
    c_ref, hre_ref, him_ref, sre, sim, lay = refs[-6:]
    t = pl.program_id(0)
    wblk = S5_WIDTH // S5_BLOCKS
    sblk = S5_N // S5_BLOCKS

    @pl.when(t == 0)
    def _():
        if has_h0:
            hre_ref[...] = h0re_ref[...]
            him_ref[...] = h0im_ref[...]
        else:
            hre_ref[...] = jnp.zeros_like(hre_ref)
            him_ref[...] = jnp.zeros_like(him_ref)

    if relayout:
        xs = jnp.concatenate([x_ref[:, b * D_MODEL:(b + 1) * D_MODEL] for b in range(nb)], axis=0)
    else:
        xs = x_ref[...]
    h = _rms(xs, g_ref[...]).astype(BF16)
    su = _dot(h, w_ref[...])
    if relayout:
        for b in range(nb):
            for j in range(S5_BLOCKS):
                lay[j, pl.ds(b, tt, stride=nb), :] = su[b * tt:(b + 1) * tt, j * wblk:(j + 1) * wblk]
        su = jnp.concatenate([lay[j] for j in range(S5_BLOCKS)], axis=-1)
    sub = su.astype(BF16)

    def project(j):
        uj = sub[:, j * wblk:(j + 1) * wblk]
        sre[:, j * sblk:(j + 1) * sblk] = _dot(uj, bre_ref[j])
        sim[:, j * sblk:(j + 1) * sblk] = _dot(uj, bim_ref[j])

    def scan(j):
        cols = slice(j * sblk, (j + 1) * sblk)
        a_re = jnp.broadcast_to(are_ref[:, cols], (nb, sblk))
        a_im = jnp.broadcast_to(aim_ref[:, cols], (nb, sblk))
        h_re = hre_ref[:, cols]
        h_im = him_ref[:, cols]
        for i in range(tt):
            r = slice(i * nb, (i + 1) * nb)
            n_re = a_re * h_re - a_im * h_im + sre[r, cols]
            n_im = a_re * h_im + a_im * h_re + sim[r, cols]
            sre[r, cols] = n_re
            sim[r, cols] = n_im
            h_re, h_im = n_re, n_im
        hre_ref[:, cols] = h_re
        him_ref[:, cols] = h_im

    def readout(j):
        hr = sre[:, j * sblk:(j + 1) * sblk].astype(BF16)
        hi = sim[:, j * sblk:(j + 1) * sblk].astype(BF16)
        return _dot(hr, cre_ref[j]) - _dot(hi, cim_ref[j])

    ys = []
    project(0)
    for j in range(S5_BLOCKS):
        if j + 1 < S5_BLOCKS:
            project(j + 1)
        scan(j)
        ys.append(readout(j))
    y = jnp.concatenate(ys, axis=-1) + d_ref[...] * su
    zz = _dot(_gelu(y).astype(BF16), gw_ref[...]) + gb_ref[...]
    c = zz[:, :S5_WIDTH] * _sigmoid(zz[:, S5_WIDTH:])
    if relayout:
        for j in range(S5_BLOCKS):
            lay[j] = c[:, j * wblk:(j + 1) * wblk]
        for b in range(nb):
            cb = jnp.concatenate([lay[j, pl.ds(b, tt, stride=nb), :] for j in range(S5_BLOCKS)], axis=-1)
            c_ref[:, b * S5_WIDTH:(b + 1) * S5_WIDTH] = cb.astype(BF16)
    else:
        c_ref[...] = c.astype(BF16)


def _s5_call(x2d, g, w_s5, a_re, a_im, b_re, b_im, c_re, c_im, d, gw, gb, h0=None,
             *, nb, seq, tt):
    rows = tt * nb
    relayout = seq > 1
    wblk = S5_WIDTH // S5_BLOCKS
    sblk = S5_N // S5_BLOCKS
    x_block = (tt, nb * D_MODEL) if relayout else (rows, D_MODEL)
    c_block = (tt, nb * S5_WIDTH) if relayout else (rows, S5_WIDTH)
    in_specs = [
        pl.BlockSpec(x_block, lambda t: (t, 0)),
        _const_spec((1, D_MODEL)),
        _const_spec((D_MODEL, S5_WIDTH)),
        _const_spec((1, S5_N)),
        _const_spec((1, S5_N)),
        _const_spec((S5_BLOCKS, wblk, sblk)),
        _const_spec((S5_BLOCKS, wblk, sblk)),
        _const_spec((S5_BLOCKS, sblk, wblk)),
        _const_spec((S5_BLOCKS, sblk, wblk)),
        _const_spec((1, S5_WIDTH)),
        _const_spec((S5_WIDTH, 2 * S5_WIDTH)),
        _const_spec((1, 2 * S5_WIDTH)),
    ]
    args = [x2d, g, w_s5, a_re, a_im, b_re, b_im, c_re, c_im, d, gw, gb]
    if h0 is not None:
        in_specs += [_const_spec((nb, S5_N)), _const_spec((nb, S5_N))]
        args += list(h0)
    state_spec = pl.BlockSpec((nb, S5_N), lambda t: (0, 0))
    c_shape = (seq, nb * S5_WIDTH) if relayout else (nb, S5_WIDTH)
    return pl.pallas_call(
        functools.partial(_s5_kernel, nb, tt, h0 is not None, relayout),
        grid=(seq // tt,),
        in_specs=in_specs,
        out_specs=[pl.BlockSpec(c_block, lambda t: (t, 0)), state_spec, state_spec],
        out_shape=[
            jax.ShapeDtypeStruct(c_shape, BF16),
            jax.ShapeDtypeStruct((nb, S5_N), F32),
            jax.ShapeDtypeStruct((nb, S5_N), F32),
        ],
        scratch_shapes=[pltpu.VMEM((rows, S5_N), F32), pltpu.VMEM((rows, S5_N), F32),
                        pltpu.VMEM((S5_BLOCKS, rows, wblk), F32)],
        compiler_params=_params(1),
        name="s5",
    )(*args)


def _merge_body(x, branches, g2, g3, wgate, wbr, wout):
    h = _rms(x, g2).astype(BF16)
    merged = None
    for n, br in enumerate(branches):
        gate = _sigmoid(_dot(h, wgate[:, n * D_MODEL:(n + 1) * D_MODEL]))
        term = gate * _dot(br, wbr[n][...])
        merged = term if merged is None else merged + term
    return x + _rms(_dot(merged.astype(BF16), wout[...]), g3)


def _merge_kernel(n_p, splits, *refs):
    (x_ref, a_ref, b_ref, c_ref, xs_ref, as_ref, bs_ref, cs_ref, g2_ref, g3_ref,
     wgate, wb0, wb1, wb2, wout) = refs[:15]
    n_r = len(splits)
    rider_src = refs[15:15 + n_r]
    o_ref, os_ref = refs[15 + n_r:17 + n_r]
    rider_out = refs[17 + n_r:]
    wbr = (wb0, wb1, wb2)
    i = pl.program_id(0)
    _run_riders(rider_src, rider_out, splits)

    @pl.when(i < n_p)
    def _():
        o_ref[...] = _merge_body(x_ref[...], (a_ref[...], b_ref[...], c_ref[...]),
                                 g2_ref[...], g3_ref[...], wgate, wbr, wout)

    @pl.when(i == n_p)
    def _():
        os_ref[...] = _merge_body(xs_ref[...], (as_ref[...], bs_ref[...], cs_ref[...]),
                                  g2_ref[...], g3_ref[...], wgate, wbr, wout)


def _merge_call(x2d, a, b, c, xs, a_s, b_s, c_s, g2, g3, wgate, wbr, wout, riders,
                *, n_p, tm, xmap, out_shape):
    ns = xs.shape[0]
    r_in, r_out, r_shapes = _rider_specs(riders, n_p)
    return pl.pallas_call(
        functools.partial(_merge_kernel, n_p, tuple(r['splits'] for r in riders)),
        grid=(n_p + 1,),
        in_specs=[
            pl.BlockSpec((tm, D_MODEL), xmap),
            pl.BlockSpec((tm, HG_WIDTH), xmap),
            pl.BlockSpec((tm, GM_WIDTH), xmap),
            pl.BlockSpec((tm, S5_WIDTH), xmap),
            _const_spec((ns, D_MODEL)),
            _const_spec((ns, HG_WIDTH)),
            _const_spec((ns, GM_WIDTH)),
            _const_spec((ns, S5_WIDTH)),
            _const_spec((1, D_MODEL)),
            _const_spec((1, D_MODEL)),
            _const_spec((D_MODEL, N_BRANCH * D_MODEL)),
            _const_spec((HG_WIDTH, D_MODEL)),
            _const_spec((GM_WIDTH, D_MODEL)),
            _const_spec((S5_WIDTH, D_MODEL)),
            _const_spec((D_MODEL, D_MODEL)),
        ] + r_in,
        out_specs=[pl.BlockSpec((tm, D_MODEL), xmap),
                   pl.BlockSpec((ns, D_MODEL), lambda i: (0, 0))] + r_out,
        out_shape=[jax.ShapeDtypeStruct(out_shape, F32),
                   jax.ShapeDtypeStruct((ns, D_MODEL), F32)] + r_shapes,
        compiler_params=_params(1),
        name="merge",
    )(x2d, a, b, c, xs, a_s, b_s, c_s, g2, g3, wgate, *wbr, wout, *[r['src'] for r in riders])


def _sample_proj_kernel(layer, x_ref, g_ref, whg_ref, wgm_ref, logits_ref, lng_ref, lnb_ref,
                        w00_ref, b00_ref, q_ref, f_ref, kt_ref, v_ref, xg_ref, bout_ref, vrow_ref):
    x = x_ref[...]
    h = _rms(x, g_ref[...]).astype(BF16)
    p = _dot(h, whg_ref[...])
    z = p[:, 1 * HG_WIDTH:2 * HG_WIDTH]
    logf, kk = _hgrn_gates(z, _lower_bound(logits_ref, layer))
    q_ref[...] = p[:, 0 * HG_WIDTH:1 * HG_WIDTH]
    f_ref[...] = jnp.exp(logf)
    for hd in range(HG_HEADS):
        kt_ref[hd] = kk[:, hd * HG_DK:(hd + 1) * HG_DK].T.astype(BF16)
    v_ref[...] = p[:, 2 * HG_WIDTH:3 * HG_WIDTH]
    xg_ref[...] = p[:, 3 * HG_WIDTH:4 * HG_WIDTH]
    u, v = _gmlp_uv(x, g_ref[...], wgm_ref, lng_ref[...], lnb_ref[...])
    bout_ref[...] = (u * (v * w00_ref[...] + b00_ref[...])).astype(BF16)
    vrow_ref[...] = v


def _sample_proj_call(x, g, w_hg, w_gm, logits, lng, lnb, w00, b00, *, layer, n):
    depth = logits.shape[0]
    assert n == HG_DK, "per-head key transpose is written for a square (sequences, keys) tile"
    f32_out = jax.ShapeDtypeStruct((n, HG_WIDTH), F32)
    row_spec = pl.BlockSpec((n, HG_WIDTH), lambda i: (0, 0))
    return pl.pallas_call(
        functools.partial(_sample_proj_kernel, layer),
        grid=(1,),
        in_specs=[
            _const_spec((n, D_MODEL)),
            _const_spec((1, D_MODEL)),
            _const_spec((D_MODEL, 4 * HG_WIDTH)),
            _const_spec((D_MODEL, 2 * GM_WIDTH)),
            _const_spec((depth, HG_WIDTH)),
            _const_spec((1, GM_WIDTH)),
            _const_spec((1, GM_WIDTH)),
            _const_spec((1, GM_WIDTH)),
            _const_spec((1, GM_WIDTH)),
        ],
        out_specs=[row_spec, row_spec, pl.BlockSpec((HG_HEADS, HG_DK, n), lambda i: (0, 0, 0)),
                   row_spec, row_spec, row_spec, row_spec],
        out_shape=[f32_out, f32_out, jax.ShapeDtypeStruct((HG_HEADS, HG_DK, n), BF16),
                   f32_out, f32_out, jax.ShapeDtypeStruct((n, GM_WIDTH), BF16), f32_out],
        compiler_params=_params(1),
        name="sample_proj",
    )(x, g, w_hg, w_gm, logits, lng, lnb, w00, b00)


def _hgrn_step_kernel(bb, layer, whole_stack, *refs):
    s_ref, fc_ref, kt_ref, q_ref, v_ref, xg_ref, ng_ref = refs[:7]
    snew_ref, a_ref, o_s = refs[-3:]
    if whole_stack:
        for j in range(snew_ref.shape[0]):
            if j != layer:
                snew_ref[j] = jnp.zeros(snew_ref.shape[1:], F32)
        snew_ref = snew_ref.at[layer]
    n = kt_ref.shape[-1]
    seq_id = lax.broadcasted_iota(jnp.int32, (n, HG_DK), 0)
    first = pl.program_id(0) * bb
    for i in range(bb):
        here = seq_id == first + i
        for hd in range(HG_HEADS):
            cs = slice(hd * HG_DK, (hd + 1) * HG_DK)
            v_only = jnp.where(here, jnp.broadcast_to(v_ref[i:i + 1, cs], (n, HG_DK)), 0.0)
            kv = _dot(kt_ref[hd], v_only.astype(BF16))
            f_col = fc_ref[0, hd, :, i:i + 1]
            s_new = f_col * s_ref[i, hd] + kv
            snew_ref[i, hd] = s_new
            q8 = jnp.broadcast_to(q_ref[i:i + 1, cs], (8, HG_DK)).astype(BF16)
            o_s[i:i + 1, cs] = _dot(q8, s_new.astype(BF16))[0:1]
    a_ref[...] = _hgrn_out(o_s[...], xg_ref[...], ng_ref[...]).astype(BF16)


def _hgrn_step_call(s_all, prev, fc, kt, q, v, xg, ng, *, layer, n, bb):
    col_spec = pl.BlockSpec((1, HG_HEADS, HG_DK, bb), lambda i: (i, 0, 0, 0))
    row_spec = pl.BlockSpec((bb, HG_WIDTH), lambda i: (i, 0))
    st_spec = pl.BlockSpec((None, bb, HG_HEADS, HG_DK, HG_DK), lambda i: (layer, i, 0, 0, 0))
    in_specs = [st_spec, col_spec, _const_spec((HG_HEADS, HG_DK, n)), row_spec, row_spec, row_spec,
                _const_spec((1, HG_WIDTH))]
    args = [s_all, fc, kt, q, v, xg, ng]
    aliases = {}
    out_st_spec = st_spec
    if prev is not None:
        in_specs.append(pl.BlockSpec(memory_space=pl.ANY))
        args.append(prev)
        aliases = {len(args) - 1: 0}
    else:
        out_st_spec = pl.BlockSpec((s_all.shape[0], bb, HG_HEADS, HG_DK, HG_DK), lambda i: (0, i, 0, 0, 0))
    return pl.pallas_call(
        functools.partial(_hgrn_step_kernel, bb, layer, prev is None),
        grid=(n // bb,),
        in_specs=in_specs,
        out_specs=[out_st_spec, row_spec],
        out_shape=[
            jax.ShapeDtypeStruct(s_all.shape, F32),
            jax.ShapeDtypeStruct((n, HG_WIDTH), BF16),
        ],
        scratch_shapes=[pltpu.VMEM((bb, HG_WIDTH), F32)],
        input_output_aliases=aliases,
        compiler_params=_params(1),
        name="hgrn_step",
    )(*args)


def _s5_params(lam_re, lam_im, log_dt, b_re, b_im, c_re, c_im):
    lr = lam_re.astype(F32)
    li = lam_im.astype(F32)
    dt = jnp.exp(log_dt.astype(F32))[:, None]
    mag = jnp.exp(lr * dt)
    a_re = mag * jnp.cos(li * dt)
    a_im = mag * jnp.sin(li * dt)
    den = lr * lr + li * li
    k_re = ((a_re - 1.0) * lr + a_im * li) / den
    k_im = (a_im * lr - (a_re - 1.0) * li) / den
    bre = b_re.astype(F32)
    bim = b_im.astype(F32)
    bb_re = k_re[..., None] * bre - k_im[..., None] * bim
    bb_im = k_re[..., None] * bim + k_im[..., None] * bre
    gpb = S5_GROUPS // S5_BLOCKS
    eye = jnp.eye(gpb, dtype=F32)

    def pack_in(w):
        w = w.reshape(S5_BLOCKS, gpb, S5_STATE, S5_GROUP)
        out = jnp.einsum('jgph,gk->jghkp', w, eye)
        return out.reshape(S5_BLOCKS, gpb * S5_GROUP, gpb * S5_STATE).astype(BF16)

    def pack_out(w):
        w = w.reshape(S5_BLOCKS, gpb, S5_GROUP, S5_STATE)
        out = jnp.einsum('jghp,gk->jgpkh', w, eye)
        return out.reshape(S5_BLOCKS, gpb * S5_STATE, gpb * S5_GROUP).astype(BF16)

    return (a_re.reshape(1, S5_N), a_im.reshape(1, S5_N), pack_in(bb_re), pack_in(bb_im),
            pack_out(c_re.astype(F32)), pack_out(c_im.astype(F32)))


def kernel(x_prompt, x_sample, state_hgrn, state_s5_re, state_s5_im, norm_g, ffn_w_gate, ffn_w_up, ffn_w_down, w_in, hgrn_lb_logits, hgrn_norm_g, gmlp_ws, gmlp_bs, gmlp_norm_g, gmlp_norm_b, s5_lam_re, s5_lam_im, s5_log_dt, s5_b_re, s5_b_im, s5_c_re, s5_c_im, s5_d, s5_glu_w, s5_glu_b, w_branch, w_out):
    nb, seq, _ = x_prompt.shape
    ns = x_sample.shape[0]
    depth = norm_g.shape[0]
    tm = 1024
    tl = 128
    tg = 512
    tt = 128
    bb = 8
    nt = seq // tm
    n_p = nb * nt

    def bmajor(i):
        return (jnp.minimum(i, n_p - 1), 0)

    def tmajor(i):
        j = jnp.minimum(i, n_p - 1)
        return (j % nt, j // nt)

    def ffn_riders(l, which):
        return [_rider(ffn_w_gate, (l, which)), _rider(ffn_w_up, (l, which)),
                _rider(ffn_w_down, (l, which))]

    def mixer_riders(l):
        in_splits = ((0, COL_GM), (COL_GM, COL_S5), (COL_S5, COL_GATE), (COL_GATE, N_IN))
        return ([_rider(w_in, (l,), in_splits)]
                + [_rider(w_branch, (l, n)) for n in range(N_BRANCH)]
                + [_rider(w_out, (l,)), _rider(s5_glu_w, (l,))])

    yp = x_prompt.reshape(nb * seq, D_MODEL)
    ys = x_sample.reshape(ns, D_MODEL)
    logits = hgrn_lb_logits.astype(F32)
    tril = jnp.tril(jnp.ones((GM_CHUNK, GM_CHUNK), F32))
    ffn_w = _cast_call(ffn_riders(0, 0), n_p)

    outs = {k: [] for k in ('hg_p', 're_p', 'im_p', 're_s', 'im_s', 'v_s')}
    hg_s = None
    for l in range(depth):
        g = [norm_g[l, i].reshape(1, D_MODEL) for i in range(6)]
        ng = hgrn_norm_g[l].reshape(1, HG_WIDTH)
        lng = gmlp_norm_g[l].reshape(1, GM_WIDTH)
        lnb = gmlp_norm_b[l].reshape(1, GM_WIDTH)
        wm = (gmlp_ws[l] * tril).astype(BF16)
        bst = gmlp_bs[l].T
        gc = GM_WIDTH // GM_GROUPS
        w00 = jnp.repeat(gmlp_ws[l][:, 0, 0], gc).reshape(1, GM_WIDTH)
        b00 = jnp.repeat(gmlp_bs[l][:, 0], gc).reshape(1, GM_WIDTH)
        s5p = _s5_params(s5_lam_re[l], s5_lam_im[l], s5_log_dt[l], s5_b_re[l], s5_b_im[l],
                         s5_c_re[l], s5_c_im[l])
        s5d = s5_d[l].reshape(1, S5_WIDTH)
        gb = s5_glu_b[l].reshape(1, 2 * S5_WIDTH)
        tm_shape = (seq, nb * D_MODEL)

        x1, xs1, w_hg, w_gm, w_s5, w_gate, wb0, wb1, wb2, wout, gw = _ffn_call(
            yp, ys, g[0], g[1], *ffn_w, mixer_riders(l), n_p=n_p, tm=tm,
            in_map=bmajor if l == 0 else tmajor, out_map=tmajor, out_shape=tm_shape)
        a_p, s_hg = _hgrn_prompt_call(x1, g[2], w_hg, logits, ng, layer=l, nb=nb, seq=seq, tl=tl)
        b_p = _gmlp_prompt_call(x1, g[2], w_gm, lng, lnb, wm, bst, nb=nb, seq=seq, tg=tg)
        c_p, h_re, h_im = _s5_call(x1, g[2], w_s5, *s5p, s5d, gw, gb, nb=nb, seq=seq, tt=tt)
        outs['hg_p'].append(s_hg)
        outs['re_p'].append(h_re.reshape(nb, S5_GROUPS, S5_STATE))
        outs['im_p'].append(h_im.reshape(nb, S5_GROUPS, S5_STATE))

        q, f, kt, v, xg, b_s, v_rows = _sample_proj_call(
            xs1, g[2], w_hg, w_gm, logits, lng, lnb, w00, b00, layer=l, n=ns)
        fc = f.reshape(ns // bb, bb, HG_HEADS, HG_DK).transpose(0, 2, 3, 1)
        hg_s, a_s = _hgrn_step_call(state_hgrn, hg_s, fc, kt, q, v, xg, ng, layer=l, n=ns, bb=bb)
        h0 = (state_s5_re[l].reshape(ns, S5_N), state_s5_im[l].reshape(ns, S5_N))
        c_s, hs_re, hs_im = _s5_call(xs1, g[2], w_s5, *s5p, s5d, gw, gb, h0, nb=ns, seq=1, tt=1)

        x2, xs2, *ffn_w = _merge_call(x1, a_p, b_p, c_p, xs1, a_s, b_s, c_s, g[2], g[3],
                                      w_gate, (wb0, wb1, wb2), wout, ffn_riders(l, 1),
                                      n_p=n_p, tm=tm, xmap=tmajor, out_shape=tm_shape)
        last = l == depth - 1
        yp, ys, *ffn_next = _ffn_call(x2, xs2, g[4], g[5], *ffn_w,
                                      [] if last else ffn_riders(l + 1, 0), n_p=n_p, tm=tm,
                                      in_map=tmajor, out_map=bmajor if last else tmajor,
                                      out_shape=(nb * seq, D_MODEL) if last else tm_shape)
        ffn_w = ffn_next
        outs['re_s'].append(hs_re.reshape(ns, S5_GROUPS, S5_STATE))
        outs['im_s'].append(hs_im.reshape(ns, S5_GROUPS, S5_STATE))
        outs['v_s'].append(v_rows.reshape(ns, 1, GM_WIDTH))

    return (yp.reshape(nb, seq, D_MODEL), ys.reshape(ns, 1, D_MODEL),
            jnp.stack(outs['hg_p']), jnp.stack(outs['re_p']), jnp.stack(outs['im_p']),
            hg_s, jnp.stack(outs['re_s']), jnp.stack(outs['im_s']),
            jnp.stack(outs['v_s']))
```

```python
import functools
import math

import jax
import jax.numpy as jnp
from jax import lax
from jax.experimental import pallas as pl
from jax.experimental.pallas import tpu as pltpu

F32 = jnp.float32
BF16 = jnp.bfloat16

D_MODEL = 1024
D_FF = 2816
HG_HEADS = 4
HG_DK = 128
HG_WIDTH = HG_HEADS * HG_DK
GM_GROUPS = 4
GM_CHUNK = 128
GM_WIDTH = 512
S5_WIDTH = 512
S5_GROUP = 16
S5_GROUPS = S5_WIDTH // S5_GROUP
S5_STATE = 64
S5_N = S5_GROUPS * S5_STATE
S5_BLOCKS = 4
N_BRANCH = 3
EPS = 1e-6

COL_GM = 4 * HG_WIDTH
COL_S5 = COL_GM + 2 * GM_WIDTH
COL_GATE = COL_S5 + S5_WIDTH
N_IN = COL_GATE + N_BRANCH * D_MODEL

GLA_CHUNK = 32
FF_CHUNK = 512
VMEM_LIMIT = 56 * 1024 * 1024


def _rms(x, g):
    return x * lax.rsqrt(jnp.mean(x * x, axis=-1, keepdims=True) + EPS) * g


def _gelu(x):
    c = math.sqrt(2.0 / math.pi)
    hx = 0.5 * x
    return hx + hx * jnp.tanh(x * (c + (c * 0.044715) * (x * x)))


def _sigmoid(x):
    return 0.5 + 0.5 * jnp.tanh(0.5 * x)


def _dot(a, b):
    return jnp.dot(a, b, preferred_element_type=F32)


def _dot_nt(a, b):
    return lax.dot_general(a, b, (((1,), (1,)), ((), ())), preferred_element_type=F32)


def _dot_tn(a, b):
    return lax.dot_general(a, b, (((0,), (0,)), ((), ())), preferred_element_type=F32)


def _const_spec(shape):
    n = len(shape)
    return pl.BlockSpec(shape, lambda *_: (0,) * n, pipeline_mode=pl.Buffered(1))


class _Row:
    def __init__(self, arr, index):
        self.arr, self.index = arr, index

    @property
    def spec(self):
        tail = self.arr.shape[1:]
        index = self.index
        return pl.BlockSpec((None,) + tail, lambda *_: (index,) + (0,) * len(tail),
                            pipeline_mode=pl.Buffered(1))


def _params(n_grid):
    return pltpu.CompilerParams(
        dimension_semantics=("arbitrary",) * n_grid, vmem_limit_bytes=VMEM_LIMIT)


def _rider(src, lead=(), splits=None):
    k, n = src.shape[-2:]
    return dict(src=src, lead=tuple(lead), k=k, n=n, splits=tuple(splits or ((0, n),)))


def _rider_specs(riders, n_chunks):
    in_specs, out_specs, out_shapes = [], [], []
    for r in riders:
        rows = r['k'] // n_chunks
        assert rows * n_chunks == r['k'] and rows % 16 == 0, (r['k'], n_chunks)

        def src_map(i, lead=r['lead']):
            return lead + (jnp.minimum(i, n_chunks - 1), 0)

        in_specs.append(pl.BlockSpec((None,) * len(r['lead']) + (rows, r['n']), src_map))
        for c0, c1 in r['splits']:
            out_specs.append(pl.BlockSpec((rows, c1 - c0), lambda i: (jnp.minimum(i, n_chunks - 1), 0)))
            out_shapes.append(jax.ShapeDtypeStruct((r['k'], c1 - c0), BF16))
    return in_specs, out_specs, out_shapes


def _run_riders(src_refs, out_refs, splits):
    k = 0
    for src, sp in zip(src_refs, splits):
        for c0, c1 in sp:
            out_refs[k][...] = src[:, c0:c1].astype(BF16)
            k += 1


def _cast_kernel(splits, *refs):
    n = len(splits)
    _run_riders(refs[:n], refs[n:], splits)


def _cast_call(riders, n_chunks):
    in_specs, out_specs, out_shapes = _rider_specs(riders, n_chunks)
    return pl.pallas_call(
        functools.partial(_cast_kernel, tuple(r['splits'] for r in riders)),
        grid=(n_chunks,),
        in_specs=in_specs,
        out_specs=out_specs,
        out_shape=out_shapes,
        compiler_params=_params(1),
        name="cast",
    )(*[r['src'] for r in riders])


def _ffn_body(x, g_pre, g_post, wg_ref, wu_ref, wd_ref):
    h = _rms(x, g_pre).astype(BF16)
    d = None
    for c0 in range(0, D_FF, FF_CHUNK):
        c1 = min(c0 + FF_CHUNK, D_FF)
        a = _dot(h, wg_ref[:, c0:c1])
        u = _dot(h, wu_ref[:, c0:c1])
        m = (a * _sigmoid(a) * u).astype(BF16)
        part = _dot(m, wd_ref[c0:c1, :])
        d = part if d is None else d + part
    return x + 0.5 * _rms(d, g_post)


def _ffn_kernel(n_p, emit_h, splits, *refs):
    x_ref, xs_ref, gpre_ref, gpost_ref, gnext_ref, wg, wu, wd = refs[:8]
    n_r = len(splits)
    rider_src = refs[8:8 + n_r]
    n_out = 4 if emit_h else 2
    outs = refs[8 + n_r:8 + n_r + n_out]
    rider_out = refs[8 + n_r + n_out:]
    i = pl.program_id(0)
    _run_riders(rider_src, rider_out, splits)

    def run(src_ref, dst_ref, h_ref):
        y = _ffn_body(src_ref[...], gpre_ref[...], gpost_ref[...], wg, wu, wd)
        dst_ref[...] = y
        if emit_h:
            h_ref[...] = _rms(y, gnext_ref[...]).astype(BF16)

    @pl.when(i < n_p)
    def _():
        run(x_ref, outs[0], outs[2] if emit_h else None)

    @pl.when(i == n_p)
    def _():
        run(xs_ref, outs[1], outs[3] if emit_h else None)


def _ffn_call(x2d, xs, g_pre, g_post, g_next, wg, wu, wd, riders, *, emit_h, n_p, tm, in_map,
              out_map, out_shape):
    ns = xs.shape[0]
    r_in, r_out, r_shapes = _rider_specs(riders, n_p)
    out_specs = [pl.BlockSpec((tm, D_MODEL), out_map), pl.BlockSpec((ns, D_MODEL), lambda i: (0, 0))]
    out_shapes = [jax.ShapeDtypeStruct(out_shape, F32), jax.ShapeDtypeStruct((ns, D_MODEL), F32)]
    if emit_h:
        out_specs = out_specs + out_specs
        out_shapes += [jax.ShapeDtypeStruct(out_shape, BF16), jax.ShapeDtypeStruct((ns, D_MODEL), BF16)]
    return pl.pallas_call(
        functools.partial(_ffn_kernel, n_p, emit_h, tuple(r['splits'] for r in riders)),
        grid=(n_p + 1,),
        in_specs=[
            pl.BlockSpec((tm, D_MODEL), in_map),
            _const_spec((ns, D_MODEL)),
            g_pre.spec, g_post.spec, g_next.spec,
            _const_spec((D_MODEL, D_FF)),
            _const_spec((D_MODEL, D_FF)),
            _const_spec((D_FF, D_MODEL)),
        ] + r_in,
        out_specs=out_specs + r_out,
        out_shape=out_shapes + r_shapes,
        compiler_params=_params(1),
        name="ffn",
    )(x2d, xs, g_pre.arr, g_post.arr, g_next.arr, wg, wu, wd, *[r['src'] for r in riders])


def _hgrn_gates(z, lb):
    ls = jnp.minimum(z, 0.0) - jnp.log1p(jnp.exp(-jnp.abs(z)))
    c = jnp.log1p(-lb) + ls
    a = jnp.log(jnp.maximum(lb, 1e-37))
    lae = jnp.maximum(a, c) + jnp.log1p(jnp.exp(-jnp.abs(a - c)))
    logf = jnp.where(lb > 0.0, lae, c)
    k = (1.0 - lb) * _sigmoid(-z)
    return logf, k


def _hgrn_out(o, xg, ng):
    parts = []
    for hd in range(HG_HEADS):
        oh = o[:, hd * HG_DK:(hd + 1) * HG_DK]
        parts.append(oh * lax.rsqrt(jnp.mean(oh * oh, axis=-1, keepdims=True) + EPS))
    on = jnp.concatenate(parts, axis=-1)
    return on * ng * (xg * _sigmoid(xg))


def _lower_bound(logits_ref, layer):
    lg = logits_ref[...]
    m = jnp.max(lg, axis=0, keepdims=True)
    e = jnp.exp(lg - m)
    den = jnp.sum(e, axis=0, keepdims=True)
    lb = jnp.zeros_like(den)
    for j in range(1, layer + 1):
        lb = lb + e[j:j + 1] / den
    return lb


def _hgrn_prompt_kernel(layer, nb, tl, h_ref, w_ref, logits_ref, ng_ref, tri_ref,
                        a_ref, sfin_ref, st_ref, q_s, k_s, v_s, b_s, o_s):
    t = pl.program_id(0)
    nt = pl.num_programs(0)

    @pl.when(t == 0)
    def _():
        st_ref[...] = jnp.zeros_like(st_ref)

    h = jnp.concatenate([h_ref[:, b * D_MODEL:(b + 1) * D_MODEL] for b in range(nb)], axis=0)
    p = _dot(h, w_ref[...])
    xq = p[:, 0 * HG_WIDTH:1 * HG_WIDTH]
    z = p[:, 1 * HG_WIDTH:2 * HG_WIDTH]
    xi = p[:, 2 * HG_WIDTH:3 * HG_WIDTH]
    xg = p[:, 3 * HG_WIDTH:4 * HG_WIDTH]
    lb = _lower_bound(logits_ref, layer)
    logf, kk = _hgrn_gates(z, lb)

    tri = tri_ref[...]
    tr = tri.shape[0]
    l0 = logf.astype(BF16)
    r1 = logf - l0.astype(F32)
    l1 = r1.astype(BF16)
    l2 = (r1 - l1.astype(F32)).astype(BF16)
    for i in range(nb * tl // tr):
        rs = slice(i * tr, (i + 1) * tr)
        b_s[rs, :] = _dot(tri, l0[rs]) + _dot(tri, l1[rs]) + _dot(tri, l2[rs])

    q_s[...] = xq
    k_s[...] = kk
    v_s[...] = xi

    c = GLA_CHUNK
    mid = c // 2 - 1
    row = lax.broadcasted_iota(jnp.int32, (c, c), 0)
    col = lax.broadcasted_iota(jnp.int32, (c, c), 1)
    causal = col <= row

    def chunk(ci, carry):
        pending = []
        for b in range(nb):
            r = pl.ds(pl.multiple_of(b * tl + ci * c, c), c)
            for hd in range(HG_HEADS):
                cs = slice(hd * HG_DK, (hd + 1) * HG_DK)
                bq = b_s[r, cs]
                qc = q_s[r, cs]
                kc = k_s[r, cs]
                vc = v_s[r, cs].astype(BF16)
                b_mid = bq[mid:mid + 1]
                b_last = bq[c - 1:c]
                st = st_ref[b, hd]
                q_in = (qc * jnp.exp(bq)).astype(BF16)
                q_md = (qc * jnp.exp(bq - b_mid)).astype(BF16)
                k_md = (kc * jnp.exp(b_mid - bq)).astype(BF16)
                k_ls = (kc * jnp.exp(b_last - bq)).astype(BF16)
                scores = _dot_nt(q_md, k_md)
                o_inter = _dot_nt(q_in, st.astype(BF16))
                st_ref[b, hd] = st * jnp.exp(b_last) + _dot_tn(vc, k_ls)
                pending.append((r, cs, scores, o_inter, vc))
        for r, cs, scores, o_inter, vc in pending:
            sc = jnp.where(causal, scores, 0.0).astype(BF16)
            o_s[r, cs] = o_inter + _dot(sc, vc)
        return carry

    lax.fori_loop(0, tl // c, chunk, 0)

    a = _hgrn_out(o_s[...], xg, ng_ref[...]).astype(BF16)
    for b in range(nb):
        a_ref[:, b * HG_WIDTH:(b + 1) * HG_WIDTH] = a[b * tl:(b + 1) * tl]

    @pl.when(t == nt - 1)
    def _():
        for b in range(nb):
            for hd in range(HG_HEADS):
                sfin_ref[b, hd] = st_ref[b, hd].T


def _hgrn_prompt_call(h_tm, w_hg, logits, ng, *, layer, nb, seq, tl):
    c = GLA_CHUNK
    tr = min(256, nb * tl)
    idx = jnp.arange(tr)
    tri = ((idx[:, None] >= idx[None, :]) & (idx[:, None] // c == idx[None, :] // c)).astype(BF16)
    depth = logits.shape[0]
    rows = nb * tl
    return pl.pallas_call(
        functools.partial(_hgrn_prompt_kernel, layer, nb, tl),
        grid=(seq // tl,),
        in_specs=[
            pl.BlockSpec((tl, nb * D_MODEL), lambda t: (t, 0)),
            _const_spec((D_MODEL, 4 * HG_WIDTH)),
            _const_spec((depth, HG_WIDTH)),
            ng.spec,
            _const_spec((tr, tr)),
        ],
        out_specs=[
            pl.BlockSpec((tl, nb * HG_WIDTH), lambda t: (t, 0)),
            pl.BlockSpec((nb, HG_HEADS, HG_DK, HG_DK), lambda t: (0, 0, 0, 0)),
        ],
        out_shape=[
            jax.ShapeDtypeStruct((seq, nb * HG_WIDTH), BF16),
            jax.ShapeDtypeStruct((nb, HG_HEADS, HG_DK, HG_DK), F32),
        ],
        scratch_shapes=[
            pltpu.VMEM((nb, HG_HEADS, HG_DK, HG_DK), F32),
            pltpu.VMEM((rows, HG_WIDTH), F32),
            pltpu.VMEM((rows, HG_WIDTH), F32),
            pltpu.VMEM((rows, HG_WIDTH), F32),
            pltpu.VMEM((rows, HG_WIDTH), F32),
            pltpu.VMEM((rows, HG_WIDTH), F32),
        ],
        compiler_params=_params(1),
        name="hgrn_prompt",
    )(h_tm, w_hg, logits, ng.arr, tri)


def _gmlp_uv(h, w_ref, lng, lnb):
    p = _dot(h, w_ref[...])
    u = _gelu(p[:, :GM_WIDTH])
    gv = _gelu(p[:, GM_WIDTH:])
    mu = jnp.mean(gv, axis=-1, keepdims=True)
    dv = gv - mu
    var = jnp.mean(dv * dv, axis=-1, keepdims=True)
    v = dv * lax.rsqrt(var + EPS) * lng + lnb
    return u, v


def _gmlp_prompt_kernel(tg, h_ref, w_ref, lng_ref, lnb_ref, wm_ref, bst_ref, o_ref):
    u, v = _gmlp_uv(h_ref[...], w_ref, lng_ref[...], lnb_ref[...])
    vb = v.astype(BF16)
    gc = GM_WIDTH // GM_GROUPS
    rows = []
    for j in range(tg // GM_CHUNK):
        parts = []
        for gi in range(GM_GROUPS):
            vj = vb[j * GM_CHUNK:(j + 1) * GM_CHUNK, gi * gc:(gi + 1) * gc]
            parts.append(_dot(wm_ref[gi], vj) + bst_ref[:, gi:gi + 1])
        rows.append(jnp.concatenate(parts, axis=-1))
    mix = jnp.concatenate(rows, axis=0)
    o_ref[...] = (u * mix).astype(BF16)


def _gmlp_prompt_call(h_tm, w_gm, lng, lnb, wm, bst, *, nb, seq, tg):
    return pl.pallas_call(
        functools.partial(_gmlp_prompt_kernel, tg),
        grid=(nb, seq // tg),
        in_specs=[
            pl.BlockSpec((tg, D_MODEL), lambda b, t: (t, b)),
            _const_spec((D_MODEL, 2 * GM_WIDTH)),
            lng.spec, lnb.spec, wm.spec, bst.spec,
        ],
        out_specs=pl.BlockSpec((tg, GM_WIDTH), lambda b, t: (t, b)),
        out_shape=jax.ShapeDtypeStruct((seq, nb * GM_WIDTH), BF16),
        compiler_params=_params(2),
        name="gmlp_prompt",
    )(h_tm, w_gm, lng.arr, lnb.arr, wm.arr, bst.arr)


def _s5_kernel(nb, tt, has_h0, relayout, *refs):
    (x_ref, w_ref, are_ref, aim_ref, bre_ref, bim_ref, cre_ref, cim_ref, d_ref,
     gw_ref, gb_ref) = refs[:11]
    h0re_ref, h0im_ref = refs[11:13] if has_h0 else (None, None)
    c_ref, hre_ref, him_ref, sre, sim, lay = refs[-6:]
    t = pl.program_id(0)
    wblk = S5_WIDTH // S5_BLOCKS
    sblk = S5_N // S5_BLOCKS

    @pl.when(t == 0)
    def _():
        if has_h0:
            hre_ref[...] = h0re_ref[...]
            him_ref[...] = h0im_ref[...]
        else:
            hre_ref[...] = jnp.zeros_like(hre_ref)
            him_ref[...] = jnp.zeros_like(him_ref)

    if relayout:
        h = jnp.concatenate([x_ref[:, b * D_MODEL:(b + 1) * D_MODEL] for b in range(nb)], axis=0)
    else:
        h = x_ref[...]
    su = _dot(h, w_ref[...])
    if relayout:
        for b in range(nb):
            for j in range(S5_BLOCKS):
                lay[j, pl.ds(b, tt, stride=nb), :] = su[b * tt:(b + 1) * tt, j * wblk:(j + 1) * wblk]
        su = jnp.concatenate([lay[j] for j in range(S5_BLOCKS)], axis=-1)
    sub = su.astype(BF16)

    def project(j):
        uj = sub[:, j * wblk:(j + 1) * wblk]
        sre[:, j * sblk:(j + 1) * sblk] = _dot(uj, bre_ref[j])
        sim[:, j * sblk:(j + 1) * sblk] = _dot(uj, bim_ref[j])

    def scan(j):
        cols = slice(j * sblk, (j + 1) * sblk)
        a_re = jnp.broadcast_to(are_ref[:, cols], (nb, sblk))
        a_im = jnp.broadcast_to(aim_ref[:, cols], (nb, sblk))
        h_re = hre_ref[:, cols]
        h_im = him_ref[:, cols]
        for i in range(tt):
            r = slice(i * nb, (i + 1) * nb)
            n_re = a_re * h_re - a_im * h_im + sre[r, cols]
            n_im = a_re * h_im + a_im * h_re + sim[r, cols]
            sre[r, cols] = n_re
            sim[r, cols] = n_im
            h_re, h_im = n_re, n_im
        hre_ref[:, cols] = h_re
        him_ref[:, cols] = h_im

    def readout(j):
        hr = sre[:, j * sblk:(j + 1) * sblk].astype(BF16)
        hi = sim[:, j * sblk:(j + 1) * sblk].astype(BF16)
        return _dot(hr, cre_ref[j]) - _dot(hi, cim_ref[j])

    ys = []
    project(0)
    for j in range(S5_BLOCKS):
        if j + 1 < S5_BLOCKS:
            project(j + 1)
        scan(j)
        ys.append(readout(j))
    y = jnp.concatenate(ys, axis=-1) + d_ref[...] * su
    zz = _dot(_gelu(y).astype(BF16), gw_ref[...]) + gb_ref[...]
    c = zz[:, :S5_WIDTH] * _sigmoid(zz[:, S5_WIDTH:])
    if relayout:
        for j in range(S5_BLOCKS):
            lay[j] = c[:, j * wblk:(j + 1) * wblk]
        for b in range(nb):
            cb = jnp.concatenate([lay[j, pl.ds(b, tt, stride=nb), :] for j in range(S5_BLOCKS)], axis=-1)
            c_ref[:, b * S5_WIDTH:(b + 1) * S5_WIDTH] = cb.astype(BF16)
    else:
        c_ref[...] = c.astype(BF16)


def _s5_call(h2d, w_s5, s5p, d, gw, gb, h0=None, *, nb, seq, tt):
    rows = tt * nb
    relayout = seq > 1
    wblk = S5_WIDTH // S5_BLOCKS
    sblk = S5_N // S5_BLOCKS
    x_block = (tt, nb * D_MODEL) if relayout else (rows, D_MODEL)
    c_block = (tt, nb * S5_WIDTH) if relayout else (rows, S5_WIDTH)
    in_specs = [
        pl.BlockSpec(x_block, lambda t: (t, 0)),
        _const_spec((D_MODEL, S5_WIDTH)),
    ] + [p.spec for p in s5p] + [
        d.spec,
        _const_spec((S5_WIDTH, 2 * S5_WIDTH)),
        gb.spec,
    ]
    args = [h2d, w_s5] + [p.arr for p in s5p] + [d.arr, gw, gb.arr]
    if h0 is not None:
        in_specs += [_const_spec((nb, S5_N)), _const_spec((nb, S5_N))]
        args += list(h0)
    state_spec = pl.BlockSpec((nb, S5_N), lambda t: (0, 0))
    c_shape = (seq, nb * S5_WIDTH) if relayout else (nb, S5_WIDTH)
    return pl.pallas_call(
        functools.partial(_s5_kernel, nb, tt, h0 is not None, relayout),
        grid=(seq // tt,),
        in_specs=in_specs,
        out_specs=[pl.BlockSpec(c_block, lambda t: (t, 0)), state_spec, state_spec],
        out_shape=[
            jax.ShapeDtypeStruct(c_shape, BF16),
            jax.ShapeDtypeStruct((nb, S5_N), F32),
            jax.ShapeDtypeStruct((nb, S5_N), F32),
        ],
        scratch_shapes=[pltpu.VMEM((rows, S5_N), F32), pltpu.VMEM((rows, S5_N), F32),
                        pltpu.VMEM((S5_BLOCKS, rows, wblk), F32)],
        compiler_params=_params(1),
        name="s5",
    )(*args)


def _merge_body(x, h, branches, g3, wgate, wbr, wout):
    merged = None
    for n, br in enumerate(branches):
        gate = _sigmoid(_dot(h, wgate[:, n * D_MODEL:(n + 1) * D_MODEL]))
        term = gate * _dot(br, wbr[n][...])
        merged = term if merged is None else merged + term
    return x + _rms(_dot(merged.astype(BF16), wout[...]), g3)


def _merge_kernel(n_p, splits, *refs):
    (x_ref, h_ref, a_ref, b_ref, c_ref, xs_ref, hs_ref, as_ref, bs_ref, cs_ref, g3_ref,
     wgate, wb0, wb1, wb2, wout) = refs[:16]
    n_r = len(splits)
    rider_src = refs[16:16 + n_r]
    o_ref, os_ref = refs[16 + n_r:18 + n_r]
    rider_out = refs[18 + n_r:]
    wbr = (wb0, wb1, wb2)
    i = pl.program_id(0)
    _run_riders(rider_src, rider_out, splits)

    @pl.when(i < n_p)
    def _():
        o_ref[...] = _merge_body(x_ref[...], h_ref[...], (a_ref[...], b_ref[...], c_ref[...]),
                                 g3_ref[...], wgate, wbr, wout)

    @pl.when(i == n_p)
    def _():
        os_ref[...] = _merge_body(xs_ref[...], hs_ref[...], (as_ref[...], bs_ref[...], cs_ref[...]),
                                  g3_ref[...], wgate, wbr, wout)


def _merge_call(x2d, h2d, a, b, c, xs, hs, a_s, b_s, c_s, g3, wgate, wbr, wout, riders,
                *, n_p, tm, xmap, out_shape):
    ns = xs.shape[0]
    r_in, r_out, r_shapes = _rider_specs(riders, n_p)
    return pl.pallas_call(
        functools.partial(_merge_kernel, n_p, tuple(r['splits'] for r in riders)),
        grid=(n_p + 1,),
        in_specs=[
            pl.BlockSpec((tm, D_MODEL), xmap),
            pl.BlockSpec((tm, D_MODEL), xmap),
            pl.BlockSpec((tm, HG_WIDTH), xmap),
            pl.BlockSpec((tm, GM_WIDTH), xmap),
            pl.BlockSpec((tm, S5_WIDTH), xmap),
            _const_spec((ns, D_MODEL)),
            _const_spec((ns, D_MODEL)),
            _const_spec((ns, HG_WIDTH)),
            _const_spec((ns, GM_WIDTH)),
            _const_spec((ns, S5_WIDTH)),
            g3.spec,
            _const_spec((D_MODEL, N_BRANCH * D_MODEL)),
            _const_spec((HG_WIDTH, D_MODEL)),
            _const_spec((GM_WIDTH, D_MODEL)),
            _const_spec((S5_WIDTH, D_MODEL)),
            _const_spec((D_MODEL, D_MODEL)),
        ] + r_in,
        out_specs=[pl.BlockSpec((tm, D_MODEL), xmap),
                   pl.BlockSpec((ns, D_MODEL), lambda i: (0, 0))] + r_out,
        out_shape=[jax.ShapeDtypeStruct(out_shape, F32),
                   jax.ShapeDtypeStruct((ns, D_MODEL), F32)] + r_shapes,
        compiler_params=_params(1),
        name="merge",
    )(x2d, h2d, a, b, c, xs, hs, a_s, b_s, c_s, g3.arr, wgate, *wbr, wout, *[r['src'] for r in riders])


def _sample_proj_kernel(layer, h_ref, whg_ref, wgm_ref, logits_ref, lng_ref, lnb_ref,
                        w00_ref, b00_ref, q_ref, f_ref, kt_ref, v_ref, xg_ref, bout_ref, vrow_ref):
    h = h_ref[...]
    p = _dot(h, whg_ref[...])
    z = p[:, 1 * HG_WIDTH:2 * HG_WIDTH]
    logf, kk = _hgrn_gates(z, _lower_bound(logits_ref, layer))
    q_ref[...] = p[:, 0 * HG_WIDTH:1 * HG_WIDTH]
    f_ref[...] = jnp.exp(logf)
    for hd in range(HG_HEADS):
        kt_ref[hd] = kk[:, hd * HG_DK:(hd + 1) * HG_DK].T.astype(BF16)
    v_ref[...] = p[:, 2 * HG_WIDTH:3 * HG_WIDTH]
    xg_ref[...] = p[:, 3 * HG_WIDTH:4 * HG_WIDTH]
    u, v = _gmlp_uv(h, wgm_ref, lng_ref[...], lnb_ref[...])
    bout_ref[...] = (u * (v * w00_ref[...] + b00_ref[...])).astype(BF16)
    vrow_ref[...] = v


def _sample_proj_call(h, w_hg, w_gm, logits, lng, lnb, w00, b00, *, layer, n):
    depth = logits.shape[0]
    assert n == HG_DK, "per-head key transpose is written for a square (sequences, keys) tile"
    f32_out = jax.ShapeDtypeStruct((n, HG_WIDTH), F32)
    row_spec = pl.BlockSpec((n, HG_WIDTH), lambda i: (0, 0))
    return pl.pallas_call(
        functools.partial(_sample_proj_kernel, layer),
        grid=(1,),
        in_specs=[
            _const_spec((n, D_MODEL)),
            _const_spec((D_MODEL, 4 * HG_WIDTH)),
            _const_spec((D_MODEL, 2 * GM_WIDTH)),
            _const_spec((depth, HG_WIDTH)),
            lng.spec, lnb.spec, w00.spec, b00.spec,
        ],
        out_specs=[row_spec, row_spec, pl.BlockSpec((HG_HEADS, HG_DK, n), lambda i: (0, 0, 0)),
                   row_spec, row_spec, row_spec, row_spec],
        out_shape=[f32_out, f32_out, jax.ShapeDtypeStruct((HG_HEADS, HG_DK, n), BF16),
                   f32_out, f32_out, jax.ShapeDtypeStruct((n, GM_WIDTH), BF16), f32_out],
        compiler_params=_params(1),
        name="sample_proj",
    )(h, w_hg, w_gm, logits, lng.arr, lnb.arr, w00.arr, b00.arr)


def _hgrn_step_kernel(bb, layer, whole_stack, *refs):
    s_ref, fc_ref, kt_ref, q_ref, v_ref, xg_ref, ng_ref = refs[:7]
    snew_ref, a_ref, o_s = refs[-3:]
    if whole_stack:
        for j in range(snew_ref.shape[0]):
            if j != layer:
                snew_ref[j] = jnp.zeros(snew_ref.shape[1:], F32)
        snew_ref = snew_ref.at[layer]
    n = kt_ref.shape[-1]
    seq_id = lax.broadcasted_iota(jnp.int32, (n, HG_DK), 0)
    first = pl.program_id(0) * bb
    for i in range(bb):
        here = seq_id == first + i
        for hd in range(HG_HEADS):
            cs = slice(hd * HG_DK, (hd + 1) * HG_DK)
            v_only = jnp.where(here, jnp.broadcast_to(v_ref[i:i + 1, cs], (n, HG_DK)), 0.0)
            kv = _dot(kt_ref[hd], v_only.astype(BF16))
            f_col = fc_ref[0, hd, :, i:i + 1]
            s_new = f_col * s_ref[i, hd] + kv
            snew_ref[i, hd] = s_new
            q8 = jnp.broadcast_to(q_ref[i:i + 1, cs], (8, HG_DK)).astype(BF16)
            o_s[i:i + 1, cs] = _dot(q8, s_new.astype(BF16))[0:1]
    a_ref[...] = _hgrn_out(o_s[...], xg_ref[...], ng_ref[...]).astype(BF16)


def _hgrn_step_call(s_all, prev, fc, kt, q, v, xg, ng, *, layer, n, bb):
    col_spec = pl.BlockSpec((1, HG_HEADS, HG_DK, bb), lambda i: (i, 0, 0, 0))
    row_spec = pl.BlockSpec((bb, HG_WIDTH), lambda i: (i, 0))
    st_spec = pl.BlockSpec((None, bb, HG_HEADS, HG_DK, HG_DK), lambda i: (layer, i, 0, 0, 0))
    in_specs = [st_spec, col_spec, _const_spec((HG_HEADS, HG_DK, n)), row_spec, row_spec, row_spec,
                ng.spec]
    args = [s_all, fc, kt, q, v, xg, ng.arr]
    aliases = {}
    out_st_spec = st_spec
    if prev is not None:
        in_specs.append(pl.BlockSpec(memory_space=pl.ANY))
        args.append(prev)
        aliases = {len(args) - 1: 0}
    else:
        out_st_spec = pl.BlockSpec((s_all.shape[0], bb, HG_HEADS, HG_DK, HG_DK), lambda i: (0, i, 0, 0, 0))
    return pl.pallas_call(
        functools.partial(_hgrn_step_kernel, bb, layer, prev is None),
        grid=(n // bb,),
        in_specs=in_specs,
        out_specs=[out_st_spec, row_spec],
        out_shape=[
            jax.ShapeDtypeStruct(s_all.shape, F32),
            jax.ShapeDtypeStruct((n, HG_WIDTH), BF16),
        ],
        scratch_shapes=[pltpu.VMEM((bb, HG_WIDTH), F32)],
        input_output_aliases=aliases,
        compiler_params=_params(1),
        name="hgrn_step",
    )(*args)


def _s5_params(lam_re, lam_im, log_dt, b_re, b_im, c_re, c_im):
    depth = lam_re.shape[0]
    lr = lam_re.astype(F32)
    li = lam_im.astype(F32)
    dt = jnp.exp(log_dt.astype(F32))[..., None]
    mag = jnp.exp(lr * dt)
    a_re = mag * jnp.cos(li * dt)
    a_im = mag * jnp.sin(li * dt)
    den = lr * lr + li * li
    k_re = ((a_re - 1.0) * lr + a_im * li) / den
    k_im = (a_im * lr - (a_re - 1.0) * li) / den
    bre = b_re.astype(F32)
    bim = b_im.astype(F32)
    bb_re = k_re[..., None] * bre - k_im[..., None] * bim
    bb_im = k_re[..., None] * bim + k_im[..., None] * bre
    gpb = S5_GROUPS // S5_BLOCKS
    eye = jnp.eye(gpb, dtype=F32)

    def pack_in(w):
        w = w.reshape(depth, S5_BLOCKS, gpb, S5_STATE, S5_GROUP)
        out = jnp.einsum('ljgph,gk->ljghkp', w, eye)
        return out.reshape(depth, S5_BLOCKS, gpb * S5_GROUP, gpb * S5_STATE).astype(BF16)

    def pack_out(w):
        w = w.reshape(depth, S5_BLOCKS, gpb, S5_GROUP, S5_STATE)
        out = jnp.einsum('ljghp,gk->ljgpkh', w, eye)
        return out.reshape(depth, S5_BLOCKS, gpb * S5_STATE, gpb * S5_GROUP).astype(BF16)

    return (a_re.reshape(depth, 1, S5_N), a_im.reshape(depth, 1, S5_N), pack_in(bb_re), pack_in(bb_im),
            pack_out(c_re.astype(F32)), pack_out(c_im.astype(F32)))


def kernel(x_prompt, x_sample, state_hgrn, state_s5_re, state_s5_im, norm_g, ffn_w_gate, ffn_w_up, ffn_w_down, w_in, hgrn_lb_logits, hgrn_norm_g, gmlp_ws, gmlp_bs, gmlp_norm_g, gmlp_norm_b, s5_lam_re, s5_lam_im, s5_log_dt, s5_b_re, s5_b_im, s5_c_re, s5_c_im, s5_d, s5_glu_w, s5_glu_b, w_branch, w_out):
    nb, seq, _ = x_prompt.shape
    ns = x_sample.shape[0]
    depth = norm_g.shape[0]
    tm = 1024
    tl = 128
    tg = 512
    tt = 128
    bb = 16
    nt = seq // tm
    n_p = nb * nt

    def bmajor(i):
        return (jnp.minimum(i, n_p - 1), 0)

    def tmajor(i):
        j = jnp.minimum(i, n_p - 1)
        return (j % nt, j // nt)

    def ffn_riders(l, which):
        return [_rider(ffn_w_gate, (l, which)), _rider(ffn_w_up, (l, which)),
                _rider(ffn_w_down, (l, which))]

    def mixer_riders(l):
        in_splits = ((0, COL_GM), (COL_GM, COL_S5), (COL_S5, COL_GATE), (COL_GATE, N_IN))
        return ([_rider(w_in, (l,), in_splits)]
                + [_rider(w_branch, (l, n)) for n in range(N_BRANCH)]
                + [_rider(w_out, (l,)), _rider(s5_glu_w, (l,))])

    n_norm = norm_g.shape[1]
    g_all = norm_g.reshape(depth * n_norm, 1, D_MODEL)
    ng_all = hgrn_norm_g.reshape(depth, 1, HG_WIDTH)
    lng_all = gmlp_norm_g.reshape(depth, 1, GM_WIDTH)
    lnb_all = gmlp_norm_b.reshape(depth, 1, GM_WIDTH)
    tril = jnp.tril(jnp.ones((GM_CHUNK, GM_CHUNK), F32))
    wm_all = (gmlp_ws * tril).astype(BF16)
    bst_all = gmlp_bs.transpose(0, 2, 1)
    gc = GM_WIDTH // GM_GROUPS
    w00_all = jnp.repeat(gmlp_ws[:, :, 0, 0], gc, axis=1).reshape(depth, 1, GM_WIDTH)
    b00_all = jnp.repeat(gmlp_bs[:, :, 0], gc, axis=1).reshape(depth, 1, GM_WIDTH)
    s5p_all = _s5_params(s5_lam_re, s5_lam_im, s5_log_dt, s5_b_re, s5_b_im, s5_c_re, s5_c_im)
    s5d_all = s5_d.reshape(depth, 1, S5_WIDTH)
    gb_all = s5_glu_b.reshape(depth, 1, 2 * S5_WIDTH)
    logits = hgrn_lb_logits.astype(F32)

    yp = x_prompt.reshape(nb * seq, D_MODEL)
    ys = x_sample.reshape(ns, D_MODEL)
    ffn_w = _cast_call(ffn_riders(0, 0), n_p)

    outs = {k: [] for k in ('hg_p', 're_p', 'im_p', 're_s', 'im_s', 'v_s')}
    hg_s = None
    for l in range(depth):
        g = [_Row(g_all, l * n_norm + i) for i in range(n_norm)]
        ng, lng, lnb = _Row(ng_all, l), _Row(lng_all, l), _Row(lnb_all, l)
        s5p = [_Row(p, l) for p in s5p_all]
        s5d, gb = _Row(s5d_all, l), _Row(gb_all, l)
        tm_shape = (seq, nb * D_MODEL)

        x1, xs1, h1, hs1, w_hg, w_gm, w_s5, w_gate, wb0, wb1, wb2, wout, gw = _ffn_call(
            yp, ys, g[0], g[1], g[2], *ffn_w, mixer_riders(l), emit_h=True, n_p=n_p, tm=tm,
            in_map=bmajor if l == 0 else tmajor, out_map=tmajor, out_shape=tm_shape)
        a_p, s_hg = _hgrn_prompt_call(h1, w_hg, logits, ng, layer=l, nb=nb, seq=seq, tl=tl)
        b_p = _gmlp_prompt_call(h1, w_gm, lng, lnb, _Row(wm_all, l), _Row(bst_all, l),
                                nb=nb, seq=seq, tg=tg)
        c_p, h_re, h_im = _s5_call(h1, w_s5, s5p, s5d, gw, gb, nb=nb, seq=seq, tt=tt)
        outs['hg_p'].append(s_hg)
        outs['re_p'].append(h_re.reshape(nb, S5_GROUPS, S5_STATE))
        outs['im_p'].append(h_im.reshape(nb, S5_GROUPS, S5_STATE))

        q, f, kt, v, xg, b_s, v_rows = _sample_proj_call(
            hs1, w_hg, w_gm, logits, lng, lnb, _Row(w00_all, l), _Row(b00_all, l), layer=l, n=ns)
        fc = f.reshape(ns // bb, bb, HG_HEADS, HG_DK).transpose(0, 2, 3, 1)
        hg_s, a_s = _hgrn_step_call(state_hgrn, hg_s, fc, kt, q, v, xg, ng, layer=l, n=ns, bb=bb)
        h0 = (state_s5_re[l].reshape(ns, S5_N), state_s5_im[l].reshape(ns, S5_N))
        c_s, hs_re, hs_im = _s5_call(hs1, w_s5, s5p, s5d, gw, gb, h0, nb=ns, seq=1, tt=1)

        x2, xs2, *ffn_w = _merge_call(x1, h1, a_p, b_p, c_p, xs1, hs1, a_s, b_s, c_s, g[3],
                                      w_gate, (wb0, wb1, wb2), wout, ffn_riders(l, 1),
                                      n_p=n_p, tm=tm, xmap=tmajor, out_shape=tm_shape)
        last = l == depth - 1
        yp, ys, *ffn_w = _ffn_call(x2, xs2, g[4], g[5], g[5], *ffn_w,
                                   [] if last else ffn_riders(l + 1, 0), emit_h=False, n_p=n_p,
                                   tm=tm, in_map=tmajor, out_map=bmajor if last else tmajor,
                                   out_shape=(nb * seq, D_MODEL) if last else tm_shape)
        outs['re_s'].append(hs_re.reshape(ns, S5_GROUPS, S5_STATE))
        outs['im_s'].append(hs_im.reshape(ns, S5_GROUPS, S5_STATE))
        outs['v_s'].append(v_rows.reshape(ns, 1, GM_WIDTH))

    return (yp.reshape(nb, seq, D_MODEL), ys.reshape(ns, 1, D_MODEL),
            jnp.stack(outs['hg_p']), jnp.stack(outs['re_p']), jnp.stack(outs['im_p']),
            hg_s, jnp.stack(outs['re_s']), jnp.stack(outs['im_s']),
            jnp.stack(outs['v_s']))
```

```python
import functools
import math

import jax
import jax.numpy as jnp
from jax import lax
from jax.experimental import pallas as pl
from jax.experimental.pallas import tpu as pltpu

F32 = jnp.float32
BF16 = jnp.bfloat16

D_MODEL = 1024
D_FF = 2816
HG_HEADS = 4
HG_DK = 128
HG_WIDTH = HG_HEADS * HG_DK
GM_GROUPS = 4
GM_CHUNK = 128
GM_WIDTH = 512
S5_WIDTH = 512
S5_GROUP = 16
S5_GROUPS = S5_WIDTH // S5_GROUP
S5_STATE = 64
S5_N = S5_GROUPS * S5_STATE
S5_BLOCKS = 4
N_BRANCH = 3
EPS = 1e-6

COL_GM = 4 * HG_WIDTH
COL_S5 = COL_GM + 2 * GM_WIDTH
COL_GATE = COL_S5 + S5_WIDTH
N_IN = COL_GATE + N_BRANCH * D_MODEL

GLA_CHUNK = 32
FF_CHUNK = 512
TILE_PIECES = 2
VMEM_LIMIT = 56 * 1024 * 1024


def _rms(x, g):
    return x * lax.rsqrt(jnp.mean(x * x, axis=-1, keepdims=True) + EPS) * g


def _gelu(x):
    c = math.sqrt(2.0 / math.pi)
    hx = 0.5 * x
    return hx + hx * jnp.tanh(x * (c + (c * 0.044715) * (x * x)))


def _sigmoid(x):
    return 0.5 + 0.5 * jnp.tanh(0.5 * x)


def _dot(a, b):
    return jnp.dot(a, b, preferred_element_type=F32)


def _dot_nt(a, b):
    return lax.dot_general(a, b, (((1,), (1,)), ((), ())), preferred_element_type=F32)


def _dot_tn(a, b):
    return lax.dot_general(a, b, (((0,), (0,)), ((), ())), preferred_element_type=F32)


def _const_spec(shape):
    n = len(shape)
    return pl.BlockSpec(shape, lambda *_: (0,) * n, pipeline_mode=pl.Buffered(1))


class _Row:
    def __init__(self, arr, index):
        self.arr, self.index = arr, index

    @property
    def spec(self):
        tail = self.arr.shape[1:]
        index = self.index
        return pl.BlockSpec((None,) + tail, lambda *_: (index,) + (0,) * len(tail),
                            pipeline_mode=pl.Buffered(1))


def _params(n_grid):
    return pltpu.CompilerParams(
        dimension_semantics=("arbitrary",) * n_grid, vmem_limit_bytes=VMEM_LIMIT)


def _rider(src, lead=(), splits=None):
    k, n = src.shape[-2:]
    return dict(src=src, lead=tuple(lead), k=k, n=n, splits=tuple(splits or ((0, n),)))


def _rider_specs(riders, n_chunks):
    in_specs, out_specs, out_shapes = [], [], []
    for r in riders:
        rows = r['k'] // n_chunks
        assert rows * n_chunks == r['k'] and rows % 16 == 0, (r['k'], n_chunks)

        def src_map(i, lead=r['lead']):
            return lead + (jnp.minimum(i, n_chunks - 1), 0)

        in_specs.append(pl.BlockSpec((None,) * len(r['lead']) + (rows, r['n']), src_map))
        for c0, c1 in r['splits']:
            out_specs.append(pl.BlockSpec((rows, c1 - c0), lambda i: (jnp.minimum(i, n_chunks - 1), 0)))
            out_shapes.append(jax.ShapeDtypeStruct((r['k'], c1 - c0), BF16))
    return in_specs, out_specs, out_shapes


def _run_riders(src_refs, out_refs, splits):
    k = 0
    for src, sp in zip(src_refs, splits):
        for c0, c1 in sp:
            out_refs[k][...] = src[:, c0:c1].astype(BF16)
            k += 1


def _cast_kernel(splits, *refs):
    n = len(splits)
    _run_riders(refs[:n], refs[n:], splits)


def _cast_call(riders, n_chunks):
    in_specs, out_specs, out_shapes = _rider_specs(riders, n_chunks)
    return pl.pallas_call(
        functools.partial(_cast_kernel, tuple(r['splits'] for r in riders)),
        grid=(n_chunks,),
        in_specs=in_specs,
        out_specs=out_specs,
        out_shape=out_shapes,
        compiler_params=_params(1),
        name="cast",
    )(*[r['src'] for r in riders])


def _staggered(n_pieces, n_stages, begin, stage, end):
    for step in range(n_stages + n_pieces - 1):
        for p in range(n_pieces):
            s = step - p
            if s == 0:
                begin(p)
            if 0 <= s < n_stages:
                stage(p, s)
            if s == n_stages - 1:
                end(p)


def _ffn_tile(src_ref, dst_ref, h_ref, g_pre, g_post, g_next, wg_ref, wu_ref, wd_ref, n_pieces):
    rows = src_ref.shape[0] // n_pieces
    chunks = [(c0, min(c0 + FF_CHUNK, D_FF)) for c0 in range(0, D_FF, FF_CHUNK)]
    hs = [None] * n_pieces
    ds = [None] * n_pieces

    def begin(p):
        hs[p] = _rms(src_ref[p * rows:(p + 1) * rows, :], g_pre).astype(BF16)

    def stage(p, s):
        c0, c1 = chunks[s]
        a = _dot(hs[p], wg_ref[:, c0:c1])
        u = _dot(hs[p], wu_ref[:, c0:c1])
        m = (a * _sigmoid(a) * u).astype(BF16)
        part = _dot(m, wd_ref[c0:c1, :])
        ds[p] = part if ds[p] is None else ds[p] + part

    def end(p):
        r = slice(p * rows, (p + 1) * rows)
        y = src_ref[r, :] + 0.5 * _rms(ds[p], g_post)
        dst_ref[r, :] = y
        if h_ref is not None:
            h_ref[r, :] = _rms(y, g_next).astype(BF16)

    _staggered(n_pieces, len(chunks), begin, stage, end)


def _ffn_kernel(n_p, emit_h, splits, *refs):
    x_ref, xs_ref, gpre_ref, gpost_ref, gnext_ref, wg, wu, wd = refs[:8]
    n_r = len(splits)
    rider_src = refs[8:8 + n_r]
    n_out = 4 if emit_h else 2
    outs = refs[8 + n_r:8 + n_r + n_out]
    rider_out = refs[8 + n_r + n_out:]
    i = pl.program_id(0)
    _run_riders(rider_src, rider_out, splits)

    def run(src_ref, dst_ref, h_ref, n_pieces):
        _ffn_tile(src_ref, dst_ref, h_ref, gpre_ref[...], gpost_ref[...], gnext_ref[...],
                  wg, wu, wd, n_pieces)

    @pl.when(i < n_p)
    def _():
        run(x_ref, outs[0], outs[2] if emit_h else None, TILE_PIECES)

    @pl.when(i == n_p)
    def _():
        run(xs_ref, outs[1], outs[3] if emit_h else None, 1)


def _ffn_call(x2d, xs, g_pre, g_post, g_next, wg, wu, wd, riders, *, emit_h, n_p, tm, in_map,
              out_map, out_shape):
    ns = xs.shape[0]
    r_in, r_out, r_shapes = _rider_specs(riders, n_p)
    out_specs = [pl.BlockSpec((tm, D_MODEL), out_map), pl.BlockSpec((ns, D_MODEL), lambda i: (0, 0))]
    out_shapes = [jax.ShapeDtypeStruct(out_shape, F32), jax.ShapeDtypeStruct((ns, D_MODEL), F32)]
    if emit_h:
        out_specs = out_specs + out_specs
        out_shapes += [jax.ShapeDtypeStruct(out_shape, BF16), jax.ShapeDtypeStruct((ns, D_MODEL), BF16)]
    return pl.pallas_call(
        functools.partial(_ffn_kernel, n_p, emit_h, tuple(r['splits'] for r in riders)),
        grid=(n_p + 1,),
        in_specs=[
            pl.BlockSpec((tm, D_MODEL), in_map),
            _const_spec((ns, D_MODEL)),
            g_pre.spec, g_post.spec, g_next.spec,
            _const_spec((D_MODEL, D_FF)),
            _const_spec((D_MODEL, D_FF)),
            _const_spec((D_FF, D_MODEL)),
        ] + r_in,
        out_specs=out_specs + r_out,
        out_shape=out_shapes + r_shapes,
        compiler_params=_params(1),
        name="ffn",
    )(x2d, xs, g_pre.arr, g_post.arr, g_next.arr, wg, wu, wd, *[r['src'] for r in riders])


def _hgrn_gates(z, lb):
    ls = jnp.minimum(z, 0.0) - jnp.log1p(jnp.exp(-jnp.abs(z)))
    c = jnp.log1p(-lb) + ls
    a = jnp.log(jnp.maximum(lb, 1e-37))
    lae = jnp.maximum(a, c) + jnp.log1p(jnp.exp(-jnp.abs(a - c)))
    logf = jnp.where(lb > 0.0, lae, c)
    k = (1.0 - lb) * _sigmoid(-z)
    return logf, k


def _hgrn_out(o, xg, ng):
    parts = []
    for hd in range(HG_HEADS):
        oh = o[:, hd * HG_DK:(hd + 1) * HG_DK]
        parts.append(oh * lax.rsqrt(jnp.mean(oh * oh, axis=-1, keepdims=True) + EPS))
    on = jnp.concatenate(parts, axis=-1)
    return on * ng * (xg * _sigmoid(xg))


def _lower_bound(logits_ref, layer):
    lg = logits_ref[...]
    m = jnp.max(lg, axis=0, keepdims=True)
    e = jnp.exp(lg - m)
    den = jnp.sum(e, axis=0, keepdims=True)
    lb = jnp.zeros_like(den)
    for j in range(1, layer + 1):
        lb = lb + e[j:j + 1] / den
    return lb


def _hgrn_prompt_kernel(layer, nb, tl, h_ref, w_ref, logits_ref, ng_ref, tri_ref,
                        a_ref, sfin_ref, st_ref, q_s, k_s, v_s, b_s, o_s):
    t = pl.program_id(0)
    nt = pl.num_programs(0)

    @pl.when(t == 0)
    def _():
        st_ref[...] = jnp.zeros_like(st_ref)

    h = jnp.concatenate([h_ref[:, b * D_MODEL:(b + 1) * D_MODEL] for b in range(nb)], axis=0)
    p = _dot(h, w_ref[...])
    xq = p[:, 0 * HG_WIDTH:1 * HG_WIDTH]
    z = p[:, 1 * HG_WIDTH:2 * HG_WIDTH]
    xi = p[:, 2 * HG_WIDTH:3 * HG_WIDTH]
    xg = p[:, 3 * HG_WIDTH:4 * HG_WIDTH]
    lb = _lower_bound(logits_ref, layer)
    logf, kk = _hgrn_gates(z, lb)

    tri = tri_ref[...]
    tr = tri.shape[0]
    l0 = logf.astype(BF16)
    r1 = logf - l0.astype(F32)
    l1 = r1.astype(BF16)
    l2 = (r1 - l1.astype(F32)).astype(BF16)
    for i in range(nb * tl // tr):
        rs = slice(i * tr, (i + 1) * tr)
        b_s[rs, :] = _dot(tri, l0[rs]) + _dot(tri, l1[rs]) + _dot(tri, l2[rs])

    q_s[...] = xq
    k_s[...] = kk
    v_s[...] = xi

    c = GLA_CHUNK
    mid = c // 2 - 1
    row = lax.broadcasted_iota(jnp.int32, (c, c), 0)
    col = lax.broadcasted_iota(jnp.int32, (c, c), 1)
    causal = col <= row

    def chunk(ci, carry):
        pending = []
        for b in range(nb):
            r = pl.ds(pl.multiple_of(b * tl + ci * c, c), c)
            for hd in range(HG_HEADS):
                cs = slice(hd * HG_DK, (hd + 1) * HG_DK)
                bq = b_s[r, cs]
                qc = q_s[r, cs]
                kc = k_s[r, cs]
                vc = v_s[r, cs].astype(BF16)
                b_mid = bq[mid:mid + 1]
                b_last = bq[c - 1:c]
                st = st_ref[b, hd]
                q_in = (qc * jnp.exp(bq)).astype(BF16)
                q_md = (qc * jnp.exp(bq - b_mid)).astype(BF16)
                k_md = (kc * jnp.exp(b_mid - bq)).astype(BF16)
                k_ls = (kc * jnp.exp(b_last - bq)).astype(BF16)
                scores = _dot_nt(q_md, k_md)
                o_inter = _dot_nt(q_in, st.astype(BF16))
                st_ref[b, hd] = st * jnp.exp(b_last) + _dot_tn(vc, k_ls)
                pending.append((r, cs, scores, o_inter, vc))
        for r, cs, scores, o_inter, vc in pending:
            sc = jnp.where(causal, scores, 0.0).astype(BF16)
            o_s[r, cs] = o_inter + _dot(sc, vc)
        return carry

    lax.fori_loop(0, tl // c, chunk, 0)

    a = _hgrn_out(o_s[...], xg, ng_ref[...]).astype(BF16)
    for b in range(nb):
        a_ref[:, b * HG_WIDTH:(b + 1) * HG_WIDTH] = a[b * tl:(b + 1) * tl]

    @pl.when(t == nt - 1)
    def _():
        for b in range(nb):
            for hd in range(HG_HEADS):
                sfin_ref[b, hd] = st_ref[b, hd].T


def _hgrn_prompt_call(h_tm, w_hg, logits, ng, *, layer, nb, seq, tl):
    c = GLA_CHUNK
    tr = min(256, nb * tl)
    idx = jnp.arange(tr)
    tri = ((idx[:, None] >= idx[None, :]) & (idx[:, None] // c == idx[None, :] // c)).astype(BF16)
    depth = logits.shape[0]
    rows = nb * tl
    return pl.pallas_call(
        functools.partial(_hgrn_prompt_kernel, layer, nb, tl),
        grid=(seq // tl,),
        in_specs=[
            pl.BlockSpec((tl, nb * D_MODEL), lambda t: (t, 0)),
            _const_spec((D_MODEL, 4 * HG_WIDTH)),
            _const_spec((depth, HG_WIDTH)),
            ng.spec,
            _const_spec((tr, tr)),
        ],
        out_specs=[
            pl.BlockSpec((tl, nb * HG_WIDTH), lambda t: (t, 0)),
            pl.BlockSpec((nb, HG_HEADS, HG_DK, HG_DK), lambda t: (0, 0, 0, 0)),
        ],
        out_shape=[
            jax.ShapeDtypeStruct((seq, nb * HG_WIDTH), BF16),
            jax.ShapeDtypeStruct((nb, HG_HEADS, HG_DK, HG_DK), F32),
        ],
        scratch_shapes=[
            pltpu.VMEM((nb, HG_HEADS, HG_DK, HG_DK), F32),
            pltpu.VMEM((rows, HG_WIDTH), F32),
            pltpu.VMEM((rows, HG_WIDTH), F32),
            pltpu.VMEM((rows, HG_WIDTH), F32),
            pltpu.VMEM((rows, HG_WIDTH), F32),
            pltpu.VMEM((rows, HG_WIDTH), F32),
        ],
        compiler_params=_params(1),
        name="hgrn_prompt",
    )(h_tm, w_hg, logits, ng.arr, tri)


def _gmlp_uv(h, w_ref, lng, lnb):
    p = _dot(h, w_ref[...])
    u = _gelu(p[:, :GM_WIDTH])
    gv = _gelu(p[:, GM_WIDTH:])
    mu = jnp.mean(gv, axis=-1, keepdims=True)
    dv = gv - mu
    var = jnp.mean(dv * dv, axis=-1, keepdims=True)
    v = dv * lax.rsqrt(var + EPS) * lng + lnb
    return u, v


def _gmlp_prompt_kernel(tg, h_ref, w_ref, lng_ref, lnb_ref, wm_ref, bst_ref, o_ref):
    u, v = _gmlp_uv(h_ref[...], w_ref, lng_ref[...], lnb_ref[...])
    vb = v.astype(BF16)
    gc = GM_WIDTH // GM_GROUPS
    rows = []
    for j in range(tg // GM_CHUNK):
        parts = []
        for gi in range(GM_GROUPS):
            vj = vb[j * GM_CHUNK:(j + 1) * GM_CHUNK, gi * gc:(gi + 1) * gc]
            parts.append(_dot(wm_ref[gi], vj) + bst_ref[:, gi:gi + 1])
        rows.append(jnp.concatenate(parts, axis=-1))
    mix = jnp.concatenate(rows, axis=0)
    o_ref[...] = (u * mix).astype(BF16)


def _gmlp_prompt_call(h_tm, w_gm, lng, lnb, wm, bst, *, nb, seq, tg):
    return pl.pallas_call(
        functools.partial(_gmlp_prompt_kernel, tg),
        grid=(nb, seq // tg),
        in_specs=[
            pl.BlockSpec((tg, D_MODEL), lambda b, t: (t, b)),
            _const_spec((D_MODEL, 2 * GM_WIDTH)),
            lng.spec, lnb.spec, wm.spec, bst.spec,
        ],
        out_specs=pl.BlockSpec((tg, GM_WIDTH), lambda b, t: (t, b)),
        out_shape=jax.ShapeDtypeStruct((seq, nb * GM_WIDTH), BF16),
        compiler_params=_params(2),
        name="gmlp_prompt",
    )(h_tm, w_gm, lng.arr, lnb.arr, wm.arr, bst.arr)


def _s5_kernel(nb, tt, has_h0, relayout, *refs):
    (x_ref, w_ref, are_ref, aim_ref, bre_ref, bim_ref, cre_ref, cim_ref, d_ref,
     gw_ref, gb_ref) = refs[:11]
    h0re_ref, h0im_ref = refs[11:13] if has_h0 else (None, None)
    c_ref, hre_ref, him_ref, sre, sim, lay = refs[-6:]
    t = pl.program_id(0)
    wblk = S5_WIDTH // S5_BLOCKS
    sblk = S5_N // S5_BLOCKS

    @pl.when(t == 0)
    def _():
        if has_h0:
            hre_ref[...] = h0re_ref[...]
            him_ref[...] = h0im_ref[...]
        else:
            hre_ref[...] = jnp.zeros_like(hre_ref)
            him_ref[...] = jnp.zeros_like(him_ref)

    if relayout:
        h = jnp.concatenate([x_ref[:, b * D_MODEL:(b + 1) * D_MODEL] for b in range(nb)], axis=0)
    else:
        h = x_ref[...]
    su = _dot(h, w_ref[...])
    if relayout:
        for b in range(nb):
            for j in range(S5_BLOCKS):
                lay[j, pl.ds(b, tt, stride=nb), :] = su[b * tt:(b + 1) * tt, j * wblk:(j + 1) * wblk]
        su = jnp.concatenate([lay[j] for j in range(S5_BLOCKS)], axis=-1)
    sub = su.astype(BF16)

    def project(j):
        uj = sub[:, j * wblk:(j + 1) * wblk]
        sre[:, j * sblk:(j + 1) * sblk] = _dot(uj, bre_ref[j])
        sim[:, j * sblk:(j + 1) * sblk] = _dot(uj, bim_ref[j])

    def scan(j):
        cols = slice(j * sblk, (j + 1) * sblk)
        a_re = jnp.broadcast_to(are_ref[:, cols], (nb, sblk))
        a_im = jnp.broadcast_to(aim_ref[:, cols], (nb, sblk))
        h_re = hre_ref[:, cols]
        h_im = him_ref[:, cols]
        for i in range(tt):
            r = slice(i * nb, (i + 1) * nb)
            n_re = a_re * h_re - a_im * h_im + sre[r, cols]
            n_im = a_re * h_im + a_im * h_re + sim[r, cols]
            sre[r, cols] = n_re
            sim[r, cols] = n_im
            h_re, h_im = n_re, n_im
        hre_ref[:, cols] = h_re
        him_ref[:, cols] = h_im

    def readout(j):
        hr = sre[:, j * sblk:(j + 1) * sblk].astype(BF16)
        hi = sim[:, j * sblk:(j + 1) * sblk].astype(BF16)
        return _dot(hr, cre_ref[j]) - _dot(hi, cim_ref[j])

    ys = []
    project(0)
    for j in range(S5_BLOCKS):
        if j + 1 < S5_BLOCKS:
            project(j + 1)
        scan(j)
        ys.append(readout(j))
    y = jnp.concatenate(ys, axis=-1) + d_ref[...] * su
    zz = _dot(_gelu(y).astype(BF16), gw_ref[...]) + gb_ref[...]
    c = zz[:, :S5_WIDTH] * _sigmoid(zz[:, S5_WIDTH:])
    if relayout:
        for j in range(S5_BLOCKS):
            lay[j] = c[:, j * wblk:(j + 1) * wblk]
        for b in range(nb):
            cb = jnp.concatenate([lay[j, pl.ds(b, tt, stride=nb), :] for j in range(S5_BLOCKS)], axis=-1)
            c_ref[:, b * S5_WIDTH:(b + 1) * S5_WIDTH] = cb.astype(BF16)
    else:
        c_ref[...] = c.astype(BF16)


def _s5_call(h2d, w_s5, s5p, d, gw, gb, h0=None, *, nb, seq, tt):
    rows = tt * nb
    relayout = seq > 1
    wblk = S5_WIDTH // S5_BLOCKS
    sblk = S5_N // S5_BLOCKS
    x_block = (tt, nb * D_MODEL) if relayout else (rows, D_MODEL)
    c_block = (tt, nb * S5_WIDTH) if relayout else (rows, S5_WIDTH)
    in_specs = [
        pl.BlockSpec(x_block, lambda t: (t, 0)),
        _const_spec((D_MODEL, S5_WIDTH)),
    ] + [p.spec for p in s5p] + [
        d.spec,
        _const_spec((S5_WIDTH, 2 * S5_WIDTH)),
        gb.spec,
    ]
    args = [h2d, w_s5] + [p.arr for p in s5p] + [d.arr, gw, gb.arr]
    if h0 is not None:
        in_specs += [_const_spec((nb, S5_N)), _const_spec((nb, S5_N))]
        args += list(h0)
    state_spec = pl.BlockSpec((nb, S5_N), lambda t: (0, 0))
    c_shape = (seq, nb * S5_WIDTH) if relayout else (nb, S5_WIDTH)
    return pl.pallas_call(
        functools.partial(_s5_kernel, nb, tt, h0 is not None, relayout),
        grid=(seq // tt,),
        in_specs=in_specs,
        out_specs=[pl.BlockSpec(c_block, lambda t: (t, 0)), state_spec, state_spec],
        out_shape=[
            jax.ShapeDtypeStruct(c_shape, BF16),
            jax.ShapeDtypeStruct((nb, S5_N), F32),
            jax.ShapeDtypeStruct((nb, S5_N), F32),
        ],
        scratch_shapes=[pltpu.VMEM((rows, S5_N), F32), pltpu.VMEM((rows, S5_N), F32),
                        pltpu.VMEM((S5_BLOCKS, rows, wblk), F32)],
        compiler_params=_params(1),
        name="s5",
    )(*args)


def _merge_tile(x_ref, h_ref, br_refs, dst_ref, g3, wgate, wbr, wout, n_pieces):
    rows = x_ref.shape[0] // n_pieces
    merged = [None] * n_pieces
    proj = [None] * n_pieces

    def begin(p):
        pass

    def stage(p, s):
        r = slice(p * rows, (p + 1) * rows)
        if s < N_BRANCH:
            gate = _sigmoid(_dot(h_ref[r, :], wgate[:, s * D_MODEL:(s + 1) * D_MODEL]))
            term = gate * _dot(br_refs[s][r, :], wbr[s][...])
            merged[p] = term if merged[p] is None else merged[p] + term
        else:
            proj[p] = _dot(merged[p].astype(BF16), wout[...])

    def end(p):
        r = slice(p * rows, (p + 1) * rows)
        dst_ref[r, :] = x_ref[r, :] + _rms(proj[p], g3)

    _staggered(n_pieces, N_BRANCH + 1, begin, stage, end)


def _merge_kernel(n_p, splits, *refs):
    (x_ref, h_ref, a_ref, b_ref, c_ref, xs_ref, hs_ref, as_ref, bs_ref, cs_ref, g3_ref,
     wgate, wb0, wb1, wb2, wout) = refs[:16]
    n_r = len(splits)
    rider_src = refs[16:16 + n_r]
    o_ref, os_ref = refs[16 + n_r:18 + n_r]
    rider_out = refs[18 + n_r:]
    wbr = (wb0, wb1, wb2)
    i = pl.program_id(0)
    _run_riders(rider_src, rider_out, splits)

    @pl.when(i < n_p)
    def _():
        _merge_tile(x_ref, h_ref, (a_ref, b_ref, c_ref), o_ref, g3_ref[...], wgate, wbr, wout,
                    TILE_PIECES)

    @pl.when(i == n_p)
    def _():
        _merge_tile(xs_ref, hs_ref, (as_ref, bs_ref, cs_ref), os_ref, g3_ref[...], wgate, wbr, wout, 1)


def _merge_call(x2d, h2d, a, b, c, xs, hs, a_s, b_s, c_s, g3, wgate, wbr, wout, riders,
                *, n_p, tm, xmap, out_shape):
    ns = xs.shape[0]
    r_in, r_out, r_shapes = _rider_specs(riders, n_p)
    return pl.pallas_call(
        functools.partial(_merge_kernel, n_p, tuple(r['splits'] for r in riders)),
        grid=(n_p + 1,),
        in_specs=[
            pl.BlockSpec((tm, D_MODEL), xmap),
            pl.BlockSpec((tm, D_MODEL), xmap),
            pl.BlockSpec((tm, HG_WIDTH), xmap),
            pl.BlockSpec((tm, GM_WIDTH), xmap),
            pl.BlockSpec((tm, S5_WIDTH), xmap),
            _const_spec((ns, D_MODEL)),
            _const_spec((ns, D_MODEL)),
            _const_spec((ns, HG_WIDTH)),
            _const_spec((ns, GM_WIDTH)),
            _const_spec((ns, S5_WIDTH)),
            g3.spec,
            _const_spec((D_MODEL, N_BRANCH * D_MODEL)),
            _const_spec((HG_WIDTH, D_MODEL)),
            _const_spec((GM_WIDTH, D_MODEL)),
            _const_spec((S5_WIDTH, D_MODEL)),
            _const_spec((D_MODEL, D_MODEL)),
        ] + r_in,
        out_specs=[pl.BlockSpec((tm, D_MODEL), xmap),
                   pl.BlockSpec((ns, D_MODEL), lambda i: (0, 0))] + r_out,
        out_shape=[jax.ShapeDtypeStruct(out_shape, F32),
                   jax.ShapeDtypeStruct((ns, D_MODEL), F32)] + r_shapes,
        compiler_params=_params(1),
        name="merge",
    )(x2d, h2d, a, b, c, xs, hs, a_s, b_s, c_s, g3.arr, wgate, *wbr, wout, *[r['src'] for r in riders])


def _sample_proj_kernel(layer, h_ref, whg_ref, wgm_ref, logits_ref, lng_ref, lnb_ref,
                        w00_ref, b00_ref, q_ref, f_ref, kt_ref, v_ref, xg_ref, bout_ref, vrow_ref):
    h = h_ref[...]
    p = _dot(h, whg_ref[...])
    z = p[:, 1 * HG_WIDTH:2 * HG_WIDTH]
    logf, kk = _hgrn_gates(z, _lower_bound(logits_ref, layer))
    q_ref[...] = p[:, 0 * HG_WIDTH:1 * HG_WIDTH]
    f_ref[...] = jnp.exp(logf)
    for hd in range(HG_HEADS):
        kt_ref[hd] = kk[:, hd * HG_DK:(hd + 1) * HG_DK].T.astype(BF16)
    v_ref[...] = p[:, 2 * HG_WIDTH:3 * HG_WIDTH]
    xg_ref[...] = p[:, 3 * HG_WIDTH:4 * HG_WIDTH]
    u, v = _gmlp_uv(h, wgm_ref, lng_ref[...], lnb_ref[...])
    bout_ref[...] = (u * (v * w00_ref[...] + b00_ref[...])).astype(BF16)
    vrow_ref[...] = v


def _sample_proj_call(h, w_hg, w_gm, logits, lng, lnb, w00, b00, *, layer, n):
    depth = logits.shape[0]
    assert n == HG_DK, "per-head key transpose is written for a square (sequences, keys) tile"
    f32_out = jax.ShapeDtypeStruct((n, HG_WIDTH), F32)
    row_spec = pl.BlockSpec((n, HG_WIDTH), lambda i: (0, 0))
    return pl.pallas_call(
        functools.partial(_sample_proj_kernel, layer),
        grid=(1,),
        in_specs=[
            _const_spec((n, D_MODEL)),
            _const_spec((D_MODEL, 4 * HG_WIDTH)),
            _const_spec((D_MODEL, 2 * GM_WIDTH)),
            _const_spec((depth, HG_WIDTH)),
            lng.spec, lnb.spec, w00.spec, b00.spec,
        ],
        out_specs=[row_spec, row_spec, pl.BlockSpec((HG_HEADS, HG_DK, n), lambda i: (0, 0, 0)),
                   row_spec, row_spec, row_spec, row_spec],
        out_shape=[f32_out, f32_out, jax.ShapeDtypeStruct((HG_HEADS, HG_DK, n), BF16),
                   f32_out, f32_out, jax.ShapeDtypeStruct((n, GM_WIDTH), BF16), f32_out],
        compiler_params=_params(1),
        name="sample_proj",
    )(h, w_hg, w_gm, logits, lng.arr, lnb.arr, w00.arr, b00.arr)


def _hgrn_step_kernel(bb, layer, whole_stack, *refs):
    s_ref, fc_ref, kt_ref, q_ref, v_ref, xg_ref, ng_ref = refs[:7]
    snew_ref, a_ref, o_s = refs[-3:]
    if whole_stack:
        for j in range(snew_ref.shape[0]):
            if j != layer:
                snew_ref[j] = jnp.zeros(snew_ref.shape[1:], F32)
        snew_ref = snew_ref.at[layer]
    n = kt_ref.shape[-1]
    seq_id = lax.broadcasted_iota(jnp.int32, (n, HG_DK), 0)
    first = pl.program_id(0) * bb
    for i in range(bb):
        here = seq_id == first + i
        for hd in range(HG_HEADS):
            cs = slice(hd * HG_DK, (hd + 1) * HG_DK)
            v_only = jnp.where(here, jnp.broadcast_to(v_ref[i:i + 1, cs], (n, HG_DK)), 0.0)
            kv = _dot(kt_ref[hd], v_only.astype(BF16))
            f_col = fc_ref[0, hd, :, i:i + 1]
            s_new = f_col * s_ref[i, hd] + kv
            snew_ref[i, hd] = s_new
            q8 = jnp.broadcast_to(q_ref[i:i + 1, cs], (8, HG_DK)).astype(BF16)
            o_s[i:i + 1, cs] = _dot(q8, s_new.astype(BF16))[0:1]
    a_ref[...] = _hgrn_out(o_s[...], xg_ref[...], ng_ref[...]).astype(BF16)


def _hgrn_step_call(s_all, prev, fc, kt, q, v, xg, ng, *, layer, n, bb):
    col_spec = pl.BlockSpec((1, HG_HEADS, HG_DK, bb), lambda i: (i, 0, 0, 0))
    row_spec = pl.BlockSpec((bb, HG_WIDTH), lambda i: (i, 0))
    st_spec = pl.BlockSpec((None, bb, HG_HEADS, HG_DK, HG_DK), lambda i: (layer, i, 0, 0, 0))
    in_specs = [st_spec, col_spec, _const_spec((HG_HEADS, HG_DK, n)), row_spec, row_spec, row_spec,
                ng.spec]
    args = [s_all, fc, kt, q, v, xg, ng.arr]
    aliases = {}
    out_st_spec = st_spec
    if prev is not None:
        in_specs.append(pl.BlockSpec(memory_space=pl.ANY))
        args.append(prev)
        aliases = {len(args) - 1: 0}
    else:
        out_st_spec = pl.BlockSpec((s_all.shape[0], bb, HG_HEADS, HG_DK, HG_DK), lambda i: (0, i, 0, 0, 0))
    return pl.pallas_call(
        functools.partial(_hgrn_step_kernel, bb, layer, prev is None),
        grid=(n // bb,),
        in_specs=in_specs,
        out_specs=[out_st_spec, row_spec],
        out_shape=[
            jax.ShapeDtypeStruct(s_all.shape, F32),
            jax.ShapeDtypeStruct((n, HG_WIDTH), BF16),
        ],
        scratch_shapes=[pltpu.VMEM((bb, HG_WIDTH), F32)],
        input_output_aliases=aliases,
        compiler_params=_params(1),
        name="hgrn_step",
    )(*args)


def _s5_params(lam_re, lam_im, log_dt, b_re, b_im, c_re, c_im):
    depth = lam_re.shape[0]
    lr = lam_re.astype(F32)
    li = lam_im.astype(F32)
    dt = jnp.exp(log_dt.astype(F32))[..., None]
    mag = jnp.exp(lr * dt)
    a_re = mag * jnp.cos(li * dt)
    a_im = mag * jnp.sin(li * dt)
    den = lr * lr + li * li
    k_re = ((a_re - 1.0) * lr + a_im * li) / den
    k_im = (a_im * lr - (a_re - 1.0) * li) / den
    bre = b_re.astype(F32)
    bim = b_im.astype(F32)
    bb_re = k_re[..., None] * bre - k_im[..., None] * bim
    bb_im = k_re[..., None] * bim + k_im[..., None] * bre
    gpb = S5_GROUPS // S5_BLOCKS
    eye = jnp.eye(gpb, dtype=F32)

    def pack_in(w):
        w = w.reshape(depth, S5_BLOCKS, gpb, S5_STATE, S5_GROUP)
        out = jnp.einsum('ljgph,gk->ljghkp', w, eye)
        return out.reshape(depth, S5_BLOCKS, gpb * S5_GROUP, gpb * S5_STATE).astype(BF16)

    def pack_out(w):
        w = w.reshape(depth, S5_BLOCKS, gpb, S5_GROUP, S5_STATE)
        out = jnp.einsum('ljghp,gk->ljgpkh', w, eye)
        return out.reshape(depth, S5_BLOCKS, gpb * S5_STATE, gpb * S5_GROUP).astype(BF16)

    return (a_re.reshape(depth, 1, S5_N), a_im.reshape(depth, 1, S5_N), pack_in(bb_re), pack_in(bb_im),
            pack_out(c_re.astype(F32)), pack_out(c_im.astype(F32)))


def kernel(x_prompt, x_sample, state_hgrn, state_s5_re, state_s5_im, norm_g, ffn_w_gate, ffn_w_up, ffn_w_down, w_in, hgrn_lb_logits, hgrn_norm_g, gmlp_ws, gmlp_bs, gmlp_norm_g, gmlp_norm_b, s5_lam_re, s5_lam_im, s5_log_dt, s5_b_re, s5_b_im, s5_c_re, s5_c_im, s5_d, s5_glu_w, s5_glu_b, w_branch, w_out):
    nb, seq, _ = x_prompt.shape
    ns = x_sample.shape[0]
    depth = norm_g.shape[0]
    tm = 1024
    tl = 128
    tg = 1024
    tt = 128
    bb = 16
    nt = seq // tm
    n_p = nb * nt

    def bmajor(i):
        return (jnp.minimum(i, n_p - 1), 0)

    def tmajor(i):
        j = jnp.minimum(i, n_p - 1)
        return (j % nt, j // nt)

    def ffn_riders(l, which):
        return [_rider(ffn_w_gate, (l, which)), _rider(ffn_w_up, (l, which)),
                _rider(ffn_w_down, (l, which))]

    def mixer_riders(l):
        in_splits = ((0, COL_GM), (COL_GM, COL_S5), (COL_S5, COL_GATE), (COL_GATE, N_IN))
        return ([_rider(w_in, (l,), in_splits)]
                + [_rider(w_branch, (l, n)) for n in range(N_BRANCH)]
                + [_rider(w_out, (l,)), _rider(s5_glu_w, (l,))])

    n_norm = norm_g.shape[1]
    g_all = norm_g.reshape(depth * n_norm, 1, D_MODEL)
    ng_all = hgrn_norm_g.reshape(depth, 1, HG_WIDTH)
    lng_all = gmlp_norm_g.reshape(depth, 1, GM_WIDTH)
    lnb_all = gmlp_norm_b.reshape(depth, 1, GM_WIDTH)
    tril = jnp.tril(jnp.ones((GM_CHUNK, GM_CHUNK), F32))
    wm_all = (gmlp_ws * tril).astype(BF16)
    bst_all = gmlp_bs.transpose(0, 2, 1)
    gc = GM_WIDTH // GM_GROUPS
    w00_all = jnp.repeat(gmlp_ws[:, :, 0, 0], gc, axis=1).reshape(depth, 1, GM_WIDTH)
    b00_all = jnp.repeat(gmlp_bs[:, :, 0], gc, axis=1).reshape(depth, 1, GM_WIDTH)
    s5p_all = _s5_params(s5_lam_re, s5_lam_im, s5_log_dt, s5_b_re, s5_b_im, s5_c_re, s5_c_im)
    s5d_all = s5_d.reshape(depth, 1, S5_WIDTH)
    gb_all = s5_glu_b.reshape(depth, 1, 2 * S5_WIDTH)
    logits = hgrn_lb_logits.astype(F32)

    yp = x_prompt.reshape(nb * seq, D_MODEL)
    ys = x_sample.reshape(ns, D_MODEL)
    ffn_w = _cast_call(ffn_riders(0, 0), n_p)

    outs = {k: [] for k in ('hg_p', 're_p', 'im_p', 're_s', 'im_s', 'v_s')}
    hg_s = None
    for l in range(depth):
        g = [_Row(g_all, l * n_norm + i) for i in range(n_norm)]
        ng, lng, lnb = _Row(ng_all, l), _Row(lng_all, l), _Row(lnb_all, l)
        s5p = [_Row(p, l) for p in s5p_all]
        s5d, gb = _Row(s5d_all, l), _Row(gb_all, l)
        tm_shape = (seq, nb * D_MODEL)

        x1, xs1, h1, hs1, w_hg, w_gm, w_s5, w_gate, wb0, wb1, wb2, wout, gw = _ffn_call(
            yp, ys, g[0], g[1], g[2], *ffn_w, mixer_riders(l), emit_h=True, n_p=n_p, tm=tm,
            in_map=bmajor if l == 0 else tmajor, out_map=tmajor, out_shape=tm_shape)
        a_p, s_hg = _hgrn_prompt_call(h1, w_hg, logits, ng, layer=l, nb=nb, seq=seq, tl=tl)
        b_p = _gmlp_prompt_call(h1, w_gm, lng, lnb, _Row(wm_all, l), _Row(bst_all, l),
                                nb=nb, seq=seq, tg=tg)
        c_p, h_re, h_im = _s5_call(h1, w_s5, s5p, s5d, gw, gb, nb=nb, seq=seq, tt=tt)
        outs['hg_p'].append(s_hg)
        outs['re_p'].append(h_re.reshape(nb, S5_GROUPS, S5_STATE))
        outs['im_p'].append(h_im.reshape(nb, S5_GROUPS, S5_STATE))

        q, f, kt, v, xg, b_s, v_rows = _sample_proj_call(
            hs1, w_hg, w_gm, logits, lng, lnb, _Row(w00_all, l), _Row(b00_all, l), layer=l, n=ns)
        fc = f.reshape(ns // bb, bb, HG_HEADS, HG_DK).transpose(0, 2, 3, 1)
        hg_s, a_s = _hgrn_step_call(state_hgrn, hg_s, fc, kt, q, v, xg, ng, layer=l, n=ns, bb=bb)
        h0 = (state_s5_re[l].reshape(ns, S5_N), state_s5_im[l].reshape(ns, S5_N))
        c_s, hs_re, hs_im = _s5_call(hs1, w_s5, s5p, s5d, gw, gb, h0, nb=ns, seq=1, tt=1)

        x2, xs2, *ffn_w = _merge_call(x1, h1, a_p, b_p, c_p, xs1, hs1, a_s, b_s, c_s, g[3],
                                      w_gate, (wb0, wb1, wb2), wout, ffn_riders(l, 1),
                                      n_p=n_p, tm=tm, xmap=tmajor, out_shape=tm_shape)
        last = l == depth - 1
        yp, ys, *ffn_w = _ffn_call(x2, xs2, g[4], g[5], g[5], *ffn_w,
                                   [] if last else ffn_riders(l + 1, 0), emit_h=False, n_p=n_p,
                                   tm=tm, in_map=tmajor, out_map=bmajor if last else tmajor,
                                   out_shape=(nb * seq, D_MODEL) if last else tm_shape)
        outs['re_s'].append(hs_re.reshape(ns, S5_GROUPS, S5_STATE))
        outs['im_s'].append(hs_im.reshape(ns, S5_GROUPS, S5_STATE))
        outs['v_s'].append(v_rows.reshape(ns, 1, GM_WIDTH))

    return (yp.reshape(nb, seq, D_MODEL), ys.reshape(ns, 1, D_MODEL),
            jnp.stack(outs['hg_p']), jnp.stack(outs['re_p']), jnp.stack(outs['im_p']),
            hg_s, jnp.stack(outs['re_s']), jnp.stack(outs['im_s']),
            jnp.stack(outs['v_s']))
```

```python
import functools
import math

import jax
import jax.numpy as jnp
from jax import lax
from jax.experimental import pallas as pl
from jax.experimental.pallas import tpu as pltpu

F32 = jnp.float32
BF16 = jnp.bfloat16

D_MODEL = 1024
D_FF = 2816
HG_HEADS = 4
HG_DK = 128
HG_WIDTH = HG_HEADS * HG_DK
GM_GROUPS = 4
GM_CHUNK = 128
GM_WIDTH = 512
S5_WIDTH = 512
S5_GROUP = 16
S5_GROUPS = S5_WIDTH // S5_GROUP
S5_STATE = 64
S5_N = S5_GROUPS * S5_STATE
S5_BLOCKS = 4
N_BRANCH = 3
EPS = 1e-6

COL_GM = 4 * HG_WIDTH
COL_S5 = COL_GM + 2 * GM_WIDTH
COL_GATE = COL_S5 + S5_WIDTH
N_IN = COL_GATE + N_BRANCH * D_MODEL

GLA_CHUNK = 32
FF_CHUNK = 512
TILE_PIECES = 2
GM_PIECES = 4
HG_PIECES = 2
VMEM_LIMIT = 56 * 1024 * 1024


def _rms(x, g):
    return x * lax.rsqrt(jnp.mean(x * x, axis=-1, keepdims=True) + EPS) * g


def _gelu(x):
    c = math.sqrt(2.0 / math.pi)
    hx = 0.5 * x
    return hx + hx * jnp.tanh(x * (c + (c * 0.044715) * (x * x)))


def _sigmoid(x):
    return 0.5 + 0.5 * jnp.tanh(0.5 * x)


def _dot(a, b):
    return jnp.dot(a, b, preferred_element_type=F32)


def _dot_nt(a, b):
    return lax.dot_general(a, b, (((1,), (1,)), ((), ())), preferred_element_type=F32)


def _dot_tn(a, b):
    return lax.dot_general(a, b, (((0,), (0,)), ((), ())), preferred_element_type=F32)


def _const_spec(shape):
    n = len(shape)
    return pl.BlockSpec(shape, lambda *_: (0,) * n, pipeline_mode=pl.Buffered(1))


class _Row:
    def __init__(self, arr, index):
        self.arr, self.index = arr, index

    @property
    def spec(self):
        tail = self.arr.shape[1:]
        index = self.index
        return pl.BlockSpec((None,) + tail, lambda *_: (index,) + (0,) * len(tail),
                            pipeline_mode=pl.Buffered(1))


def _params(n_grid):
    return pltpu.CompilerParams(
        dimension_semantics=("arbitrary",) * n_grid, vmem_limit_bytes=VMEM_LIMIT)


def _rider(src, lead=(), splits=None):
    k, n = src.shape[-2:]
    return dict(src=src, lead=tuple(lead), k=k, n=n, splits=tuple(splits or ((0, n),)))


def _rider_specs(riders, n_chunks):
    in_specs, out_specs, out_shapes = [], [], []
    for r in riders:
        rows = r['k'] // n_chunks
        assert rows * n_chunks == r['k'] and rows % 16 == 0, (r['k'], n_chunks)

        def src_map(i, lead=r['lead']):
            return lead + (jnp.minimum(i, n_chunks - 1), 0)

        in_specs.append(pl.BlockSpec((None,) * len(r['lead']) + (rows, r['n']), src_map))
        for c0, c1 in r['splits']:
            out_specs.append(pl.BlockSpec((rows, c1 - c0), lambda i: (jnp.minimum(i, n_chunks - 1), 0)))
            out_shapes.append(jax.ShapeDtypeStruct((r['k'], c1 - c0), BF16))
    return in_specs, out_specs, out_shapes


def _run_riders(src_refs, out_refs, splits):
    k = 0
    for src, sp in zip(src_refs, splits):
        for c0, c1 in sp:
            out_refs[k][...] = src[:, c0:c1].astype(BF16)
            k += 1


def _cast_kernel(splits, *refs):
    n = len(splits)
    _run_riders(refs[:n], refs[n:], splits)


def _cast_call(riders, n_chunks):
    in_specs, out_specs, out_shapes = _rider_specs(riders, n_chunks)
    return pl.pallas_call(
        functools.partial(_cast_kernel, tuple(r['splits'] for r in riders)),
        grid=(n_chunks,),
        in_specs=in_specs,
        out_specs=out_specs,
        out_shape=out_shapes,
        compiler_params=_params(1),
        name="cast",
    )(*[r['src'] for r in riders])


def _staggered(n_pieces, n_stages, begin, stage, end):
    for step in range(n_stages + n_pieces - 1):
        for p in range(n_pieces):
            s = step - p
            if s == 0:
                begin(p)
            if 0 <= s < n_stages:
                stage(p, s)
            if s == n_stages - 1:
                end(p)


def _ffn_tile(src_ref, dst_ref, h_ref, g_pre, g_post, g_next, wg_ref, wu_ref, wd_ref, n_pieces):
    rows = src_ref.shape[0] // n_pieces
    chunks = [(c0, min(c0 + FF_CHUNK, D_FF)) for c0 in range(0, D_FF, FF_CHUNK)]
    hs = [None] * n_pieces
    ds = [None] * n_pieces

    def begin(p):
        hs[p] = _rms(src_ref[p * rows:(p + 1) * rows, :], g_pre).astype(BF16)

    def stage(p, s):
        c0, c1 = chunks[s]
        a = _dot(hs[p], wg_ref[:, c0:c1])
        u = _dot(hs[p], wu_ref[:, c0:c1])
        m = (a * _sigmoid(a) * u).astype(BF16)
        part = _dot(m, wd_ref[c0:c1, :])
        ds[p] = part if ds[p] is None else ds[p] + part

    def end(p):
        r = slice(p * rows, (p + 1) * rows)
        y = src_ref[r, :] + 0.5 * _rms(ds[p], g_post)
        dst_ref[r, :] = y
        if h_ref is not None:
            h_ref[r, :] = _rms(y, g_next).astype(BF16)

    _staggered(n_pieces, len(chunks), begin, stage, end)


def _ffn_kernel(n_p, emit_h, splits, *refs):
    x_ref, xs_ref, gpre_ref, gpost_ref, gnext_ref, wg, wu, wd = refs[:8]
    n_r = len(splits)
    rider_src = refs[8:8 + n_r]
    n_out = 4 if emit_h else 2
    outs = refs[8 + n_r:8 + n_r + n_out]
    rider_out = refs[8 + n_r + n_out:]
    i = pl.program_id(0)
    _run_riders(rider_src, rider_out, splits)

    def run(src_ref, dst_ref, h_ref, n_pieces):
        _ffn_tile(src_ref, dst_ref, h_ref, gpre_ref[...], gpost_ref[...], gnext_ref[...],
                  wg, wu, wd, n_pieces)

    @pl.when(i < n_p)
    def _():
        run(x_ref, outs[0], outs[2] if emit_h else None, TILE_PIECES)

    @pl.when(i == n_p)
    def _():
        run(xs_ref, outs[1], outs[3] if emit_h else None, 1)


def _ffn_call(x2d, xs, g_pre, g_post, g_next, wg, wu, wd, riders, *, emit_h, n_p, tm, in_map,
              out_map, out_shape):
    ns = xs.shape[0]
    r_in, r_out, r_shapes = _rider_specs(riders, n_p)
    out_specs = [pl.BlockSpec((tm, D_MODEL), out_map), pl.BlockSpec((ns, D_MODEL), lambda i: (0, 0))]
    out_shapes = [jax.ShapeDtypeStruct(out_shape, F32), jax.ShapeDtypeStruct((ns, D_MODEL), F32)]
    if emit_h:
        out_specs = out_specs + out_specs
        out_shapes += [jax.ShapeDtypeStruct(out_shape, BF16), jax.ShapeDtypeStruct((ns, D_MODEL), BF16)]
    return pl.pallas_call(
        functools.partial(_ffn_kernel, n_p, emit_h, tuple(r['splits'] for r in riders)),
        grid=(n_p + 1,),
        in_specs=[
            pl.BlockSpec((tm, D_MODEL), in_map),
            _const_spec((ns, D_MODEL)),
            g_pre.spec, g_post.spec, g_next.spec,
            _const_spec((D_MODEL, D_FF)),
            _const_spec((D_MODEL, D_FF)),
            _const_spec((D_FF, D_MODEL)),
        ] + r_in,
        out_specs=out_specs + r_out,
        out_shape=out_shapes + r_shapes,
        compiler_params=_params(1),
        name="ffn",
    )(x2d, xs, g_pre.arr, g_post.arr, g_next.arr, wg, wu, wd, *[r['src'] for r in riders])


def _hgrn_gates(z, lb):
    ls = jnp.minimum(z, 0.0) - jnp.log1p(jnp.exp(-jnp.abs(z)))
    c = jnp.log1p(-lb) + ls
    a = jnp.log(jnp.maximum(lb, 1e-37))
    lae = jnp.maximum(a, c) + jnp.log1p(jnp.exp(-jnp.abs(a - c)))
    logf = jnp.where(lb > 0.0, lae, c)
    k = (1.0 - lb) * _sigmoid(-z)
    return logf, k


def _hgrn_out_head(oh, xg, ng):
    on = oh * lax.rsqrt(jnp.mean(oh * oh, axis=-1, keepdims=True) + EPS)
    return on * ng * (xg * _sigmoid(xg))


def _hgrn_out(o, xg, ng):
    heads = [slice(hd * HG_DK, (hd + 1) * HG_DK) for hd in range(HG_HEADS)]
    return jnp.concatenate([_hgrn_out_head(o[:, cs], xg[:, cs], ng[:, cs]) for cs in heads], axis=-1)


def _lower_bound(logits_ref, layer):
    lg = logits_ref[...]
    m = jnp.max(lg, axis=0, keepdims=True)
    e = jnp.exp(lg - m)
    den = jnp.sum(e, axis=0, keepdims=True)
    lb = jnp.zeros_like(den)
    for j in range(1, layer + 1):
        lb = lb + e[j:j + 1] / den
    return lb


def _hgrn_prompt_kernel(layer, nb, tl, h_ref, w_ref, logits_ref, ng_ref, tri_ref,
                        a_ref, sfin_ref, st_ref, q_s, k_s, v_s, b_s, g_s, o_s):
    t = pl.program_id(0)
    nt = pl.num_programs(0)

    @pl.when(t == 0)
    def _():
        st_ref[...] = jnp.zeros_like(st_ref)

    lb = _lower_bound(logits_ref, layer)
    tri = tri_ref[...]
    tr = tri.shape[0]
    n_pieces = HG_PIECES
    seqs = nb // n_pieces
    rows = seqs * tl
    proj = [None] * n_pieces

    def stage(p, s):
        if s == 0:
            h = jnp.concatenate([h_ref[:, b * D_MODEL:(b + 1) * D_MODEL]
                                 for b in range(p * seqs, (p + 1) * seqs)], axis=0)
            proj[p] = _dot(h, w_ref[...])
            return
        pp = proj[p]
        logf, kk = _hgrn_gates(pp[:, 1 * HG_WIDTH:2 * HG_WIDTH], lb)
        l0 = logf.astype(BF16)
        r1 = logf - l0.astype(F32)
        l1 = r1.astype(BF16)
        l2 = (r1 - l1.astype(F32)).astype(BF16)
        for i in range(rows // tr):
            rs = slice(i * tr, (i + 1) * tr)
            dst = slice(p * rows + i * tr, p * rows + (i + 1) * tr)
            b_s[dst, :] = _dot(tri, l0[rs]) + _dot(tri, l1[rs]) + _dot(tri, l2[rs])
        whole = slice(p * rows, (p + 1) * rows)
        q_s[whole, :] = pp[:, 0 * HG_WIDTH:1 * HG_WIDTH]
        k_s[whole, :] = kk
        v_s[whole, :] = pp[:, 2 * HG_WIDTH:3 * HG_WIDTH]
        g_s[whole, :] = pp[:, 3 * HG_WIDTH:4 * HG_WIDTH]

    _staggered(n_pieces, 2, lambda p: None, stage, lambda p: None)

    c = GLA_CHUNK
    mid = c // 2 - 1
    row = lax.broadcasted_iota(jnp.int32, (c, c), 0)
    col = lax.broadcasted_iota(jnp.int32, (c, c), 1)
    causal = col <= row
    ng = ng_ref[...]

    def chunk(ci, carry):
        pending = []
        for b in range(nb):
            r = pl.ds(pl.multiple_of(b * tl + ci * c, c), c)
            for hd in range(HG_HEADS):
                cs = slice(hd * HG_DK, (hd + 1) * HG_DK)
                bq = b_s[r, cs]
                qc = q_s[r, cs]
                kc = k_s[r, cs]
                vc = v_s[r, cs].astype(BF16)
                b_mid = bq[mid:mid + 1]
                b_last = bq[c - 1:c]
                st = st_ref[b, hd]
                q_in = (qc * jnp.exp(bq)).astype(BF16)
                q_md = (qc * jnp.exp(bq - b_mid)).astype(BF16)
                k_md = (kc * jnp.exp(b_mid - bq)).astype(BF16)
                k_ls = (kc * jnp.exp(b_last - bq)).astype(BF16)
                scores = _dot_nt(q_md, k_md)
                o_inter = _dot_nt(q_in, st.astype(BF16))
                st_ref[b, hd] = st * jnp.exp(b_last) + _dot_tn(vc, k_ls)
                pending.append((r, cs, scores, o_inter, vc))
        for r, cs, scores, o_inter, vc in pending:
            sc = jnp.where(causal, scores, 0.0).astype(BF16)
            o = o_inter + _dot(sc, vc)
            o_s[r, cs] = _hgrn_out_head(o, g_s[r, cs], ng[:, cs]).astype(BF16)
        return carry

    lax.fori_loop(0, tl // c, chunk, 0)

    for b in range(nb):
        a_ref[:, b * HG_WIDTH:(b + 1) * HG_WIDTH] = o_s[b * tl:(b + 1) * tl, :]

    @pl.when(t == nt - 1)
    def _():
        for b in range(nb):
            for hd in range(HG_HEADS):
                sfin_ref[b, hd] = st_ref[b, hd].T


def _hgrn_prompt_call(h_tm, w_hg, logits, ng, *, layer, nb, seq, tl):
    c = GLA_CHUNK
    tr = min(256, nb * tl)
    idx = jnp.arange(tr)
    tri = ((idx[:, None] >= idx[None, :]) & (idx[:, None] // c == idx[None, :] // c)).astype(BF16)
    depth = logits.shape[0]
    rows = nb * tl
    return pl.pallas_call(
        functools.partial(_hgrn_prompt_kernel, layer, nb, tl),
        grid=(seq // tl,),
        in_specs=[
            pl.BlockSpec((tl, nb * D_MODEL), lambda t: (t, 0)),
            _const_spec((D_MODEL, 4 * HG_WIDTH)),
            _const_spec((depth, HG_WIDTH)),
            ng.spec,
            _const_spec((tr, tr)),
        ],
        out_specs=[
            pl.BlockSpec((tl, nb * HG_WIDTH), lambda t: (t, 0)),
            pl.BlockSpec((nb, HG_HEADS, HG_DK, HG_DK), lambda t: (0, 0, 0, 0)),
        ],
        out_shape=[
            jax.ShapeDtypeStruct((seq, nb * HG_WIDTH), BF16),
            jax.ShapeDtypeStruct((nb, HG_HEADS, HG_DK, HG_DK), F32),
        ],
        scratch_shapes=[
            pltpu.VMEM((nb, HG_HEADS, HG_DK, HG_DK), F32),
            pltpu.VMEM((rows, HG_WIDTH), F32),
            pltpu.VMEM((rows, HG_WIDTH), F32),
            pltpu.VMEM((rows, HG_WIDTH), F32),
            pltpu.VMEM((rows, HG_WIDTH), F32),
            pltpu.VMEM((rows, HG_WIDTH), F32),
            pltpu.VMEM((rows, HG_WIDTH), BF16),
        ],
        compiler_params=_params(1),
        name="hgrn_prompt",
    )(h_tm, w_hg, logits, ng.arr, tri)


def _gmlp_act(p, lng, lnb):
    u = _gelu(p[:, :GM_WIDTH])
    gv = _gelu(p[:, GM_WIDTH:])
    mu = jnp.mean(gv, axis=-1, keepdims=True)
    dv = gv - mu
    var = jnp.mean(dv * dv, axis=-1, keepdims=True)
    v = dv * lax.rsqrt(var + EPS) * lng + lnb
    return u, v


def _gmlp_uv(h, w_ref, lng, lnb):
    return _gmlp_act(_dot(h, w_ref[...]), lng, lnb)


def _gmlp_prompt_kernel(tg, h_ref, w_ref, lng_ref, lnb_ref, wm_ref, bst_ref, o_ref):
    n_pieces = GM_PIECES
    rows = tg // n_pieces
    gc = GM_WIDTH // GM_GROUPS
    proj = [None] * n_pieces
    uv = [None] * n_pieces

    def stage(p, s):
        r = slice(p * rows, (p + 1) * rows)
        if s == 0:
            proj[p] = _dot(h_ref[r, :], w_ref[...])
        elif s == 1:
            uv[p] = _gmlp_act(proj[p], lng_ref[...], lnb_ref[...])
        else:
            u, v = uv[p]
            vb = v.astype(BF16)
            chunks = []
            for j in range(rows // GM_CHUNK):
                parts = []
                for gi in range(GM_GROUPS):
                    vj = vb[j * GM_CHUNK:(j + 1) * GM_CHUNK, gi * gc:(gi + 1) * gc]
                    parts.append(_dot(wm_ref[gi], vj) + bst_ref[:, gi:gi + 1])
                chunks.append(jnp.concatenate(parts, axis=-1))
            mix = jnp.concatenate(chunks, axis=0)
            o_ref[r, :] = (u * mix).astype(BF16)

    _staggered(n_pieces, 3, lambda p: None, stage, lambda p: None)


def _gmlp_prompt_call(h_tm, w_gm, lng, lnb, wm, bst, *, nb, seq, tg):
    return pl.pallas_call(
        functools.partial(_gmlp_prompt_kernel, tg),
        grid=(nb, seq // tg),
        in_specs=[
            pl.BlockSpec((tg, D_MODEL), lambda b, t: (t, b)),
            _const_spec((D_MODEL, 2 * GM_WIDTH)),
            lng.spec, lnb.spec, wm.spec, bst.spec,
        ],
        out_specs=pl.BlockSpec((tg, GM_WIDTH), lambda b, t: (t, b)),
        out_shape=jax.ShapeDtypeStruct((seq, nb * GM_WIDTH), BF16),
        compiler_params=_params(2),
        name="gmlp_prompt",
    )(h_tm, w_gm, lng.arr, lnb.arr, wm.arr, bst.arr)


def _s5_kernel(nb, tt, has_h0, relayout, *refs):
    (x_ref, w_ref, are_ref, aim_ref, bre_ref, bim_ref, cre_ref, cim_ref, d_ref,
     gw_ref, gb_ref) = refs[:11]
    h0re_ref, h0im_ref = refs[11:13] if has_h0 else (None, None)
    c_ref, hre_ref, him_ref, sre, sim, lay = refs[-6:]
    t = pl.program_id(0)
    wblk = S5_WIDTH // S5_BLOCKS
    sblk = S5_N // S5_BLOCKS

    @pl.when(t == 0)
    def _():
        if has_h0:
            hre_ref[...] = h0re_ref[...]
            him_ref[...] = h0im_ref[...]
        else:
            hre_ref[...] = jnp.zeros_like(hre_ref)
            him_ref[...] = jnp.zeros_like(him_ref)

    if relayout:
        h = jnp.concatenate([x_ref[:, b * D_MODEL:(b + 1) * D_MODEL] for b in range(nb)], axis=0)
    else:
        h = x_ref[...]
    su = _dot(h, w_ref[...])
    if relayout:
        for b in range(nb):
            for j in range(S5_BLOCKS):
                lay[j, pl.ds(b, tt, stride=nb), :] = su[b * tt:(b + 1) * tt, j * wblk:(j + 1) * wblk]
        su = jnp.concatenate([lay[j] for j in range(S5_BLOCKS)], axis=-1)
    sub = su.astype(BF16)

    def project(j):
        uj = sub[:, j * wblk:(j + 1) * wblk]
        sre[:, j * sblk:(j + 1) * sblk] = _dot(uj, bre_ref[j])
        sim[:, j * sblk:(j + 1) * sblk] = _dot(uj, bim_ref[j])

    def scan(j):
        cols = slice(j * sblk, (j + 1) * sblk)
        a_re = jnp.broadcast_to(are_ref[:, cols], (nb, sblk))
        a_im = jnp.broadcast_to(aim_ref[:, cols], (nb, sblk))
        h_re = hre_ref[:, cols]
        h_im = him_ref[:, cols]
        for i in range(tt):
            r = slice(i * nb, (i + 1) * nb)
            n_re = a_re * h_re - a_im * h_im + sre[r, cols]
            n_im = a_re * h_im + a_im * h_re + sim[r, cols]
            sre[r, cols] = n_re
            sim[r, cols] = n_im
            h_re, h_im = n_re, n_im
        hre_ref[:, cols] = h_re
        him_ref[:, cols] = h_im

    def readout(j):
        hr = sre[:, j * sblk:(j + 1) * sblk].astype(BF16)
        hi = sim[:, j * sblk:(j + 1) * sblk].astype(BF16)
        return _dot(hr, cre_ref[j]) - _dot(hi, cim_ref[j])

    ys = []
    project(0)
    for j in range(S5_BLOCKS):
        if j + 1 < S5_BLOCKS:
            project(j + 1)
        scan(j)
        ys.append(readout(j))
    y = jnp.concatenate(ys, axis=-1) + d_ref[...] * su
    zz = _dot(_gelu(y).astype(BF16), gw_ref[...]) + gb_ref[...]
    c = zz[:, :S5_WIDTH] * _sigmoid(zz[:, S5_WIDTH:])
    if relayout:
        for j in range(S5_BLOCKS):
            lay[j] = c[:, j * wblk:(j + 1) * wblk]
        for b in range(nb):
            cb = jnp.concatenate([lay[j, pl.ds(b, tt, stride=nb), :] for j in range(S5_BLOCKS)], axis=-1)
            c_ref[:, b * S5_WIDTH:(b + 1) * S5_WIDTH] = cb.astype(BF16)
    else:
        c_ref[...] = c.astype(BF16)


def _s5_call(h2d, w_s5, s5p, d, gw, gb, h0=None, *, nb, seq, tt):
    rows = tt * nb
    relayout = seq > 1
    wblk = S5_WIDTH // S5_BLOCKS
    sblk = S5_N // S5_BLOCKS
    x_block = (tt, nb * D_MODEL) if relayout else (rows, D_MODEL)
    c_block = (tt, nb * S5_WIDTH) if relayout else (rows, S5_WIDTH)
    in_specs = [
        pl.BlockSpec(x_block, lambda t: (t, 0)),
        _const_spec((D_MODEL, S5_WIDTH)),
    ] + [p.spec for p in s5p] + [
        d.spec,
        _const_spec((S5_WIDTH, 2 * S5_WIDTH)),
        gb.spec,
    ]
    args = [h2d, w_s5] + [p.arr for p in s5p] + [d.arr, gw, gb.arr]
    if h0 is not None:
        in_specs += [_const_spec((nb, S5_N)), _const_spec((nb, S5_N))]
        args += list(h0)
    state_spec = pl.BlockSpec((nb, S5_N), lambda t: (0, 0))
    c_shape = (seq, nb * S5_WIDTH) if relayout else (nb, S5_WIDTH)
    return pl.pallas_call(
        functools.partial(_s5_kernel, nb, tt, h0 is not None, relayout),
        grid=(seq // tt,),
        in_specs=in_specs,
        out_specs=[pl.BlockSpec(c_block, lambda t: (t, 0)), state_spec, state_spec],
        out_shape=[
            jax.ShapeDtypeStruct(c_shape, BF16),
            jax.ShapeDtypeStruct((nb, S5_N), F32),
            jax.ShapeDtypeStruct((nb, S5_N), F32),
        ],
        scratch_shapes=[pltpu.VMEM((rows, S5_N), F32), pltpu.VMEM((rows, S5_N), F32),
                        pltpu.VMEM((S5_BLOCKS, rows, wblk), F32)],
        compiler_params=_params(1),
        name="s5",
    )(*args)


def _merge_tile(x_ref, h_ref, br_refs, dst_ref, g3, wgate, wbr, wout, n_pieces):
    rows = x_ref.shape[0] // n_pieces
    merged = [None] * n_pieces
    proj = [None] * n_pieces

    def begin(p):
        pass

    def stage(p, s):
        r = slice(p * rows, (p + 1) * rows)
        if s < N_BRANCH:
            gate = _sigmoid(_dot(h_ref[r, :], wgate[:, s * D_MODEL:(s + 1) * D_MODEL]))
            term = gate * _dot(br_refs[s][r, :], wbr[s][...])
            merged[p] = term if merged[p] is None else merged[p] + term
        else:
            proj[p] = _dot(merged[p].astype(BF16), wout[...])

    def end(p):
        r = slice(p * rows, (p + 1) * rows)
        dst_ref[r, :] = x_ref[r, :] + _rms(proj[p], g3)

    _staggered(n_pieces, N_BRANCH + 1, begin, stage, end)


def _merge_kernel(n_p, splits, *refs):
    (x_ref, h_ref, a_ref, b_ref, c_ref, xs_ref, hs_ref, as_ref, bs_ref, cs_ref, g3_ref,
     wgate, wb0, wb1, wb2, wout) = refs[:16]
    n_r = len(splits)
    rider_src = refs[16:16 + n_r]
    o_ref, os_ref = refs[16 + n_r:18 + n_r]
    rider_out = refs[18 + n_r:]
    wbr = (wb0, wb1, wb2)
    i = pl.program_id(0)
    _run_riders(rider_src, rider_out, splits)

    @pl.when(i < n_p)
    def _():
        _merge_tile(x_ref, h_ref, (a_ref, b_ref, c_ref), o_ref, g3_ref[...], wgate, wbr, wout,
                    TILE_PIECES)

    @pl.when(i == n_p)
    def _():
        _merge_tile(xs_ref, hs_ref, (as_ref, bs_ref, cs_ref), os_ref, g3_ref[...], wgate, wbr, wout, 1)


def _merge_call(x2d, h2d, a, b, c, xs, hs, a_s, b_s, c_s, g3, wgate, wbr, wout, riders,
                *, n_p, tm, xmap, out_shape):
    ns = xs.shape[0]
    r_in, r_out, r_shapes = _rider_specs(riders, n_p)
    return pl.pallas_call(
        functools.partial(_merge_kernel, n_p, tuple(r['splits'] for r in riders)),
        grid=(n_p + 1,),
        in_specs=[
            pl.BlockSpec((tm, D_MODEL), xmap),
            pl.BlockSpec((tm, D_MODEL), xmap),
            pl.BlockSpec((tm, HG_WIDTH), xmap),
            pl.BlockSpec((tm, GM_WIDTH), xmap),
            pl.BlockSpec((tm, S5_WIDTH), xmap),
            _const_spec((ns, D_MODEL)),
            _const_spec((ns, D_MODEL)),
            _const_spec((ns, HG_WIDTH)),
            _const_spec((ns, GM_WIDTH)),
            _const_spec((ns, S5_WIDTH)),
            g3.spec,
            _const_spec((D_MODEL, N_BRANCH * D_MODEL)),
            _const_spec((HG_WIDTH, D_MODEL)),
            _const_spec((GM_WIDTH, D_MODEL)),
            _const_spec((S5_WIDTH, D_MODEL)),
            _const_spec((D_MODEL, D_MODEL)),
        ] + r_in,
        out_specs=[pl.BlockSpec((tm, D_MODEL), xmap),
                   pl.BlockSpec((ns, D_MODEL), lambda i: (0, 0))] + r_out,
        out_shape=[jax.ShapeDtypeStruct(out_shape, F32),
                   jax.ShapeDtypeStruct((ns, D_MODEL), F32)] + r_shapes,
        compiler_params=_params(1),
        name="merge",
    )(x2d, h2d, a, b, c, xs, hs, a_s, b_s, c_s, g3.arr, wgate, *wbr, wout, *[r['src'] for r in riders])


def _sample_proj_kernel(layer, h_ref, whg_ref, wgm_ref, logits_ref, lng_ref, lnb_ref,
                        w00_ref, b00_ref, q_ref, f_ref, kt_ref, v_ref, xg_ref, bout_ref, vrow_ref):
    h = h_ref[...]
    p = _dot(h, whg_ref[...])
    z = p[:, 1 * HG_WIDTH:2 * HG_WIDTH]
    logf, kk = _hgrn_gates(z, _lower_bound(logits_ref, layer))
    q_ref[...] = p[:, 0 * HG_WIDTH:1 * HG_WIDTH]
    f_ref[...] = jnp.exp(logf)
    for hd in range(HG_HEADS):
        kt_ref[hd] = kk[:, hd * HG_DK:(hd + 1) * HG_DK].T.astype(BF16)
    v_ref[...] = p[:, 2 * HG_WIDTH:3 * HG_WIDTH]
    xg_ref[...] = p[:, 3 * HG_WIDTH:4 * HG_WIDTH]
    u, v = _gmlp_uv(h, wgm_ref, lng_ref[...], lnb_ref[...])
    bout_ref[...] = (u * (v * w00_ref[...] + b00_ref[...])).astype(BF16)
    vrow_ref[...] = v


def _sample_proj_call(h, w_hg, w_gm, logits, lng, lnb, w00, b00, *, layer, n):
    depth = logits.shape[0]
    assert n == HG_DK, "per-head key transpose is written for a square (sequences, keys) tile"
    f32_out = jax.ShapeDtypeStruct((n, HG_WIDTH), F32)
    row_spec = pl.BlockSpec((n, HG_WIDTH), lambda i: (0, 0))
    return pl.pallas_call(
        functools.partial(_sample_proj_kernel, layer),
        grid=(1,),
        in_specs=[
            _const_spec((n, D_MODEL)),
            _const_spec((D_MODEL, 4 * HG_WIDTH)),
            _const_spec((D_MODEL, 2 * GM_WIDTH)),
            _const_spec((depth, HG_WIDTH)),
            lng.spec, lnb.spec, w00.spec, b00.spec,
        ],
        out_specs=[row_spec, row_spec, pl.BlockSpec((HG_HEADS, HG_DK, n), lambda i: (0, 0, 0)),
                   row_spec, row_spec, row_spec, row_spec],
        out_shape=[f32_out, f32_out, jax.ShapeDtypeStruct((HG_HEADS, HG_DK, n), BF16),
                   f32_out, f32_out, jax.ShapeDtypeStruct((n, GM_WIDTH), BF16), f32_out],
        compiler_params=_params(1),
        name="sample_proj",
    )(h, w_hg, w_gm, logits, lng.arr, lnb.arr, w00.arr, b00.arr)


def _hgrn_step_kernel(bb, layer, whole_stack, *refs):
    s_ref, fc_ref, kt_ref, q_ref, v_ref, xg_ref, ng_ref = refs[:7]
    snew_ref, a_ref, o_s = refs[-3:]
    if whole_stack:
        for j in range(snew_ref.shape[0]):
            if j != layer:
                snew_ref[j] = jnp.zeros(snew_ref.shape[1:], F32)
        snew_ref = snew_ref.at[layer]
    n = kt_ref.shape[-1]
    seq_id = lax.broadcasted_iota(jnp.int32, (n, HG_DK), 0)
    first = pl.program_id(0) * bb
    for i in range(bb):
        here = seq_id == first + i
        for hd in range(HG_HEADS):
            cs = slice(hd * HG_DK, (hd + 1) * HG_DK)
            v_only = jnp.where(here, jnp.broadcast_to(v_ref[i:i + 1, cs], (n, HG_DK)), 0.0)
            kv = _dot(kt_ref[hd], v_only.astype(BF16))
            f_col = fc_ref[0, hd, :, i:i + 1]
            s_new = f_col * s_ref[i, hd] + kv
            snew_ref[i, hd] = s_new
            q8 = jnp.broadcast_to(q_ref[i:i + 1, cs], (8, HG_DK)).astype(BF16)
            o_s[i:i + 1, cs] = _dot(q8, s_new.astype(BF16))[0:1]
    a_ref[...] = _hgrn_out(o_s[...], xg_ref[...], ng_ref[...]).astype(BF16)


def _hgrn_step_call(s_all, prev, fc, kt, q, v, xg, ng, *, layer, n, bb):
    col_spec = pl.BlockSpec((1, HG_HEADS, HG_DK, bb), lambda i: (i, 0, 0, 0))
    row_spec = pl.BlockSpec((bb, HG_WIDTH), lambda i: (i, 0))
    st_spec = pl.BlockSpec((None, bb, HG_HEADS, HG_DK, HG_DK), lambda i: (layer, i, 0, 0, 0))
    in_specs = [st_spec, col_spec, _const_spec((HG_HEADS, HG_DK, n)), row_spec, row_spec, row_spec,
                ng.spec]
    args = [s_all, fc, kt, q, v, xg, ng.arr]
    aliases = {}
    out_st_spec = st_spec
    if prev is not None:
        in_specs.append(pl.BlockSpec(memory_space=pl.ANY))
        args.append(prev)
        aliases = {len(args) - 1: 0}
    else:
        out_st_spec = pl.BlockSpec((s_all.shape[0], bb, HG_HEADS, HG_DK, HG_DK), lambda i: (0, i, 0, 0, 0))
    return pl.pallas_call(
        functools.partial(_hgrn_step_kernel, bb, layer, prev is None),
        grid=(n // bb,),
        in_specs=in_specs,
        out_specs=[out_st_spec, row_spec],
        out_shape=[
            jax.ShapeDtypeStruct(s_all.shape, F32),
            jax.ShapeDtypeStruct((n, HG_WIDTH), BF16),
        ],
        scratch_shapes=[pltpu.VMEM((bb, HG_WIDTH), F32)],
        input_output_aliases=aliases,
        compiler_params=_params(1),
        name="hgrn_step",
    )(*args)


def _s5_params(lam_re, lam_im, log_dt, b_re, b_im, c_re, c_im):
    depth = lam_re.shape[0]
    lr = lam_re.astype(F32)
    li = lam_im.astype(F32)
    dt = jnp.exp(log_dt.astype(F32))[..., None]
    mag = jnp.exp(lr * dt)
    a_re = mag * jnp.cos(li * dt)
    a_im = mag * jnp.sin(li * dt)
    den = lr * lr + li * li
    k_re = ((a_re - 1.0) * lr + a_im * li) / den
    k_im = (a_im * lr - (a_re - 1.0) * li) / den
    bre = b_re.astype(F32)
    bim = b_im.astype(F32)
    bb_re = k_re[..., None] * bre - k_im[..., None] * bim
    bb_im = k_re[..., None] * bim + k_im[..., None] * bre
    gpb = S5_GROUPS // S5_BLOCKS
    eye = jnp.eye(gpb, dtype=F32)

    def pack_in(w):
        w = w.reshape(depth, S5_BLOCKS, gpb, S5_STATE, S5_GROUP)
        out = jnp.einsum('ljgph,gk->ljghkp', w, eye)
        return out.reshape(depth, S5_BLOCKS, gpb * S5_GROUP, gpb * S5_STATE).astype(BF16)

    def pack_out(w):
        w = w.reshape(depth, S5_BLOCKS, gpb, S5_GROUP, S5_STATE)
        out = jnp.einsum('ljghp,gk->ljgpkh', w, eye)
        return out.reshape(depth, S5_BLOCKS, gpb * S5_STATE, gpb * S5_GROUP).astype(BF16)

    return (a_re.reshape(depth, 1, S5_N), a_im.reshape(depth, 1, S5_N), pack_in(bb_re), pack_in(bb_im),
            pack_out(c_re.astype(F32)), pack_out(c_im.astype(F32)))


def kernel(x_prompt, x_sample, state_hgrn, state_s5_re, state_s5_im, norm_g, ffn_w_gate, ffn_w_up, ffn_w_down, w_in, hgrn_lb_logits, hgrn_norm_g, gmlp_ws, gmlp_bs, gmlp_norm_g, gmlp_norm_b, s5_lam_re, s5_lam_im, s5_log_dt, s5_b_re, s5_b_im, s5_c_re, s5_c_im, s5_d, s5_glu_w, s5_glu_b, w_branch, w_out):
    nb, seq, _ = x_prompt.shape
    ns = x_sample.shape[0]
    depth = norm_g.shape[0]
    tm = 1024
    tl = 128
    tg = 1024
    tt = 128
    bb = 16
    nt = seq // tm
    n_p = nb * nt

    def bmajor(i):
        return (jnp.minimum(i, n_p - 1), 0)

    def tmajor(i):
        j = jnp.minimum(i, n_p - 1)
        return (j % nt, j // nt)

    def ffn_riders(l, which):
        return [_rider(ffn_w_gate, (l, which)), _rider(ffn_w_up, (l, which)),
                _rider(ffn_w_down, (l, which))]

    def mixer_riders(l):
        in_splits = ((0, COL_GM), (COL_GM, COL_S5), (COL_S5, COL_GATE), (COL_GATE, N_IN))
        return ([_rider(w_in, (l,), in_splits)]
                + [_rider(w_branch, (l, n)) for n in range(N_BRANCH)]
                + [_rider(w_out, (l,)), _rider(s5_glu_w, (l,))])

    n_norm = norm_g.shape[1]
    g_all = norm_g.reshape(depth * n_norm, 1, D_MODEL)
    ng_all = hgrn_norm_g.reshape(depth, 1, HG_WIDTH)
    lng_all = gmlp_norm_g.reshape(depth, 1, GM_WIDTH)
    lnb_all = gmlp_norm_b.reshape(depth, 1, GM_WIDTH)
    tril = jnp.tril(jnp.ones((GM_CHUNK, GM_CHUNK), F32))
    wm_all = (gmlp_ws * tril).astype(BF16)
    bst_all = gmlp_bs.transpose(0, 2, 1)
    gc = GM_WIDTH // GM_GROUPS
    w00_all = jnp.repeat(gmlp_ws[:, :, 0, 0], gc, axis=1).reshape(depth, 1, GM_WIDTH)
    b00_all = jnp.repeat(gmlp_bs[:, :, 0], gc, axis=1).reshape(depth, 1, GM_WIDTH)
    s5p_all = _s5_params(s5_lam_re, s5_lam_im, s5_log_dt, s5_b_re, s5_b_im, s5_c_re, s5_c_im)
    s5d_all = s5_d.reshape(depth, 1, S5_WIDTH)
    gb_all = s5_glu_b.reshape(depth, 1, 2 * S5_WIDTH)
    logits = hgrn_lb_logits.astype(F32)

    yp = x_prompt.reshape(nb * seq, D_MODEL)
    ys = x_sample.reshape(ns, D_MODEL)
    ffn_w = _cast_call(ffn_riders(0, 0), n_p)

    outs = {k: [] for k in ('hg_p', 're_p', 'im_p', 're_s', 'im_s', 'v_s')}
    hg_s = None
    for l in range(depth):
        g = [_Row(g_all, l * n_norm + i) for i in range(n_norm)]
        ng, lng, lnb = _Row(ng_all, l), _Row(lng_all, l), _Row(lnb_all, l)
        s5p = [_Row(p, l) for p in s5p_all]
        s5d, gb = _Row(s5d_all, l), _Row(gb_all, l)
        tm_shape = (seq, nb * D_MODEL)

        x1, xs1, h1, hs1, w_hg, w_gm, w_s5, w_gate, wb0, wb1, wb2, wout, gw = _ffn_call(
            yp, ys, g[0], g[1], g[2], *ffn_w, mixer_riders(l), emit_h=True, n_p=n_p, tm=tm,
            in_map=bmajor if l == 0 else tmajor, out_map=tmajor, out_shape=tm_shape)
        a_p, s_hg = _hgrn_prompt_call(h1, w_hg, logits, ng, layer=l, nb=nb, seq=seq, tl=tl)
        b_p = _gmlp_prompt_call(h1, w_gm, lng, lnb, _Row(wm_all, l), _Row(bst_all, l),
                                nb=nb, seq=seq, tg=tg)
        c_p, h_re, h_im = _s5_call(h1, w_s5, s5p, s5d, gw, gb, nb=nb, seq=seq, tt=tt)
        outs['hg_p'].append(s_hg)
        outs['re_p'].append(h_re.reshape(nb, S5_GROUPS, S5_STATE))
        outs['im_p'].append(h_im.reshape(nb, S5_GROUPS, S5_STATE))

        q, f, kt, v, xg, b_s, v_rows = _sample_proj_call(
            hs1, w_hg, w_gm, logits, lng, lnb, _Row(w00_all, l), _Row(b00_all, l), layer=l, n=ns)
        fc = f.reshape(ns // bb, bb, HG_HEADS, HG_DK).transpose(0, 2, 3, 1)
        hg_s, a_s = _hgrn_step_call(state_hgrn, hg_s, fc, kt, q, v, xg, ng, layer=l, n=ns, bb=bb)
        h0 = (state_s5_re[l].reshape(ns, S5_N), state_s5_im[l].reshape(ns, S5_N))
        c_s, hs_re, hs_im = _s5_call(hs1, w_s5, s5p, s5d, gw, gb, h0, nb=ns, seq=1, tt=1)

        x2, xs2, *ffn_w = _merge_call(x1, h1, a_p, b_p, c_p, xs1, hs1, a_s, b_s, c_s, g[3],
                                      w_gate, (wb0, wb1, wb2), wout, ffn_riders(l, 1),
                                      n_p=n_p, tm=tm, xmap=tmajor, out_shape=tm_shape)
        last = l == depth - 1
        yp, ys, *ffn_w = _ffn_call(x2, xs2, g[4], g[5], g[5], *ffn_w,
                                   [] if last else ffn_riders(l + 1, 0), emit_h=False, n_p=n_p,
                                   tm=tm, in_map=tmajor, out_map=bmajor if last else tmajor,
                                   out_shape=(nb * seq, D_MODEL) if last else tm_shape)
        outs['re_s'].append(hs_re.reshape(ns, S5_GROUPS, S5_STATE))
        outs['im_s'].append(hs_im.reshape(ns, S5_GROUPS, S5_STATE))
        outs['v_s'].append(v_rows.reshape(ns, 1, GM_WIDTH))

    return (yp.reshape(nb, seq, D_MODEL), ys.reshape(ns, 1, D_MODEL),
            jnp.stack(outs['hg_p']), jnp.stack(outs['re_p']), jnp.stack(outs['im_p']),
            hg_s, jnp.stack(outs['re_s']), jnp.stack(outs['im_s']),
            jnp.stack(outs['v_s']))
```

```python
import functools
import math

import jax
import jax.numpy as jnp
from jax import lax
from jax.experimental import pallas as pl
from jax.experimental.pallas import tpu as pltpu

F32 = jnp.float32
BF16 = jnp.bfloat16

D_MODEL = 1024
D_FF = 2816
HG_HEADS = 4
HG_DK = 128
HG_WIDTH = HG_HEADS * HG_DK
GM_GROUPS = 4
GM_CHUNK = 128
GM_WIDTH = 512
S5_WIDTH = 512
S5_GROUP = 16
S5_GROUPS = S5_WIDTH // S5_GROUP
S5_STATE = 64
S5_N = S5_GROUPS * S5_STATE
S5_BLOCKS = 4
N_BRANCH = 3
EPS = 1e-6

COL_GM = 4 * HG_WIDTH
COL_S5 = COL_GM + 2 * GM_WIDTH
COL_GATE = COL_S5 + S5_WIDTH
N_IN = COL_GATE + N_BRANCH * D_MODEL

GLA_CHUNK = 32
FF_CHUNK = 512
TILE_PIECES = 2
GM_PIECES = 4
HG_PIECES = 2
VMEM_LIMIT = 56 * 1024 * 1024


def _rms(x, g):
    return x * lax.rsqrt(jnp.mean(x * x, axis=-1, keepdims=True) + EPS) * g


def _gelu(x):
    c = math.sqrt(2.0 / math.pi)
    hx = 0.5 * x
    return hx + hx * jnp.tanh(x * (c + (c * 0.044715) * (x * x)))


def _sigmoid(x):
    return 0.5 + 0.5 * jnp.tanh(0.5 * x)


def _dot(a, b):
    return jnp.dot(a, b, preferred_element_type=F32)


def _dot_nt(a, b):
    return lax.dot_general(a, b, (((1,), (1,)), ((), ())), preferred_element_type=F32)


def _dot_tn(a, b):
    return lax.dot_general(a, b, (((0,), (0,)), ((), ())), preferred_element_type=F32)


def _const_spec(shape):
    n = len(shape)
    return pl.BlockSpec(shape, lambda *_: (0,) * n, pipeline_mode=pl.Buffered(1))


class _Row:
    def __init__(self, arr, index):
        self.arr, self.index = arr, index

    @property
    def spec(self):
        tail = self.arr.shape[1:]
        index = self.index
        return pl.BlockSpec((None,) + tail, lambda *_: (index,) + (0,) * len(tail),
                            pipeline_mode=pl.Buffered(1))


def _params(n_grid):
    return pltpu.CompilerParams(
        dimension_semantics=("arbitrary",) * n_grid, vmem_limit_bytes=VMEM_LIMIT)


def _rider(src, lead=(), splits=None):
    k, n = src.shape[-2:]
    return dict(src=src, lead=tuple(lead), k=k, n=n, splits=tuple(splits or ((0, n),)))


def _rider_specs(riders, n_chunks):
    in_specs, out_specs, out_shapes = [], [], []
    for r in riders:
        rows = r['k'] // n_chunks
        assert rows * n_chunks == r['k'] and rows % 16 == 0, (r['k'], n_chunks)

        def src_map(i, lead=r['lead']):
            return lead + (jnp.minimum(i, n_chunks - 1), 0)

        in_specs.append(pl.BlockSpec((None,) * len(r['lead']) + (rows, r['n']), src_map))
        for c0, c1 in r['splits']:
            out_specs.append(pl.BlockSpec((rows, c1 - c0), lambda i: (jnp.minimum(i, n_chunks - 1), 0)))
            out_shapes.append(jax.ShapeDtypeStruct((r['k'], c1 - c0), BF16))
    return in_specs, out_specs, out_shapes


def _run_riders(src_refs, out_refs, splits):
    k = 0
    for src, sp in zip(src_refs, splits):
        for c0, c1 in sp:
            out_refs[k][...] = src[:, c0:c1].astype(BF16)
            k += 1


def _cast_kernel(splits, *refs):
    n = len(splits)
    _run_riders(refs[:n], refs[n:], splits)


def _cast_call(riders, n_chunks):
    in_specs, out_specs, out_shapes = _rider_specs(riders, n_chunks)
    return pl.pallas_call(
        functools.partial(_cast_kernel, tuple(r['splits'] for r in riders)),
        grid=(n_chunks,),
        in_specs=in_specs,
        out_specs=out_specs,
        out_shape=out_shapes,
        compiler_params=_params(1),
        name="cast",
    )(*[r['src'] for r in riders])


def _staggered(n_pieces, n_stages, begin, stage, end):
    for step in range(n_stages + n_pieces - 1):
        for p in range(n_pieces):
            s = step - p
            if s == 0:
                begin(p)
            if 0 <= s < n_stages:
                stage(p, s)
            if s == n_stages - 1:
                end(p)


def _ffn_tile(src_ref, dst_ref, h_ref, g_pre, g_post, g_next, wg_ref, wu_ref, wd_ref, n_pieces):
    rows = src_ref.shape[0] // n_pieces
    chunks = [(c0, min(c0 + FF_CHUNK, D_FF)) for c0 in range(0, D_FF, FF_CHUNK)]
    hs = [None] * n_pieces
    ds = [None] * n_pieces

    def begin(p):
        hs[p] = _rms(src_ref[p * rows:(p + 1) * rows, :], g_pre).astype(BF16)

    def stage(p, s):
        c0, c1 = chunks[s]
        a = _dot(hs[p], wg_ref[:, c0:c1])
        u = _dot(hs[p], wu_ref[:, c0:c1])
        m = (a * _sigmoid(a) * u).astype(BF16)
        part = _dot(m, wd_ref[c0:c1, :])
        ds[p] = part if ds[p] is None else ds[p] + part

    def end(p):
        r = slice(p * rows, (p + 1) * rows)
        y = src_ref[r, :] + 0.5 * _rms(ds[p], g_post)
        dst_ref[r, :] = y
        if h_ref is not None:
            h_ref[r, :] = _rms(y, g_next).astype(BF16)

    _staggered(n_pieces, len(chunks), begin, stage, end)


def _ffn_kernel(n_p, emit_h, splits, *refs):
    x_ref, xs_ref, gpre_ref, gpost_ref, gnext_ref, wg, wu, wd = refs[:8]
    n_r = len(splits)
    rider_src = refs[8:8 + n_r]
    n_out = 4 if emit_h else 2
    outs = refs[8 + n_r:8 + n_r + n_out]
    rider_out = refs[8 + n_r + n_out:]
    i = pl.program_id(0)
    _run_riders(rider_src, rider_out, splits)

    def run(src_ref, dst_ref, h_ref, n_pieces):
        _ffn_tile(src_ref, dst_ref, h_ref, gpre_ref[...], gpost_ref[...], gnext_ref[...],
                  wg, wu, wd, n_pieces)

    @pl.when(i < n_p)
    def _():
        run(x_ref, outs[0], outs[2] if emit_h else None, TILE_PIECES)

    @pl.when(i == n_p)
    def _():
        run(xs_ref, outs[1], outs[3] if emit_h else None, 1)


def _ffn_call(x2d, xs, g_pre, g_post, g_next, wg, wu, wd, riders, *, emit_h, n_p, tm, in_map,
              out_map, out_shape):
    ns = xs.shape[0]
    r_in, r_out, r_shapes = _rider_specs(riders, n_p)
    out_specs = [pl.BlockSpec((tm, D_MODEL), out_map), pl.BlockSpec((ns, D_MODEL), lambda i: (0, 0))]
    out_shapes = [jax.ShapeDtypeStruct(out_shape, F32), jax.ShapeDtypeStruct((ns, D_MODEL), F32)]
    if emit_h:
        out_specs = out_specs + out_specs
        out_shapes += [jax.ShapeDtypeStruct(out_shape, BF16), jax.ShapeDtypeStruct((ns, D_MODEL), BF16)]
    return pl.pallas_call(
        functools.partial(_ffn_kernel, n_p, emit_h, tuple(r['splits'] for r in riders)),
        grid=(n_p + 1,),
        in_specs=[
            pl.BlockSpec((tm, D_MODEL), in_map),
            _const_spec((ns, D_MODEL)),
            g_pre.spec, g_post.spec, g_next.spec,
            _const_spec((D_MODEL, D_FF)),
            _const_spec((D_MODEL, D_FF)),
            _const_spec((D_FF, D_MODEL)),
        ] + r_in,
        out_specs=out_specs + r_out,
        out_shape=out_shapes + r_shapes,
        compiler_params=_params(1),
        name="ffn",
    )(x2d, xs, g_pre.arr, g_post.arr, g_next.arr, wg, wu, wd, *[r['src'] for r in riders])


def _hgrn_gates(z, lb):
    ls = jnp.minimum(z, 0.0) - jnp.log1p(jnp.exp(-jnp.abs(z)))
    c = jnp.log1p(-lb) + ls
    a = jnp.log(jnp.maximum(lb, 1e-37))
    lae = jnp.maximum(a, c) + jnp.log1p(jnp.exp(-jnp.abs(a - c)))
    logf = jnp.where(lb > 0.0, lae, c)
    k = (1.0 - lb) * _sigmoid(-z)
    return logf, k


def _hgrn_out_head(oh, xg, ng):
    on = oh * lax.rsqrt(jnp.mean(oh * oh, axis=-1, keepdims=True) + EPS)
    return on * ng * (xg * _sigmoid(xg))


def _hgrn_out(o, xg, ng):
    heads = [slice(hd * HG_DK, (hd + 1) * HG_DK) for hd in range(HG_HEADS)]
    return jnp.concatenate([_hgrn_out_head(o[:, cs], xg[:, cs], ng[:, cs]) for cs in heads], axis=-1)


def _lower_bound(logits_ref, layer):
    lg = logits_ref[...]
    m = jnp.max(lg, axis=0, keepdims=True)
    e = jnp.exp(lg - m)
    den = jnp.sum(e, axis=0, keepdims=True)
    lb = jnp.zeros_like(den)
    for j in range(1, layer + 1):
        lb = lb + e[j:j + 1] / den
    return lb


def _hgrn_prompt_kernel(layer, nb, tl, h_ref, w_ref, logits_ref, ng_ref, tri_ref,
                        wgm_ref, lng_ref, lnb_ref, wm_ref, bst_ref,
                        a_ref, bmix_ref, sfin_ref, st_ref, q_s, k_s, v_s, b_s, g_s, o_s, h_s, bm_s):
    t = pl.program_id(0)
    nt = pl.num_programs(0)

    @pl.when(t == 0)
    def _():
        st_ref[...] = jnp.zeros_like(st_ref)

    lb = _lower_bound(logits_ref, layer)
    tri = tri_ref[...]
    tr = tri.shape[0]
    n_pieces = HG_PIECES
    seqs = nb // n_pieces
    rows = seqs * tl
    proj = [None] * n_pieces

    def stage(p, s):
        if s == 0:
            h = jnp.concatenate([h_ref[:, b * D_MODEL:(b + 1) * D_MODEL]
                                 for b in range(p * seqs, (p + 1) * seqs)], axis=0)
            proj[p] = _dot(h, w_ref[...])
            h_s[p * rows:(p + 1) * rows, :] = h
            return
        pp = proj[p]
        logf, kk = _hgrn_gates(pp[:, 1 * HG_WIDTH:2 * HG_WIDTH], lb)
        l0 = logf.astype(BF16)
        r1 = logf - l0.astype(F32)
        l1 = r1.astype(BF16)
        l2 = (r1 - l1.astype(F32)).astype(BF16)
        for i in range(rows // tr):
            rs = slice(i * tr, (i + 1) * tr)
            dst = slice(p * rows + i * tr, p * rows + (i + 1) * tr)
            b_s[dst, :] = _dot(tri, l0[rs]) + _dot(tri, l1[rs]) + _dot(tri, l2[rs])
        whole = slice(p * rows, (p + 1) * rows)
        q_s[whole, :] = pp[:, 0 * HG_WIDTH:1 * HG_WIDTH]
        k_s[whole, :] = kk
        v_s[whole, :] = pp[:, 2 * HG_WIDTH:3 * HG_WIDTH]
        g_s[whole, :] = pp[:, 3 * HG_WIDTH:4 * HG_WIDTH]

    _staggered(n_pieces, 2, lambda p: None, stage, lambda p: None)

    c = GLA_CHUNK
    mid = c // 2 - 1
    row = lax.broadcasted_iota(jnp.int32, (c, c), 0)
    col = lax.broadcasted_iota(jnp.int32, (c, c), 1)
    causal = col <= row
    ng = ng_ref[...]
    assert tl == GM_CHUNK, "one gMLP chunk per sequence and block"
    gm_rows = nb * tl // (tl // c)

    def chunk(ci, carry):
        gr = pl.ds(pl.multiple_of(ci * gm_rows, gm_rows), gm_rows)
        gm_proj = _dot(h_s[gr, :], wgm_ref[...])
        pending = []
        for b in range(nb):
            r = pl.ds(pl.multiple_of(b * tl + ci * c, c), c)
            for hd in range(HG_HEADS):
                cs = slice(hd * HG_DK, (hd + 1) * HG_DK)
                bq = b_s[r, cs]
                qc = q_s[r, cs]
                kc = k_s[r, cs]
                vc = v_s[r, cs].astype(BF16)
                b_mid = bq[mid:mid + 1]
                b_last = bq[c - 1:c]
                st = st_ref[b, hd]
                q_in = (qc * jnp.exp(bq)).astype(BF16)
                q_md = (qc * jnp.exp(bq - b_mid)).astype(BF16)
                k_md = (kc * jnp.exp(b_mid - bq)).astype(BF16)
                k_ls = (kc * jnp.exp(b_last - bq)).astype(BF16)
                scores = _dot_nt(q_md, k_md)
                o_inter = _dot_nt(q_in, st.astype(BF16))
                st_ref[b, hd] = st * jnp.exp(b_last) + _dot_tn(vc, k_ls)
                pending.append((r, cs, scores, o_inter, vc))
        u, v = _gmlp_act(gm_proj, lng_ref[...], lnb_ref[...])
        vb = v.astype(BF16)
        gc = GM_WIDTH // GM_GROUPS
        mixes = []
        for j in range(gm_rows // GM_CHUNK):
            parts = []
            for gi in range(GM_GROUPS):
                vj = vb[j * GM_CHUNK:(j + 1) * GM_CHUNK, gi * gc:(gi + 1) * gc]
                parts.append(_dot(wm_ref[gi], vj) + bst_ref[:, gi:gi + 1])
            mixes.append(jnp.concatenate(parts, axis=-1))
        bm_s[gr, :] = (u * jnp.concatenate(mixes, axis=0)).astype(BF16)
        for r, cs, scores, o_inter, vc in pending:
            sc = jnp.where(causal, scores, 0.0).astype(BF16)
            o = o_inter + _dot(sc, vc)
            o_s[r, cs] = _hgrn_out_head(o, g_s[r, cs], ng[:, cs]).astype(BF16)
        return carry

    lax.fori_loop(0, tl // c, chunk, 0)

    for b in range(nb):
        a_ref[:, b * HG_WIDTH:(b + 1) * HG_WIDTH] = o_s[b * tl:(b + 1) * tl, :]
        bmix_ref[:, b * GM_WIDTH:(b + 1) * GM_WIDTH] = bm_s[b * tl:(b + 1) * tl, :]

    @pl.when(t == nt - 1)
    def _():
        for b in range(nb):
            for hd in range(HG_HEADS):
                sfin_ref[b, hd] = st_ref[b, hd].T


def _hgrn_prompt_call(h_tm, w_hg, logits, ng, w_gm, lng, lnb, wm, bst, *, layer, nb, seq, tl):
    c = GLA_CHUNK
    tr = min(256, nb * tl)
    idx = jnp.arange(tr)
    tri = ((idx[:, None] >= idx[None, :]) & (idx[:, None] // c == idx[None, :] // c)).astype(BF16)
    depth = logits.shape[0]
    rows = nb * tl
    return pl.pallas_call(
        functools.partial(_hgrn_prompt_kernel, layer, nb, tl),
        grid=(seq // tl,),
        in_specs=[
            pl.BlockSpec((tl, nb * D_MODEL), lambda t: (t, 0)),
            _const_spec((D_MODEL, 4 * HG_WIDTH)),
            _const_spec((depth, HG_WIDTH)),
            ng.spec,
            _const_spec((tr, tr)),
            _const_spec((D_MODEL, 2 * GM_WIDTH)),
            lng.spec, lnb.spec, wm.spec, bst.spec,
        ],
        out_specs=[
            pl.BlockSpec((tl, nb * HG_WIDTH), lambda t: (t, 0)),
            pl.BlockSpec((tl, nb * GM_WIDTH), lambda t: (t, 0)),
            pl.BlockSpec((nb, HG_HEADS, HG_DK, HG_DK), lambda t: (0, 0, 0, 0)),
        ],
        out_shape=[
            jax.ShapeDtypeStruct((seq, nb * HG_WIDTH), BF16),
            jax.ShapeDtypeStruct((seq, nb * GM_WIDTH), BF16),
            jax.ShapeDtypeStruct((nb, HG_HEADS, HG_DK, HG_DK), F32),
        ],
        scratch_shapes=[
            pltpu.VMEM((nb, HG_HEADS, HG_DK, HG_DK), F32),
            pltpu.VMEM((rows, HG_WIDTH), F32),
            pltpu.VMEM((rows, HG_WIDTH), F32),
            pltpu.VMEM((rows, HG_WIDTH), F32),
            pltpu.VMEM((rows, HG_WIDTH), F32),
            pltpu.VMEM((rows, HG_WIDTH), F32),
            pltpu.VMEM((rows, HG_WIDTH), BF16),
            pltpu.VMEM((rows, D_MODEL), BF16),
            pltpu.VMEM((rows, GM_WIDTH), BF16),
        ],
        compiler_params=_params(1),
        name="hgrn_prompt",
    )(h_tm, w_hg, logits, ng.arr, tri, w_gm, lng.arr, lnb.arr, wm.arr, bst.arr)


def _gmlp_act(p, lng, lnb):
    u = _gelu(p[:, :GM_WIDTH])
    gv = _gelu(p[:, GM_WIDTH:])
    mu = jnp.mean(gv, axis=-1, keepdims=True)
    dv = gv - mu
    var = jnp.mean(dv * dv, axis=-1, keepdims=True)
    v = dv * lax.rsqrt(var + EPS) * lng + lnb
    return u, v


def _gmlp_uv(h, w_ref, lng, lnb):
    return _gmlp_act(_dot(h, w_ref[...]), lng, lnb)


def _gmlp_prompt_kernel(tg, h_ref, w_ref, lng_ref, lnb_ref, wm_ref, bst_ref, o_ref):
    n_pieces = GM_PIECES
    rows = tg // n_pieces
    gc = GM_WIDTH // GM_GROUPS
    proj = [None] * n_pieces
    uv = [None] * n_pieces

    def stage(p, s):
        r = slice(p * rows, (p + 1) * rows)
        if s == 0:
            proj[p] = _dot(h_ref[r, :], w_ref[...])
        elif s == 1:
            uv[p] = _gmlp_act(proj[p], lng_ref[...], lnb_ref[...])
        else:
            u, v = uv[p]
            vb = v.astype(BF16)
            chunks = []
            for j in range(rows // GM_CHUNK):
                parts = []
                for gi in range(GM_GROUPS):
                    vj = vb[j * GM_CHUNK:(j + 1) * GM_CHUNK, gi * gc:(gi + 1) * gc]
                    parts.append(_dot(wm_ref[gi], vj) + bst_ref[:, gi:gi + 1])
                chunks.append(jnp.concatenate(parts, axis=-1))
            mix = jnp.concatenate(chunks, axis=0)
            o_ref[r, :] = (u * mix).astype(BF16)

    _staggered(n_pieces, 3, lambda p: None, stage, lambda p: None)


def _gmlp_prompt_call(h_tm, w_gm, lng, lnb, wm, bst, *, nb, seq, tg):
    return pl.pallas_call(
        functools.partial(_gmlp_prompt_kernel, tg),
        grid=(nb, seq // tg),
        in_specs=[
            pl.BlockSpec((tg, D_MODEL), lambda b, t: (t, b)),
            _const_spec((D_MODEL, 2 * GM_WIDTH)),
            lng.spec, lnb.spec, wm.spec, bst.spec,
        ],
        out_specs=pl.BlockSpec((tg, GM_WIDTH), lambda b, t: (t, b)),
        out_shape=jax.ShapeDtypeStruct((seq, nb * GM_WIDTH), BF16),
        compiler_params=_params(2),
        name="gmlp_prompt",
    )(h_tm, w_gm, lng.arr, lnb.arr, wm.arr, bst.arr)


def _s5_kernel(nb, tt, has_h0, relayout, *refs):
    (x_ref, w_ref, are_ref, aim_ref, bre_ref, bim_ref, cre_ref, cim_ref, d_ref,
     gw_ref, gb_ref) = refs[:11]
    h0re_ref, h0im_ref = refs[11:13] if has_h0 else (None, None)
    c_ref, hre_ref, him_ref, sre, sim, lay = refs[-6:]
    t = pl.program_id(0)
    wblk = S5_WIDTH // S5_BLOCKS
    sblk = S5_N // S5_BLOCKS

    @pl.when(t == 0)
    def _():
        if has_h0:
            hre_ref[...] = h0re_ref[...]
            him_ref[...] = h0im_ref[...]
        else:
            hre_ref[...] = jnp.zeros_like(hre_ref)
            him_ref[...] = jnp.zeros_like(him_ref)

    if relayout:
        h = jnp.concatenate([x_ref[:, b * D_MODEL:(b + 1) * D_MODEL] for b in range(nb)], axis=0)
    else:
        h = x_ref[...]
    su = _dot(h, w_ref[...])
    if relayout:
        for b in range(nb):
            for j in range(S5_BLOCKS):
                lay[j, pl.ds(b, tt, stride=nb), :] = su[b * tt:(b + 1) * tt, j * wblk:(j + 1) * wblk]
        su = jnp.concatenate([lay[j] for j in range(S5_BLOCKS)], axis=-1)
    sub = su.astype(BF16)

    def project(j):
        uj = sub[:, j * wblk:(j + 1) * wblk]
        sre[:, j * sblk:(j + 1) * sblk] = _dot(uj, bre_ref[j])
        sim[:, j * sblk:(j + 1) * sblk] = _dot(uj, bim_ref[j])

    def scan(j):
        cols = slice(j * sblk, (j + 1) * sblk)
        a_re = jnp.broadcast_to(are_ref[:, cols], (nb, sblk))
        a_im = jnp.broadcast_to(aim_ref[:, cols], (nb, sblk))
        h_re = hre_ref[:, cols]
        h_im = him_ref[:, cols]
        for i in range(tt):
            r = slice(i * nb, (i + 1) * nb)
            n_re = a_re * h_re - a_im * h_im + sre[r, cols]
            n_im = a_re * h_im + a_im * h_re + sim[r, cols]
            sre[r, cols] = n_re
            sim[r, cols] = n_im
            h_re, h_im = n_re, n_im
        hre_ref[:, cols] = h_re
        him_ref[:, cols] = h_im

    def readout(j):
        hr = sre[:, j * sblk:(j + 1) * sblk].astype(BF16)
        hi = sim[:, j * sblk:(j + 1) * sblk].astype(BF16)
        return _dot(hr, cre_ref[j]) - _dot(hi, cim_ref[j])

    ys = []
    project(0)
    for j in range(S5_BLOCKS):
        if j + 1 < S5_BLOCKS:
            project(j + 1)
        scan(j)
        ys.append(readout(j))
    y = jnp.concatenate(ys, axis=-1) + d_ref[...] * su
    zz = _dot(_gelu(y).astype(BF16), gw_ref[...]) + gb_ref[...]
    c = zz[:, :S5_WIDTH] * _sigmoid(zz[:, S5_WIDTH:])
    if relayout:
        for j in range(S5_BLOCKS):
            lay[j] = c[:, j * wblk:(j + 1) * wblk]
        for b in range(nb):
            cb = jnp.concatenate([lay[j, pl.ds(b, tt, stride=nb), :] for j in range(S5_BLOCKS)], axis=-1)
            c_ref[:, b * S5_WIDTH:(b + 1) * S5_WIDTH] = cb.astype(BF16)
    else:
        c_ref[...] = c.astype(BF16)


def _s5_call(h2d, w_s5, s5p, d, gw, gb, h0=None, *, nb, seq, tt):
    rows = tt * nb
    relayout = seq > 1
    wblk = S5_WIDTH // S5_BLOCKS
    sblk = S5_N // S5_BLOCKS
    x_block = (tt, nb * D_MODEL) if relayout else (rows, D_MODEL)
    c_block = (tt, nb * S5_WIDTH) if relayout else (rows, S5_WIDTH)
    in_specs = [
        pl.BlockSpec(x_block, lambda t: (t, 0)),
        _const_spec((D_MODEL, S5_WIDTH)),
    ] + [p.spec for p in s5p] + [
        d.spec,
        _const_spec((S5_WIDTH, 2 * S5_WIDTH)),
        gb.spec,
    ]
    args = [h2d, w_s5] + [p.arr for p in s5p] + [d.arr, gw, gb.arr]
    if h0 is not None:
        in_specs += [_const_spec((nb, S5_N)), _const_spec((nb, S5_N))]
        args += list(h0)
    state_spec = pl.BlockSpec((nb, S5_N), lambda t: (0, 0))
    c_shape = (seq, nb * S5_WIDTH) if relayout else (nb, S5_WIDTH)
    return pl.pallas_call(
        functools.partial(_s5_kernel, nb, tt, h0 is not None, relayout),
        grid=(seq // tt,),
        in_specs=in_specs,
        out_specs=[pl.BlockSpec(c_block, lambda t: (t, 0)), state_spec, state_spec],
        out_shape=[
            jax.ShapeDtypeStruct(c_shape, BF16),
            jax.ShapeDtypeStruct((nb, S5_N), F32),
            jax.ShapeDtypeStruct((nb, S5_N), F32),
        ],
        scratch_shapes=[pltpu.VMEM((rows, S5_N), F32), pltpu.VMEM((rows, S5_N), F32),
                        pltpu.VMEM((S5_BLOCKS, rows, wblk), F32)],
        compiler_params=_params(1),
        name="s5",
    )(*args)


def _merge_tile(x_ref, h_ref, br_refs, dst_ref, g3, wgate, wbr, wout, n_pieces):
    rows = x_ref.shape[0] // n_pieces
    merged = [None] * n_pieces
    proj = [None] * n_pieces

    def begin(p):
        pass

    def stage(p, s):
        r = slice(p * rows, (p + 1) * rows)
        if s < N_BRANCH:
            gate = _sigmoid(_dot(h_ref[r, :], wgate[:, s * D_MODEL:(s + 1) * D_MODEL]))
            term = gate * _dot(br_refs[s][r, :], wbr[s][...])
            merged[p] = term if merged[p] is None else merged[p] + term
        else:
            proj[p] = _dot(merged[p].astype(BF16), wout[...])

    def end(p):
        r = slice(p * rows, (p + 1) * rows)
        dst_ref[r, :] = x_ref[r, :] + _rms(proj[p], g3)

    _staggered(n_pieces, N_BRANCH + 1, begin, stage, end)


def _merge_kernel(n_p, splits, *refs):
    (x_ref, h_ref, a_ref, b_ref, c_ref, xs_ref, hs_ref, as_ref, bs_ref, cs_ref, g3_ref,
     wgate, wb0, wb1, wb2, wout) = refs[:16]
    n_r = len(splits)
    rider_src = refs[16:16 + n_r]
    o_ref, os_ref = refs[16 + n_r:18 + n_r]
    rider_out = refs[18 + n_r:]
    wbr = (wb0, wb1, wb2)
    i = pl.program_id(0)
    _run_riders(rider_src, rider_out, splits)

    @pl.when(i < n_p)
    def _():
        _merge_tile(x_ref, h_ref, (a_ref, b_ref, c_ref), o_ref, g3_ref[...], wgate, wbr, wout,
                    TILE_PIECES)

    @pl.when(i == n_p)
    def _():
        _merge_tile(xs_ref, hs_ref, (as_ref, bs_ref, cs_ref), os_ref, g3_ref[...], wgate, wbr, wout, 1)


def _merge_call(x2d, h2d, a, b, c, xs, hs, a_s, b_s, c_s, g3, wgate, wbr, wout, riders,
                *, n_p, tm, xmap, out_shape):
    ns = xs.shape[0]
    r_in, r_out, r_shapes = _rider_specs(riders, n_p)
    return pl.pallas_call(
        functools.partial(_merge_kernel, n_p, tuple(r['splits'] for r in riders)),
        grid=(n_p + 1,),
        in_specs=[
            pl.BlockSpec((tm, D_MODEL), xmap),
            pl.BlockSpec((tm, D_MODEL), xmap),
            pl.BlockSpec((tm, HG_WIDTH), xmap),
            pl.BlockSpec((tm, GM_WIDTH), xmap),
            pl.BlockSpec((tm, S5_WIDTH), xmap),
            _const_spec((ns, D_MODEL)),
            _const_spec((ns, D_MODEL)),
            _const_spec((ns, HG_WIDTH)),
            _const_spec((ns, GM_WIDTH)),
            _const_spec((ns, S5_WIDTH)),
            g3.spec,
            _const_spec((D_MODEL, N_BRANCH * D_MODEL)),
            _const_spec((HG_WIDTH, D_MODEL)),
            _const_spec((GM_WIDTH, D_MODEL)),
            _const_spec((S5_WIDTH, D_MODEL)),
            _const_spec((D_MODEL, D_MODEL)),
        ] + r_in,
        out_specs=[pl.BlockSpec((tm, D_MODEL), xmap),
                   pl.BlockSpec((ns, D_MODEL), lambda i: (0, 0))] + r_out,
        out_shape=[jax.ShapeDtypeStruct(out_shape, F32),
                   jax.ShapeDtypeStruct((ns, D_MODEL), F32)] + r_shapes,
        compiler_params=_params(1),
        name="merge",
    )(x2d, h2d, a, b, c, xs, hs, a_s, b_s, c_s, g3.arr, wgate, *wbr, wout, *[r['src'] for r in riders])


def _sample_proj_kernel(layer, h_ref, whg_ref, wgm_ref, logits_ref, lng_ref, lnb_ref,
                        w00_ref, b00_ref, q_ref, f_ref, kt_ref, v_ref, xg_ref, bout_ref, vrow_ref):
    h = h_ref[...]
    p = _dot(h, whg_ref[...])
    z = p[:, 1 * HG_WIDTH:2 * HG_WIDTH]
    logf, kk = _hgrn_gates(z, _lower_bound(logits_ref, layer))
    q_ref[...] = p[:, 0 * HG_WIDTH:1 * HG_WIDTH]
    f_ref[...] = jnp.exp(logf)
    for hd in range(HG_HEADS):
        kt_ref[hd] = kk[:, hd * HG_DK:(hd + 1) * HG_DK].T.astype(BF16)
    v_ref[...] = p[:, 2 * HG_WIDTH:3 * HG_WIDTH]
    xg_ref[...] = p[:, 3 * HG_WIDTH:4 * HG_WIDTH]
    u, v = _gmlp_uv(h, wgm_ref, lng_ref[...], lnb_ref[...])
    bout_ref[...] = (u * (v * w00_ref[...] + b00_ref[...])).astype(BF16)
    vrow_ref[...] = v


def _sample_proj_call(h, w_hg, w_gm, logits, lng, lnb, w00, b00, *, layer, n):
    depth = logits.shape[0]
    assert n == HG_DK, "per-head key transpose is written for a square (sequences, keys) tile"
    f32_out = jax.ShapeDtypeStruct((n, HG_WIDTH), F32)
    row_spec = pl.BlockSpec((n, HG_WIDTH), lambda i: (0, 0))
    return pl.pallas_call(
        functools.partial(_sample_proj_kernel, layer),
        grid=(1,),
        in_specs=[
            _const_spec((n, D_MODEL)),
            _const_spec((D_MODEL, 4 * HG_WIDTH)),
            _const_spec((D_MODEL, 2 * GM_WIDTH)),
            _const_spec((depth, HG_WIDTH)),
            lng.spec, lnb.spec, w00.spec, b00.spec,
        ],
        out_specs=[row_spec, row_spec, pl.BlockSpec((HG_HEADS, HG_DK, n), lambda i: (0, 0, 0)),
                   row_spec, row_spec, row_spec, row_spec],
        out_shape=[f32_out, f32_out, jax.ShapeDtypeStruct((HG_HEADS, HG_DK, n), BF16),
                   f32_out, f32_out, jax.ShapeDtypeStruct((n, GM_WIDTH), BF16), f32_out],
        compiler_params=_params(1),
        name="sample_proj",
    )(h, w_hg, w_gm, logits, lng.arr, lnb.arr, w00.arr, b00.arr)


def _hgrn_step_kernel(bb, layer, whole_stack, *refs):
    s_ref, fc_ref, kt_ref, q_ref, v_ref, xg_ref, ng_ref = refs[:7]
    snew_ref, a_ref, o_s = refs[-3:]
    if whole_stack:
        for j in range(snew_ref.shape[0]):
            if j != layer:
                snew_ref[j] = jnp.zeros(snew_ref.shape[1:], F32)
        snew_ref = snew_ref.at[layer]
    n = kt_ref.shape[-1]
    seq_id = lax.broadcasted_iota(jnp.int32, (n, HG_DK), 0)
    first = pl.program_id(0) * bb
    for i in range(bb):
        here = seq_id == first + i
        for hd in range(HG_HEADS):
            cs = slice(hd * HG_DK, (hd + 1) * HG_DK)
            v_only = jnp.where(here, jnp.broadcast_to(v_ref[i:i + 1, cs], (n, HG_DK)), 0.0)
            kv = _dot(kt_ref[hd], v_only.astype(BF16))
            f_col = fc_ref[0, hd, :, i:i + 1]
            s_new = f_col * s_ref[i, hd] + kv
            snew_ref[i, hd] = s_new
            q8 = jnp.broadcast_to(q_ref[i:i + 1, cs], (8, HG_DK)).astype(BF16)
            o_s[i:i + 1, cs] = _dot(q8, s_new.astype(BF16))[0:1]
    a_ref[...] = _hgrn_out(o_s[...], xg_ref[...], ng_ref[...]).astype(BF16)


def _hgrn_step_call(s_all, prev, fc, kt, q, v, xg, ng, *, layer, n, bb):
    col_spec = pl.BlockSpec((1, HG_HEADS, HG_DK, bb), lambda i: (i, 0, 0, 0))
    row_spec = pl.BlockSpec((bb, HG_WIDTH), lambda i: (i, 0))
    st_spec = pl.BlockSpec((None, bb, HG_HEADS, HG_DK, HG_DK), lambda i: (layer, i, 0, 0, 0))
    in_specs = [st_spec, col_spec, _const_spec((HG_HEADS, HG_DK, n)), row_spec, row_spec, row_spec,
                ng.spec]
    args = [s_all, fc, kt, q, v, xg, ng.arr]
    aliases = {}
    out_st_spec = st_spec
    if prev is not None:
        in_specs.append(pl.BlockSpec(memory_space=pl.ANY))
        args.append(prev)
        aliases = {len(args) - 1: 0}
    else:
        out_st_spec = pl.BlockSpec((s_all.shape[0], bb, HG_HEADS, HG_DK, HG_DK), lambda i: (0, i, 0, 0, 0))
    return pl.pallas_call(
        functools.partial(_hgrn_step_kernel, bb, layer, prev is None),
        grid=(n // bb,),
        in_specs=in_specs,
        out_specs=[out_st_spec, row_spec],
        out_shape=[
            jax.ShapeDtypeStruct(s_all.shape, F32),
            jax.ShapeDtypeStruct((n, HG_WIDTH), BF16),
        ],
        scratch_shapes=[pltpu.VMEM((bb, HG_WIDTH), F32)],
        input_output_aliases=aliases,
        compiler_params=_params(1),
        name="hgrn_step",
    )(*args)


def _s5_params(lam_re, lam_im, log_dt, b_re, b_im, c_re, c_im):
    depth = lam_re.shape[0]
    lr = lam_re.astype(F32)
    li = lam_im.astype(F32)
    dt = jnp.exp(log_dt.astype(F32))[..., None]
    mag = jnp.exp(lr * dt)
    a_re = mag * jnp.cos(li * dt)
    a_im = mag * jnp.sin(li * dt)
    den = lr * lr + li * li
    k_re = ((a_re - 1.0) * lr + a_im * li) / den
    k_im = (a_im * lr - (a_re - 1.0) * li) / den
    bre = b_re.astype(F32)
    bim = b_im.astype(F32)
    bb_re = k_re[..., None] * bre - k_im[..., None] * bim
    bb_im = k_re[..., None] * bim + k_im[..., None] * bre
    gpb = S5_GROUPS // S5_BLOCKS
    eye = jnp.eye(gpb, dtype=F32)

    def pack_in(w):
        w = w.reshape(depth, S5_BLOCKS, gpb, S5_STATE, S5_GROUP)
        out = jnp.einsum('ljgph,gk->ljghkp', w, eye)
        return out.reshape(depth, S5_BLOCKS, gpb * S5_GROUP, gpb * S5_STATE).astype(BF16)

    def pack_out(w):
        w = w.reshape(depth, S5_BLOCKS, gpb, S5_GROUP, S5_STATE)
        out = jnp.einsum('ljghp,gk->ljgpkh', w, eye)
        return out.reshape(depth, S5_BLOCKS, gpb * S5_STATE, gpb * S5_GROUP).astype(BF16)

    return (a_re.reshape(depth, 1, S5_N), a_im.reshape(depth, 1, S5_N), pack_in(bb_re), pack_in(bb_im),
            pack_out(c_re.astype(F32)), pack_out(c_im.astype(F32)))


def kernel(x_prompt, x_sample, state_hgrn, state_s5_re, state_s5_im, norm_g, ffn_w_gate, ffn_w_up, ffn_w_down, w_in, hgrn_lb_logits, hgrn_norm_g, gmlp_ws, gmlp_bs, gmlp_norm_g, gmlp_norm_b, s5_lam_re, s5_lam_im, s5_log_dt, s5_b_re, s5_b_im, s5_c_re, s5_c_im, s5_d, s5_glu_w, s5_glu_b, w_branch, w_out):
    nb, seq, _ = x_prompt.shape
    ns = x_sample.shape[0]
    depth = norm_g.shape[0]
    tm = 1024
    tl = 128
    tg = 1024
    tt = 128
    bb = 16
    nt = seq // tm
    n_p = nb * nt

    def bmajor(i):
        return (jnp.minimum(i, n_p - 1), 0)

    def tmajor(i):
        j = jnp.minimum(i, n_p - 1)
        return (j % nt, j // nt)

    def ffn_riders(l, which):
        return [_rider(ffn_w_gate, (l, which)), _rider(ffn_w_up, (l, which)),
                _rider(ffn_w_down, (l, which))]

    def mixer_riders(l):
        in_splits = ((0, COL_GM), (COL_GM, COL_S5), (COL_S5, COL_GATE), (COL_GATE, N_IN))
        return ([_rider(w_in, (l,), in_splits)]
                + [_rider(w_branch, (l, n)) for n in range(N_BRANCH)]
                + [_rider(w_out, (l,)), _rider(s5_glu_w, (l,))])

    n_norm = norm_g.shape[1]
    g_all = norm_g.reshape(depth * n_norm, 1, D_MODEL)
    ng_all = hgrn_norm_g.reshape(depth, 1, HG_WIDTH)
    lng_all = gmlp_norm_g.reshape(depth, 1, GM_WIDTH)
    lnb_all = gmlp_norm_b.reshape(depth, 1, GM_WIDTH)
    tril = jnp.tril(jnp.ones((GM_CHUNK, GM_CHUNK), F32))
    wm_all = (gmlp_ws * tril).astype(BF16)
    bst_all = gmlp_bs.transpose(0, 2, 1)
    gc = GM_WIDTH // GM_GROUPS
    w00_all = jnp.repeat(gmlp_ws[:, :, 0, 0], gc, axis=1).reshape(depth, 1, GM_WIDTH)
    b00_all = jnp.repeat(gmlp_bs[:, :, 0], gc, axis=1).reshape(depth, 1, GM_WIDTH)
    s5p_all = _s5_params(s5_lam_re, s5_lam_im, s5_log_dt, s5_b_re, s5_b_im, s5_c_re, s5_c_im)
    s5d_all = s5_d.reshape(depth, 1, S5_WIDTH)
    gb_all = s5_glu_b.reshape(depth, 1, 2 * S5_WIDTH)
    logits = hgrn_lb_logits.astype(F32)

    yp = x_prompt.reshape(nb * seq, D_MODEL)
    ys = x_sample.reshape(ns, D_MODEL)
    ffn_w = _cast_call(ffn_riders(0, 0), n_p)

    outs = {k: [] for k in ('hg_p', 're_p', 'im_p', 're_s', 'im_s', 'v_s')}
    hg_s = None
    for l in range(depth):
        g = [_Row(g_all, l * n_norm + i) for i in range(n_norm)]
        ng, lng, lnb = _Row(ng_all, l), _Row(lng_all, l), _Row(lnb_all, l)
        s5p = [_Row(p, l) for p in s5p_all]
        s5d, gb = _Row(s5d_all, l), _Row(gb_all, l)
        tm_shape = (seq, nb * D_MODEL)

        x1, xs1, h1, hs1, w_hg, w_gm, w_s5, w_gate, wb0, wb1, wb2, wout, gw = _ffn_call(
            yp, ys, g[0], g[1], g[2], *ffn_w, mixer_riders(l), emit_h=True, n_p=n_p, tm=tm,
            in_map=bmajor if l == 0 else tmajor, out_map=tmajor, out_shape=tm_shape)
        a_p, b_p, s_hg = _hgrn_prompt_call(h1, w_hg, logits, ng, w_gm, lng, lnb, _Row(wm_all, l),
                                           _Row(bst_all, l), layer=l, nb=nb, seq=seq, tl=tl)
        c_p, h_re, h_im = _s5_call(h1, w_s5, s5p, s5d, gw, gb, nb=nb, seq=seq, tt=tt)
        outs['hg_p'].append(s_hg)
        outs['re_p'].append(h_re.reshape(nb, S5_GROUPS, S5_STATE))
        outs['im_p'].append(h_im.reshape(nb, S5_GROUPS, S5_STATE))

        q, f, kt, v, xg, b_s, v_rows = _sample_proj_call(
            hs1, w_hg, w_gm, logits, lng, lnb, _Row(w00_all, l), _Row(b00_all, l), layer=l, n=ns)
        fc = f.reshape(ns // bb, bb, HG_HEADS, HG_DK).transpose(0, 2, 3, 1)
        hg_s, a_s = _hgrn_step_call(state_hgrn, hg_s, fc, kt, q, v, xg, ng, layer=l, n=ns, bb=bb)
        h0 = (state_s5_re[l].reshape(ns, S5_N), state_s5_im[l].reshape(ns, S5_N))
        c_s, hs_re, hs_im = _s5_call(hs1, w_s5, s5p, s5d, gw, gb, h0, nb=ns, seq=1, tt=1)

        x2, xs2, *ffn_w = _merge_call(x1, h1, a_p, b_p, c_p, xs1, hs1, a_s, b_s, c_s, g[3],
                                      w_gate, (wb0, wb1, wb2), wout, ffn_riders(l, 1),
                                      n_p=n_p, tm=tm, xmap=tmajor, out_shape=tm_shape)
        last = l == depth - 1
        yp, ys, *ffn_w = _ffn_call(x2, xs2, g[4], g[5], g[5], *ffn_w,
                                   [] if last else ffn_riders(l + 1, 0), emit_h=False, n_p=n_p,
                                   tm=tm, in_map=tmajor, out_map=bmajor if last else tmajor,
                                   out_shape=(nb * seq, D_MODEL) if last else tm_shape)
        outs['re_s'].append(hs_re.reshape(ns, S5_GROUPS, S5_STATE))
        outs['im_s'].append(hs_im.reshape(ns, S5_GROUPS, S5_STATE))
        outs['v_s'].append(v_rows.reshape(ns, 1, GM_WIDTH))

    return (yp.reshape(nb, seq, D_MODEL), ys.reshape(ns, 1, D_MODEL),
            jnp.stack(outs['hg_p']), jnp.stack(outs['re_p']), jnp.stack(outs['im_p']),
            hg_s, jnp.stack(outs['re_s']), jnp.stack(outs['im_s']),
            jnp.stack(outs['v_s']))
```
